```python
import math
import jax, jax.numpy as jnp
from jax import lax
import numpy as np

D_MODEL = 4096
BATCH = 1
SEQ = 8192
DEPTH = 4

HEAD_DIM = 128
Q_BLOCK = 128
SB_HEADS = (D_MODEL // 2) // HEAD_DIM
SB_W = SB_HEADS * HEAD_DIM
RET_QK_DIM = 128
RET_V_DIM = 256
RET_HEADS = (D_MODEL // 2) // RET_V_DIM
RET_QK_W = RET_HEADS * RET_QK_DIM
RET_V_W = RET_HEADS * RET_V_DIM
RET_CHUNK = 128
RET_THETA = 10000.0
DSA_Q_HEADS = D_MODEL // HEAD_DIM
DSA_KV_HEADS = DSA_Q_HEADS // 4
DSA_Q_W = DSA_Q_HEADS * HEAD_DIM
DSA_KV_W = DSA_KV_HEADS * HEAD_DIM
IDX_HEADS = 16
IDX_DIM = 64
IDX_Q_W = IDX_HEADS * IDX_DIM
IDX_TOPK_MAX = 256
ROPE_THETA = 500000.0
ROT_FRACTION = 4
D_FF = 11008
N_EXPERTS = 8
TOP_K_EXPERTS = 2
D_FF_EXPERT = 1792
LN_EPS = 1e-5
DEEPNORM_ALPHA = (2.0 * DEPTH) ** 0.25
DEEPNORM_BETA = (8.0 * DEPTH) ** -0.25

EVEN_SPLITS = (SB_W, SB_W, SB_W, RET_QK_W, RET_QK_W, RET_V_W, RET_V_W)
EVEN_IN = sum(EVEN_SPLITS)
ODD_SPLITS = (DSA_Q_W, DSA_KV_W, DSA_KV_W, IDX_Q_W, IDX_DIM, IDX_HEADS)
ODD_IN = sum(ODD_SPLITS)

kernel_name = "hybrid_stickbreak_retention_dsa_moe_deepnorm"

F32 = jnp.float32


def _split(p, sizes):
    return jnp.split(p, np.cumsum(np.array(sizes))[:-1].tolist(), axis=-1)


def layer_norm(x, g, b):
    xf = x.astype(F32)
    mu = jnp.mean(xf, axis=-1, keepdims=True)
    var = jnp.mean(jnp.square(xf - mu), axis=-1, keepdims=True)
    return ((xf - mu) * lax.rsqrt(var + LN_EPS) * g.astype(F32) + b.astype(F32)).astype(x.dtype)


def rope(x, rot_dim, theta):
    S = x.shape[1]
    half = rot_dim // 2
    pos = jnp.arange(S, dtype=F32)
    freqs = jnp.exp(-math.log(theta) * jnp.arange(half, dtype=F32) * (2.0 / rot_dim))
    ang = (pos[:, None] * freqs[None, :]).reshape((S,) + (1,) * (x.ndim - 3) + (half,))
    cos, sin = jnp.cos(ang).astype(x.dtype), jnp.sin(ang).astype(x.dtype)
    x1, x2, rest = x[..., :half], x[..., half:rot_dim], x[..., rot_dim:]
    return jnp.concatenate([x1 * cos - x2 * sin, x1 * sin + x2 * cos, rest], axis=-1)


def stick_breaking_attention(q, k, v):
    S, H, d = q.shape
    nb = S // Q_BLOCK
    kf, vf = k.astype(F32), v.astype(F32)
    key_pos = jnp.arange(S)
    qb = q.astype(F32).reshape(nb, Q_BLOCK, H, d)
    starts = jnp.arange(nb) * Q_BLOCK
    scale = d ** -0.5

    def block(args):
        q_blk, start = args
        q_pos = start + jnp.arange(Q_BLOCK)
        z = jnp.einsum('qhd,shd->hqs', q_blk, kf) * scale
        mask = key_pos[None, :] < q_pos[:, None]
        log_keep = jnp.where(mask, -jax.nn.softplus(z), 0.0)
        after = lax.cumsum(log_keep, axis=2, reverse=True) - log_keep
        w = jnp.where(mask, jnp.exp(jax.nn.log_sigmoid(z) + after), 0.0)
        return jnp.einsum('hqs,shd->qhd', w, vf)

    out = lax.map(block, (qb, starts))
    return out.reshape(S, H * d)


def retention_chunkwise(q, k, v):
    S, H, dk = q.shape
    dv = v.shape[-1]
    C = RET_CHUNK
    n = S // C
    log_gamma = jnp.log1p(-jnp.exp2(-5.0 - jnp.arange(H, dtype=F32)))
    qc = q.astype(F32).reshape(n, C, H, dk)
    kc = (k.astype(F32) * dk ** -0.5).reshape(n, C, H, dk)
    vc = v.astype(F32).reshape(n, C, H, dv)
    i = jnp.arange(C, dtype=F32)
    rel = i[:, None] - i[None, :]
    decay = jnp.where(rel >= 0, jnp.exp(log_gamma[:, None, None] * jnp.maximum(rel, 0.0)), 0.0)
    inner = jnp.einsum('nihd,njhd->nhij', qc, kc) * decay
    inner = jnp.einsum('nhij,njhv->nihv', inner, vc)
    k_decay = jnp.exp(log_gamma[:, None] * (C - 1.0 - i))
    q_decay = jnp.exp(log_gamma[:, None] * (i + 1.0))
    chunk_state = jnp.einsum('njhd,hj,njhv->nhdv', kc, k_decay, vc)
    chunk_decay = jnp.exp(log_gamma * C)[:, None, None]

    def step(state, cs):
        return chunk_decay * state + cs, state

    _, prev = lax.scan(step, jnp.zeros((H, dk, dv), F32), chunk_state)
    cross = jnp.einsum('nihd,hi,nhdv->nihv', qc, q_decay, prev)
    return (inner + cross).reshape(S, H, dv)


def head_group_norm(r):
    mu = jnp.mean(r, axis=-1, keepdims=True)
    var = jnp.mean(jnp.square(r - mu), axis=-1, keepdims=True)
    return (r - mu) * lax.rsqrt(var + LN_EPS)


def dsa_attention(q, k, v, qi, ki, wi):
    S, Hq, d = q.shape
    Hkv = k.shape[1]
    Hi, di = qi.shape[1], qi.shape[2]
    grp = Hq // Hkv
    topk = min(IDX_TOPK_MAX, S // 4)
    nb = S // Q_BLOCK
    key_pos = jnp.arange(S)
    kf, vf = k.astype(F32), v.astype(F32)
    kif = ki.astype(F32) * di ** -0.5
    qb = q.astype(F32).reshape(nb, Q_BLOCK, Hkv, grp, d)
    qib = qi.astype(F32).reshape(nb, Q_BLOCK, Hi, di)
    wib = (wi.astype(F32) * Hi ** -0.5).reshape(nb, Q_BLOCK, Hi)
    starts = jnp.arange(nb) * Q_BLOCK
    scale = d ** -0.5

    def block(args):
        q_blk, qi_blk, w_blk, start = args
        q_pos = start + jnp.arange(Q_BLOCK)
        s_idx = jax.nn.relu(jnp.einsum('qhd,sd->qhs', qi_blk, kif))
        score = jnp.einsum('qhs,qh->qs', s_idx, w_blk)
        score = jnp.where(key_pos[None, :] <= q_pos[:, None], score, -jnp.inf)
        _, idx = lax.top_k(score, topk)
        valid = idx <= q_pos[:, None]
        k_sel = kf[idx]
        v_sel = vf[idx]
        logits = jnp.einsum('qgrd,qkgd->qgrk', q_blk, k_sel) * scale
        logits = jnp.where(valid[:, None, None, :], logits, -jnp.inf)
        p = jax.nn.softmax(logits, axis=-1)
        return jnp.einsum('qgrk,qkgd->qgrd', p, v_sel)

    out = lax.map(block, (qb, qib, wib, starts))
    return out.reshape(S, Hq * d)


def even_mixer(x, w_in, w_out):
    B, S, _ = x.shape
    qa, ka, va, qr, kr, vr, gr = _split(x @ w_in, EVEN_SPLITS)
    hd = (B, S, SB_HEADS, HEAD_DIM)
    a = jax.vmap(stick_breaking_attention)(qa.reshape(hd), ka.reshape(hd), va.reshape(hd))
    qr = rope(qr.reshape(B, S, RET_HEADS, RET_QK_DIM), RET_QK_DIM, RET_THETA)
    kr = rope(kr.reshape(B, S, RET_HEADS, RET_QK_DIM), RET_QK_DIM, RET_THETA)
    r = jax.vmap(retention_chunkwise)(qr, kr, vr.reshape(B, S, RET_HEADS, RET_V_DIM))
    r = jax.nn.silu(gr.astype(F32)) * head_group_norm(r).reshape(B, S, RET_V_W)
    y = jnp.concatenate([a, r], axis=-1).astype(x.dtype)
    return y @ w_out


def odd_mixer(x, w_in, w_out):
    B, S, _ = x.shape
    q, k, v, qi, ki, wi = _split(x @ w_in, ODD_SPLITS)
    q = rope(q.reshape(B, S, DSA_Q_HEADS, HEAD_DIM), HEAD_DIM // ROT_FRACTION, ROPE_THETA)
    k = rope(k.reshape(B, S, DSA_KV_HEADS, HEAD_DIM), HEAD_DIM // ROT_FRACTION, ROPE_THETA)
    v = v.reshape(B, S, DSA_KV_HEADS, HEAD_DIM)
    qi = rope(qi.reshape(B, S, IDX_HEADS, IDX_DIM), IDX_DIM // ROT_FRACTION, ROPE_THETA)
    ki = rope(ki, IDX_DIM // ROT_FRACTION, ROPE_THETA)
    y = jax.vmap(dsa_attention)(q, k, v, qi, ki, wi).astype(x.dtype)
    return y @ w_out


def dense_swiglu(x, w_gate, w_up, w_down):
    return (jax.nn.silu(x @ w_gate) * (x @ w_up)) @ w_down


def moe_swiglu(x, w_router, w_gate, w_up, w_down):
    logits = (x @ w_router).astype(F32)
    top_val, top_idx = lax.top_k(logits, TOP_K_EXPERTS)
    gates = jax.nn.softmax(top_val, axis=-1)
    combine = jnp.sum(jax.nn.one_hot(top_idx, N_EXPERTS, dtype=F32) * gates[..., None], axis=-2)
    y = jnp.zeros(x.shape, F32)
    for e in range(N_EXPERTS):
        h = jax.nn.silu(x @ w_gate[e]) * (x @ w_up[e])
        y = y + combine[..., e:e + 1] * (h @ w_down[e]).astype(F32)
    return y.astype(x.dtype)


def setup_inputs(seed: int = 0) -> dict:
    key = jax.random.key(seed)
    ks = jax.random.split(key, 16)
    n_even = (DEPTH + 1) // 2
    n_odd = DEPTH // 2

    def nrm(k, shape, scale):
        return jax.random.normal(k, shape, F32) * scale

    d_s = D_MODEL ** -0.5
    return {
        "x": nrm(ks[0], (BATCH, SEQ, D_MODEL), 1.0),
        "even_w_in": nrm(ks[1], (n_even, D_MODEL, EVEN_IN), d_s),
        "even_w_out": nrm(ks[2], (n_even, D_MODEL, D_MODEL), d_s * DEEPNORM_BETA),
        "odd_w_in": nrm(ks[3], (n_odd, D_MODEL, ODD_IN), d_s),
        "odd_w_out": nrm(ks[4], (n_odd, D_MODEL, D_MODEL), d_s * DEEPNORM_BETA),
        "ffn_w_gate": nrm(ks[5], (n_even, D_MODEL, D_FF), d_s),
        "ffn_w_up": nrm(ks[6], (n_even, D_MODEL, D_FF), d_s),
        "ffn_w_down": nrm(ks[7], (n_even, D_FF, D_MODEL), D_FF ** -0.5 * DEEPNORM_BETA),
        "moe_w_router": nrm(ks[8], (n_odd, D_MODEL, N_EXPERTS), d_s),
        "moe_w_gate": nrm(ks[9], (n_odd, N_EXPERTS, D_MODEL, D_FF_EXPERT), d_s),
        "moe_w_up": nrm(ks[10], (n_odd, N_EXPERTS, D_MODEL, D_FF_EXPERT), d_s),
        "moe_w_down": nrm(ks[11], (n_odd, N_EXPERTS, D_FF_EXPERT, D_MODEL), D_FF_EXPERT ** -0.5 * DEEPNORM_BETA),
        "ln_g": 1.0 + nrm(ks[12], (DEPTH, 2, D_MODEL), 0.02),
        "ln_b": nrm(ks[13], (DEPTH, 2, D_MODEL), 0.02),
    }


def reference(x, even_w_in, even_w_out, odd_w_in, odd_w_out, ffn_w_gate, ffn_w_up, ffn_w_down,
              moe_w_router, moe_w_gate, moe_w_up, moe_w_down, ln_g, ln_b):
    for i in range(DEPTH):
        j = i // 2
        if i % 2 == 0:
            h = even_mixer(x, even_w_in[j], even_w_out[j])
            x = layer_norm(DEEPNORM_ALPHA * x + h, ln_g[i, 0], ln_b[i, 0])
            f = dense_swiglu(x, ffn_w_gate[j], ffn_w_up[j], ffn_w_down[j])
        else:
            h = odd_mixer(x, odd_w_in[j], odd_w_out[j])
            x = layer_norm(DEEPNORM_ALPHA * x + h, ln_g[i, 0], ln_b[i, 0])
            f = moe_swiglu(x, moe_w_router[j], moe_w_gate[j], moe_w_up[j], moe_w_down[j])
        x = layer_norm(DEEPNORM_ALPHA * x + f, ln_g[i, 1], ln_b[i, 1])
    return x
```

```python
import functools
import math

import jax
import jax.numpy as jnp
import numpy as np
from jax import lax
from jax.experimental import pallas as pl
from jax.experimental.pallas import tpu as pltpu

F32 = jnp.float32
BF16 = jnp.bfloat16
I32 = jnp.int32

D_MODEL = 4096
DEPTH = 4
HEAD_DIM = 128
SB_HEADS = 16
SB_W = SB_HEADS * HEAD_DIM
RET_HEADS = 8
RET_QK_DIM = 128
RET_V_DIM = 256
RET_QK_W = RET_HEADS * RET_QK_DIM
RET_V_W = RET_HEADS * RET_V_DIM
RET_THETA = 10000.0
DSA_Q_HEADS = 32
DSA_KV_HEADS = 8
DSA_GROUP = DSA_Q_HEADS // DSA_KV_HEADS
DSA_Q_W = DSA_Q_HEADS * HEAD_DIM
DSA_KV_W = DSA_KV_HEADS * HEAD_DIM
IDX_HEADS = 16
IDX_DIM = 64
IDX_Q_W = IDX_HEADS * IDX_DIM
IDX_TOPK_MAX = 256
ROPE_THETA = 500000.0
ROT_FRACTION = 4
D_FF = 11008
N_EXPERTS = 8
D_FF_EXPERT = 1792
LN_EPS = 1e-5
DEEPNORM_ALPHA = (2.0 * DEPTH) ** 0.25

EVEN_IN = 3 * SB_W + 2 * RET_QK_W + 2 * RET_V_W
ODD_MAIN = DSA_Q_W + 2 * DSA_KV_W + IDX_Q_W
ODD_TAIL = IDX_DIM + IDX_HEADS

LANES = 128
VMEM_LIMIT_BYTES = 56 * 1024 * 1024

INT_MIN = -2147483648
MASK_NEG = -1e30


def _cparams(*sem):
    return pltpu.CompilerParams(dimension_semantics=sem, vmem_limit_bytes=VMEM_LIMIT_BYTES)


def _sigmoid(x):
    return 1.0 / (1.0 + jnp.exp(-x))


def _dot(a, b):
    return jnp.dot(a, b, preferred_element_type=F32)


def _dot_nt(a, b):
    return lax.dot_general(a, b, (((1,), (1,)), ((), ())), preferred_element_type=F32)


def _mm_kernel(a_ref, w_ref, o_ref):
    @pl.when(pl.program_id(2) == 0)
    def _():
        o_ref[...] = jnp.zeros_like(o_ref)

    o_ref[...] += _dot(a_ref[...], w_ref[...].astype(BF16))


def _matmul(a, w, w_index, n_cols, k_dim, *, tm, tn, tk, name):
    m_dim = a.shape[0]
    tm = min(tm, m_dim)
    n_lead = w.ndim - 2
    grid = (m_dim // tm, n_cols // tn, k_dim // tk)
    return pl.pallas_call(
        _mm_kernel,
        grid=grid,
        in_specs=[
            pl.BlockSpec((tm, tk), lambda m, n, k: (m, k)),
            pl.BlockSpec((None,) * n_lead + (tk, tn), lambda m, n, k: w_index(n, k)),
        ],
        out_specs=pl.BlockSpec((tm, tn), lambda m, n, k: (m, n)),
        out_shape=jax.ShapeDtypeStruct((m_dim, n_cols), F32),
        compiler_params=_cparams("parallel", "parallel", "arbitrary"),
        name=name,
    )(a, w)


def _swiglu_kernel(*refs, blocks_per_expert):
    if blocks_per_expert is None:
        a_ref, wg_ref, wu_ref, o_ref, accg, accu = refs
        c_ref = None
    else:
        a_ref, wg_ref, wu_ref, c_ref, o_ref, accg, accu = refs
    k = pl.program_id(2)
    n = pl.program_id(1)
    last = pl.num_programs(2) - 1

    @pl.when(k == 0)
    def _():
        accg[...] = jnp.zeros_like(accg)
        accu[...] = jnp.zeros_like(accu)

    a = a_ref[...]
    accg[...] += _dot(a, wg_ref[...].astype(BF16))
    accu[...] += _dot(a, wu_ref[...].astype(BF16))

    @pl.when(k == last)
    def _():
        g = accg[...]
        h = g * _sigmoid(g) * accu[...]
        if c_ref is not None:
            e = n // blocks_per_expert
            c = c_ref[...]
            lane = lax.broadcasted_iota(I32, c.shape, 1)
            h = h * jnp.sum(jnp.where(lane == e, c, 0.0), axis=1, keepdims=True)
        o_ref[...] = h.astype(o_ref.dtype)


def _swiglu(a, wg, wu, w_index, n_cols, k_dim, *, tm, tn, tk, name, combine=None, blocks_per_expert=None):
    m_dim = a.shape[0]
    tm = min(tm, m_dim)
    n_lead = wg.ndim - 2
    grid = (m_dim // tm, n_cols // tn, k_dim // tk)
    w_spec = pl.BlockSpec((None,) * n_lead + (tk, tn), lambda m, n, k: w_index(n, k))
    in_specs = [pl.BlockSpec((tm, tk), lambda m, n, k: (m, k)), w_spec, w_spec]
    args = [a, wg, wu]
    if combine is not None:
        in_specs.append(pl.BlockSpec((tm, LANES), lambda m, n, k: (m, 0)))
        args.append(combine)
    return pl.pallas_call(
        functools.partial(_swiglu_kernel, blocks_per_expert=blocks_per_expert if combine is not None else None),
        grid=grid,
        in_specs=in_specs,
        out_specs=pl.BlockSpec((tm, tn), lambda m, n, k: (m, n)),
        out_shape=jax.ShapeDtypeStruct((m_dim, n_cols), BF16),
        scratch_shapes=[pltpu.VMEM((tm, tn), F32), pltpu.VMEM((tm, tn), F32)],
        compiler_params=_cparams("parallel", "parallel", "arbitrary"),
        name=name,
    )(*args)


def _ln_kernel(x_ref, h_ref, g_ref, b_ref, o_ref, ob_ref):
    y = DEEPNORM_ALPHA * x_ref[...] + h_ref[...]
    mu = jnp.mean(y, axis=-1, keepdims=True)
    d = y - mu
    var = jnp.mean(d * d, axis=-1, keepdims=True)
    out = d * lax.rsqrt(var + LN_EPS) * g_ref[...] + b_ref[...]
    o_ref[...] = out
    ob_ref[...] = out.astype(BF16)


def _deepnorm_ln(x, h, g3, b3, idx, *, tm=256):
    s_dim, d_dim = x.shape
    tm = min(tm, s_dim)
    row = pl.BlockSpec((tm, d_dim), lambda i: (i, 0))
    par = pl.BlockSpec((None, 1, d_dim), lambda i: (idx, 0, 0))
    return pl.pallas_call(
        _ln_kernel,
        grid=(s_dim // tm,),
        in_specs=[row, row, par, par],
        out_specs=[row, row],
        out_shape=[jax.ShapeDtypeStruct((s_dim, d_dim), F32), jax.ShapeDtypeStruct((s_dim, d_dim), BF16)],
        compiler_params=_cparams("parallel"),
        name="deepnorm_ln",
    )(x, h, g3, b3)


def _rope_tables(s_dim, period, rot_dim, theta, active=LANES):
    half = rot_dim // 2
    pos = jnp.arange(s_dim, dtype=F32)
    freqs = jnp.exp(-math.log(theta) * jnp.arange(half, dtype=F32) * (2.0 / rot_dim))
    ang = pos[:, None] * freqs[None, :]
    cos, sin = jnp.cos(ang), jnp.sin(ang)
    ones = jnp.ones((s_dim, period - rot_dim), F32)
    zeros = jnp.zeros((s_dim, period - rot_dim), F32)
    cos_p = jnp.concatenate([cos, cos, ones], axis=1)
    sin_p = jnp.concatenate([-sin, sin, zeros], axis=1)
    reps = LANES // period
    cos_t, sin_t = jnp.tile(cos_p, (1, reps)), jnp.tile(sin_p, (1, reps))
    if active < LANES:
        lane = jnp.arange(LANES)[None, :]
        cos_t = jnp.where(lane < active, cos_t, 1.0)
        sin_t = jnp.where(lane < active, sin_t, 0.0)
    return cos_t, sin_t


def _rope_tile(x, cos, sin, half, period=LANES):
    if 2 * half == LANES:
        return x * cos + pltpu.roll(x, half, 1) * sin
    lane = lax.broadcasted_iota(I32, x.shape, 1)
    first = (lane & (period - 1)) < half
    partner = jnp.where(first, pltpu.roll(x, LANES - half, 1), pltpu.roll(x, half, 1))
    return x * cos + partner * sin


SB_BLOCK = 256


def _sb_kernel(q_ref, k_ref, v_ref, o_ref, kb_ref, vb_ref):
    i = pl.program_id(1)
    blk = SB_BLOCK

    @pl.when(i == 0)
    def _():
        kb_ref[...] = k_ref[...].astype(BF16)
        vb_ref[...] = v_ref[...].astype(BF16)

    q = (q_ref[...] * (HEAD_DIM ** -0.5)).astype(BF16)
    row = lax.broadcasted_iota(I32, (blk, blk), 0)
    col = lax.broadcasted_iota(I32, (blk, blk), 1)
    causal = col < row
    later = jnp.where(row > col, 1.0, 0.0).astype(BF16)

    def block(j, acc, run, diagonal):
        start = pl.multiple_of(j * blk, blk)
        kj = kb_ref[pl.ds(start, blk), :]
        vj = vb_ref[pl.ds(start, blk), :]
        z = _dot_nt(q, kj)
        soft = jnp.log(1.0 + jnp.exp(-jnp.abs(z)))
        sp = jnp.maximum(z, 0.0) + soft
        log_beta = jnp.minimum(z, 0.0) - soft
        if diagonal:
            sp = jnp.where(causal, sp, 0.0)
        after = _dot(sp.astype(BF16), later) + run
        w = jnp.exp(log_beta - after)
        if diagonal:
            w = jnp.where(causal, w, 0.0)
        acc = acc + _dot(w.astype(BF16), vj)
        run = run + jnp.sum(sp, axis=1, keepdims=True)
        return acc, run

    acc0 = jnp.zeros((blk, HEAD_DIM), F32)
    run0 = jnp.zeros((blk, 1), F32)
    acc, run = block(i, acc0, run0, True)

    def body(t, carry):
        return block(i - 1 - t, carry[0], carry[1], False)

    acc, run = lax.fori_loop(0, i, body, (acc, run))
    o_ref[...] = acc.astype(o_ref.dtype)


def _stick_breaking(p):
    s_dim = p.shape[0]
    blk = SB_BLOCK
    kv = lambda off: pl.BlockSpec((s_dim, HEAD_DIM), lambda h, i: (0, off + h))
    return pl.pallas_call(
        _sb_kernel,
        grid=(SB_HEADS, s_dim // blk),
        in_specs=[pl.BlockSpec((blk, HEAD_DIM), lambda h, i: (i, h)), kv(SB_HEADS), kv(2 * SB_HEADS)],
        out_specs=pl.BlockSpec((blk, HEAD_DIM), lambda h, i: (i, h)),
        out_shape=jax.ShapeDtypeStruct((s_dim, SB_W), BF16),
        scratch_shapes=[pltpu.VMEM((s_dim, HEAD_DIM), BF16), pltpu.VMEM((s_dim, HEAD_DIM), BF16)],
        compiler_params=_cparams("parallel", "arbitrary"),
        name="stick_breaking",
    )(p, p, p)


RET_CHUNK = 256


def _ret_kernel(lg_ref, q_ref, k_ref, v_ref, g_ref, cos_ref, sin_ref, o_ref, state):
    h = pl.program_id(0)
    c = pl.program_id(1)
    n = RET_CHUNK

    @pl.when(c == 0)
    def _():
        state[...] = jnp.zeros_like(state)

    lg = lg_ref[h]
    cos, sin = cos_ref[...], sin_ref[...]
    q = _rope_tile(q_ref[...], cos, sin, RET_QK_DIM // 2)
    k = _rope_tile(k_ref[...], cos, sin, RET_QK_DIM // 2) * (RET_QK_DIM ** -0.5)
    v = v_ref[...].astype(BF16)

    ii = lax.broadcasted_iota(I32, (n, n), 0)
    jj = lax.broadcasted_iota(I32, (n, n), 1)
    rel = (ii - jj).astype(F32)
    decay = jnp.where(rel >= 0.0, jnp.exp(lg * jnp.maximum(rel, 0.0)), 0.0)
    inner = _dot_nt(q.astype(BF16), k.astype(BF16)) * decay
    out = _dot(inner.astype(BF16), v)

    pos = lax.broadcasted_iota(I32, (n, 1), 0).astype(F32)
    q_decay = jnp.exp(lg * (pos + 1.0))
    k_decay = jnp.exp(lg * (n - 1.0 - pos))
    prev = state[...]
    out = out + _dot((q * q_decay).astype(BF16), prev.astype(BF16))
    kd_t = jnp.transpose(k * k_decay).astype(BF16)
    state[...] = jnp.exp(lg * jnp.full((1, 1), n, F32)) * prev + _dot(kd_t, v)

    mu = jnp.mean(out, axis=-1, keepdims=True)
    d = out - mu
    var = jnp.mean(d * d, axis=-1, keepdims=True)
    g = g_ref[...]
    o_ref[...] = (g * _sigmoid(g) * (d * lax.rsqrt(var + LN_EPS))).astype(o_ref.dtype)


def _retention(p, log_gamma, cos, sin):
    s_dim = p.shape[0]
    n = RET_CHUNK
    qk = lambda off: pl.BlockSpec((n, RET_QK_DIM), lambda h, c, lg: (c, off + h))
    vg = lambda off: pl.BlockSpec((n, RET_V_DIM), lambda h, c, lg: (c, off + h))
    tab = pl.BlockSpec((n, LANES), lambda h, c, lg: (c, 0))
    q_off = 3 * SB_W // RET_QK_DIM
    v_off = (3 * SB_W + 2 * RET_QK_W) // RET_V_DIM
    grid_spec = pltpu.PrefetchScalarGridSpec(
        num_scalar_prefetch=1,
        grid=(RET_HEADS, s_dim // n),
        in_specs=[qk(q_off), qk(q_off + RET_HEADS), vg(v_off), vg(v_off + RET_HEADS), tab, tab],
        out_specs=pl.BlockSpec((n, RET_V_DIM), lambda h, c, lg: (c, h)),
        scratch_shapes=[pltpu.VMEM((RET_QK_DIM, RET_V_DIM), F32)],
    )
    return pl.pallas_call(
        _ret_kernel,
        grid_spec=grid_spec,
        out_shape=jax.ShapeDtypeStruct((s_dim, RET_V_W), BF16),
        compiler_params=_cparams("parallel", "arbitrary"),
        name="retention",
    )(log_gamma, p, p, p, p, cos, sin)


def _rope_cast_kernel(x_ref, cos_ref, sin_ref, o_ref, *, half, scale):
    cos, sin = cos_ref[...], sin_ref[...]
    for t in range(x_ref.shape[1] // LANES):
        sl = slice(t * LANES, (t + 1) * LANES)
        o_ref[:, sl] = (_rope_tile(x_ref[:, sl], cos, sin, half) * scale).astype(o_ref.dtype)


def _rope_cast(p, col0, n_cols, cos, sin, *, half, scale, tm=512, tn=1024):
    s_dim = p.shape[0]
    tm = min(tm, s_dim)
    tn = min(tn, n_cols)
    tab = pl.BlockSpec((tm, LANES), lambda i, j: (i, 0))
    return pl.pallas_call(
        functools.partial(_rope_cast_kernel, half=half, scale=scale),
        grid=(s_dim // tm, n_cols // tn),
        in_specs=[pl.BlockSpec((tm, tn), lambda i, j: (i, col0 // tn + j)), tab, tab],
        out_specs=pl.BlockSpec((tm, tn), lambda i, j: (i, j)),
        out_shape=jax.ShapeDtypeStruct((s_dim, n_cols), BF16),
        compiler_params=_cparams("parallel", "parallel"),
        name="rope_cast",
    )(p, cos, sin)


def _cast_kernel(x_ref, o_ref):
    o_ref[...] = x_ref[...].astype(o_ref.dtype)


def _cast(p, col0, n_cols, *, tm=512, tn=1024):
    s_dim = p.shape[0]
    tm = min(tm, s_dim)
    return pl.pallas_call(
        _cast_kernel,
        grid=(s_dim // tm, n_cols // tn),
        in_specs=[pl.BlockSpec((tm, tn), lambda i, j: (i, col0 // tn + j))],
        out_specs=pl.BlockSpec((tm, tn), lambda i, j: (i, j)),
        out_shape=jax.ShapeDtypeStruct((s_dim, n_cols), BF16),
        compiler_params=_cparams("parallel", "parallel"),
        name="cast_bf16",
    )(p)


def _idx_q_kernel(x_ref, cos_ref, sin_ref, o_ref):
    cos, sin = cos_ref[...], sin_ref[...]
    lane = lax.broadcasted_iota(I32, cos.shape, 1)
    low = lane < IDX_DIM
    for t in range(x_ref.shape[1] // LANES):
        y = _rope_tile(x_ref[:, t * LANES:(t + 1) * LANES], cos, sin, IDX_DIM // ROT_FRACTION // 2, IDX_DIM)
        o_ref[:, (2 * t) * LANES:(2 * t + 1) * LANES] = jnp.where(low, y, 0.0).astype(o_ref.dtype)
        o_ref[:, (2 * t + 1) * LANES:(2 * t + 2) * LANES] = jnp.where(
            low, pltpu.roll(y, IDX_DIM, 1), 0.0).astype(o_ref.dtype)


def _idx_q_prep(p, col0, cos, sin, *, tm=512):
    s_dim = p.shape[0]
    tm = min(tm, s_dim)
    tab = pl.BlockSpec((tm, LANES), lambda i: (i, 0))
    return pl.pallas_call(
        _idx_q_kernel,
        grid=(s_dim // tm,),
        in_specs=[pl.BlockSpec((tm, IDX_Q_W), lambda i: (i, col0 // IDX_Q_W)), tab, tab],
        out_specs=pl.BlockSpec((tm, IDX_HEADS * LANES), lambda i: (i, 0)),
        out_shape=jax.ShapeDtypeStruct((s_dim, IDX_HEADS * LANES), BF16),
        compiler_params=_cparams("parallel"),
        name="idx_q_prep",
    )(p, cos, sin)


def _idx_tail_kernel(x_ref, cos_ref, sin_ref, k_ref, w_ref):
    x = x_ref[...]
    lane = lax.broadcasted_iota(I32, x.shape, 1)
    y = _rope_tile(x, cos_ref[...], sin_ref[...], IDX_DIM // ROT_FRACTION // 2, IDX_DIM) * (IDX_DIM ** -0.5)
    k_ref[...] = jnp.where(lane < IDX_DIM, y, 0.0).astype(k_ref.dtype)
    w_ref[...] = x * (IDX_HEADS ** -0.5)


def _idx_tail_prep(tail, cos, sin, *, tm=512):
    s_dim = tail.shape[0]
    tm = min(tm, s_dim)
    blk = pl.BlockSpec((tm, LANES), lambda i: (i, 0))
    return pl.pallas_call(
        _idx_tail_kernel,
        grid=(s_dim // tm,),
        in_specs=[blk, blk, blk],
        out_specs=[blk, blk],
        out_shape=[jax.ShapeDtypeStruct((s_dim, LANES), BF16), jax.ShapeDtypeStruct((s_dim, LANES), F32)],
        compiler_params=_cparams("parallel"),
        name="idx_tail_prep",
    )(tail, cos, sin)


DSA_QB = 128
DSA_KB = 512


def _select_kernel(qi_ref, ki_ref, w_ref, o_ref, keys_ref, *, topk, n_kblocks, idx_bits):
    i = pl.program_id(0)
    qb, kb = DSA_QB, DSA_KB
    n_live = ((i + 1) * qb + kb - 1) // kb
    q_pos = i * qb + lax.broadcasted_iota(I32, (qb, kb), 0)
    col0 = lax.broadcasted_iota(I32, (qb, kb), 1)
    w = w_ref[...]

    def score_block(jb, _):
        start = pl.multiple_of(jb * kb, kb)
        kj = ki_ref[pl.ds(start, kb), :]
        acc = jnp.zeros((qb, kb), F32)
        for h in range(IDX_HEADS):
            z = _dot_nt(qi_ref[:, h * LANES:(h + 1) * LANES], kj)
            acc = acc + jnp.maximum(z, 0.0) * w[:, IDX_DIM + h:IDX_DIM + h + 1]
        acc = jnp.where(acc == 0.0, 0.0, acc)
        bits = lax.bitcast_convert_type(acc, I32)
        key = jnp.where(bits >= 0, bits, bits ^ 0x7FFFFFFF)
        keys_ref[jb] = jnp.where(jb * kb + col0 <= q_pos, key, INT_MIN)
        return 0

    lax.fori_loop(0, n_live, score_block, 0)

    def count(pred):
        def body(jb, part):
            kk = keys_ref[jb]
            hit = jnp.where(pred(kk, jb * kb + col0), 1.0, 0.0)
            for t in range(kb // LANES):
                part = part + hit[:, t * LANES:(t + 1) * LANES]
            return part
        part = lax.fori_loop(0, n_live, body, jnp.zeros((qb, LANES), F32))
        return jnp.sum(part, axis=1, keepdims=True)

    k_f = float(topk)
    c_nonneg = count(lambda kk, cc: kk >= 0)
    thr = jnp.where(c_nonneg >= k_f, jnp.int32(0), jnp.int32(INT_MIN))

    def thr_bit(b, thr):
        cand = thr + jnp.left_shift(jnp.int32(1), 30 - b)
        c = count(lambda kk, cc: kk >= cand)
        return jnp.where(c >= k_f, cand, thr)

    thr = lax.fori_loop(0, 31, thr_bit, thr)

    need = k_f - count(lambda kk, cc: kk > thr)

    def cut_bit(b, cut):
        cand = cut + jnp.left_shift(jnp.int32(1), idx_bits - 1 - b)
        c = count(lambda kk, cc: (kk == thr) & (cc < cand))
        return jnp.where(c < need, cand, cut)

    cut = lax.fori_loop(0, idx_bits, cut_bit, jnp.zeros((qb, 1), I32))

    def write_live(jb, _):
        kk = keys_ref[jb]
        cc = jb * kb + col0
        sel = ((kk > thr) | ((kk == thr) & (cc <= cut))) & (kk != INT_MIN)
        o_ref[jb] = jnp.where(sel, 0.0, MASK_NEG).astype(o_ref.dtype)
        return 0

    lax.fori_loop(0, n_live, write_live, 0)

    def write_dead(jb, _):
        o_ref[jb] = jnp.full((qb, kb), MASK_NEG, o_ref.dtype)
        return 0

    lax.fori_loop(n_live, n_kblocks, write_dead, 0)


def _dsa_select(qi, ki, w, topk):
    s_dim = qi.shape[0]
    qb, kb = DSA_QB, DSA_KB
    n_kblocks = s_dim // kb
    idx_bits = max(1, (s_dim - 1).bit_length())
    return pl.pallas_call(
        functools.partial(_select_kernel, topk=topk, n_kblocks=n_kblocks, idx_bits=idx_bits),
        grid=(s_dim // qb,),
        in_specs=[
            pl.BlockSpec((qb, IDX_HEADS * LANES), lambda i: (i, 0)),
            pl.BlockSpec((s_dim, LANES), lambda i: (0, 0)),
            pl.BlockSpec((qb, LANES), lambda i: (i, 0)),
        ],
        out_specs=pl.BlockSpec((None, n_kblocks, qb, kb), lambda i: (i, 0, 0, 0)),
        out_shape=jax.ShapeDtypeStruct((s_dim // qb, n_kblocks, qb, kb), BF16),
        scratch_shapes=[pltpu.VMEM((n_kblocks, qb, kb), I32)],
        compiler_params=_cparams("parallel"),
        name="dsa_select",
    )(qi, ki, w)


def _dsa_attn_kernel(q_ref, k_ref, v_ref, b_ref, o_ref):
    i = pl.program_id(1)
    qb, kb, grp = DSA_QB, DSA_KB, DSA_GROUP
    n_live = ((i + 1) * qb + kb - 1) // kb
    q = jnp.concatenate([q_ref[:, r * HEAD_DIM:(r + 1) * HEAD_DIM] for r in range(grp)], axis=0)

    def body(jb, carry):
        m, l, acc = carry
        start = pl.multiple_of(jb * kb, kb)
        kj = k_ref[pl.ds(start, kb), :]
        vj = v_ref[pl.ds(start, kb), :]
        z = _dot_nt(q, kj)
        bias = b_ref[jb].astype(F32)
        z = jnp.concatenate([z[r * qb:(r + 1) * qb, :] + bias for r in range(grp)], axis=0)
        m_new = jnp.maximum(m, jnp.max(z, axis=1, keepdims=True))
        p = jnp.exp(z - m_new)
        alpha = jnp.exp(m - m_new)
        l = alpha * l + jnp.sum(p, axis=1, keepdims=True)
        acc = alpha * acc + _dot(p.astype(BF16), vj)
        return m_new, l, acc

    m0 = jnp.full((grp * qb, 1), MASK_NEG, F32)
    l0 = jnp.zeros((grp * qb, 1), F32)
    a0 = jnp.zeros((grp * qb, HEAD_DIM), F32)
    m, l, acc = lax.fori_loop(0, n_live, body, (m0, l0, a0))
    out = acc / l
    for r in range(grp):
        o_ref[:, r * HEAD_DIM:(r + 1) * HEAD_DIM] = out[r * qb:(r + 1) * qb, :].astype(o_ref.dtype)


def _dsa_attention(q, k, v, bias):
    s_dim = q.shape[0]
    qb, kb = DSA_QB, DSA_KB
    gw = DSA_GROUP * HEAD_DIM
    kv = pl.BlockSpec((s_dim, HEAD_DIM), lambda g, i: (0, g))
    return pl.pallas_call(
        _dsa_attn_kernel,
        grid=(DSA_KV_HEADS, s_dim // qb),
        in_specs=[
            pl.BlockSpec((qb, gw), lambda g, i: (i, g)),
            kv, kv,
            pl.BlockSpec((None, s_dim // kb, qb, kb), lambda g, i: (i, 0, 0, 0)),
        ],
        out_specs=pl.BlockSpec((qb, gw), lambda g, i: (i, g)),
        out_shape=jax.ShapeDtypeStruct((s_dim, DSA_Q_W), BF16),
        compiler_params=_cparams("parallel", "arbitrary"),
        name="dsa_attention",
    )(q, k, v, bias)


def _router_kernel(x_ref, w_ref, o_ref):
    x = x_ref[...]
    w = w_ref[...]
    xh = x.astype(BF16)
    xl = (x - xh.astype(F32)).astype(BF16)
    wh = w.astype(BF16)
    wl = (w - wh.astype(F32)).astype(BF16)
    logits = _dot(xh, wh) + (_dot(xh, wl) + _dot(xl, wh))
    lane = lax.broadcasted_iota(I32, logits.shape, 1).astype(F32)
    logits = jnp.where(lane < N_EXPERTS, logits, -jnp.inf)
    v1 = jnp.max(logits, axis=1, keepdims=True)
    i1 = jnp.min(jnp.where(logits == v1, lane, float(LANES)), axis=1, keepdims=True)
    rest = jnp.where(lane == i1, -jnp.inf, logits)
    v2 = jnp.max(rest, axis=1, keepdims=True)
    i2 = jnp.min(jnp.where(rest == v2, lane, float(LANES)), axis=1, keepdims=True)
    e2 = jnp.exp(v2 - v1)
    g1 = 1.0 / (1.0 + e2)
    g2 = e2 / (1.0 + e2)
    o_ref[...] = jnp.where(lane == i1, g1, jnp.where(lane == i2, g2, 0.0))


def _router(x, w_pad, *, tm=256):
    s_dim, d_dim = x.shape
    tm = min(tm, s_dim)
    return pl.pallas_call(
        _router_kernel,
        grid=(s_dim // tm,),
        in_specs=[pl.BlockSpec((tm, d_dim), lambda i: (i, 0)), pl.BlockSpec((d_dim, LANES), lambda i: (0, 0))],
        out_specs=pl.BlockSpec((tm, LANES), lambda i: (i, 0)),
        out_shape=jax.ShapeDtypeStruct((s_dim, LANES), F32),
        compiler_params=_cparams("parallel"),
        name="moe_router",
    )(x, w_pad)


MM_TILES = dict(tm=2048, tn=1024, tk=512)


def _even_layer(x, xb, j, i, even_w_in, even_w_out, ffn_w_gate, ffn_w_up, ffn_w_down, ln_g3, ln_b3, tabs):
    d = D_MODEL
    p = _matmul(xb, even_w_in, lambda n, k: (j, k, n), EVEN_IN, d, name="even_in_proj", **MM_TILES)
    a = _stick_breaking(p)
    r = _retention(p, tabs["log_gamma"], tabs["ret_cos"], tabs["ret_sin"])
    y = jnp.concatenate([a, r], axis=1)
    h = _matmul(y, even_w_out, lambda n, k: (j, k, n), d, d, name="out_proj", **MM_TILES)
    x, xb = _deepnorm_ln(x, h, ln_g3, ln_b3, 2 * i)
    hid = _swiglu(xb, ffn_w_gate, ffn_w_up, lambda n, k: (j, k, n), D_FF, d, tm=2048, tn=256, tk=1024,
                  name="ffn_up")
    f = _matmul(hid, ffn_w_down, lambda n, k: (j, k, n), d, D_FF, tm=2048, tn=2048, tk=256, name="ffn_down")
    return _deepnorm_ln(x, f, ln_g3, ln_b3, 2 * i + 1)


def _odd_layer(x, xb, j, i, odd_w_in, odd_w_out, moe_w_router, moe_w_gate, moe_w_up, moe_w_down,
               ln_g3, ln_b3, tabs):
    d = D_MODEL
    s_dim = x.shape[0]
    p = _matmul(xb, odd_w_in, lambda n, k: (j, k, n), ODD_MAIN, d, name="odd_in_proj", **MM_TILES)
    w_tail = jnp.pad(odd_w_in[j, :, ODD_MAIN:], ((0, 0), (0, LANES - ODD_TAIL)))
    tail = _matmul(xb, w_tail, lambda n, k: (k, n), LANES, d, tm=2048, tn=LANES, tk=2048, name="odd_in_tail")

    cos_h, sin_h = tabs["head_cos"], tabs["head_sin"]
    q = _rope_cast(p, 0, DSA_Q_W, cos_h, sin_h, half=HEAD_DIM // ROT_FRACTION // 2, scale=HEAD_DIM ** -0.5)
    k = _rope_cast(p, DSA_Q_W, DSA_KV_W, cos_h, sin_h, half=HEAD_DIM // ROT_FRACTION // 2, scale=1.0)
    v = _cast(p, DSA_Q_W + DSA_KV_W, DSA_KV_W)
    qi = _idx_q_prep(p, DSA_Q_W + 2 * DSA_KV_W, tabs["idx_cos"], tabs["idx_sin"])
    ki, wi = _idx_tail_prep(tail, tabs["tail_cos"], tabs["tail_sin"])

    bias = _dsa_select(qi, ki, wi, min(IDX_TOPK_MAX, s_dim // 4))
    y = _dsa_attention(q, k, v, bias)
    h = _matmul(y, odd_w_out, lambda n, k: (j, k, n), d, d, name="out_proj", **MM_TILES)
    x, xb = _deepnorm_ln(x, h, ln_g3, ln_b3, 2 * i)

    w_router = jnp.pad(moe_w_router[j], ((0, 0), (0, LANES - N_EXPERTS)))
    combine = _router(x, w_router)
    bpe = D_FF_EXPERT // 256
    hid = _swiglu(xb, moe_w_gate, moe_w_up, lambda n, k: (j, n // bpe, k, n % bpe), N_EXPERTS * D_FF_EXPERT, d,
                  tm=2048, tn=256, tk=1024, name="moe_up", combine=combine, blocks_per_expert=bpe)
    f = _matmul(hid, moe_w_down, lambda n, k: (j, k // bpe, k % bpe, n), d, N_EXPERTS * D_FF_EXPERT,
                tm=2048, tn=2048, tk=256, name="moe_down")
    return _deepnorm_ln(x, f, ln_g3, ln_b3, 2 * i + 1)


def kernel(x, even_w_in, even_w_out, odd_w_in, odd_w_out, ffn_w_gate, ffn_w_up, ffn_w_down, moe_w_router,
           moe_w_gate, moe_w_up, moe_w_down, ln_g, ln_b):
    batch, s_dim, d = x.shape
    ln_g3 = ln_g.reshape(2 * DEPTH, 1, d)
    ln_b3 = ln_b.reshape(2 * DEPTH, 1, d)
    ret_cos, ret_sin = _rope_tables(s_dim, LANES, RET_QK_DIM, RET_THETA)
    head_cos, head_sin = _rope_tables(s_dim, LANES, HEAD_DIM // ROT_FRACTION, ROPE_THETA)
    idx_cos, idx_sin = _rope_tables(s_dim, IDX_DIM, IDX_DIM // ROT_FRACTION, ROPE_THETA)
    tail_cos, tail_sin = _rope_tables(s_dim, IDX_DIM, IDX_DIM // ROT_FRACTION, ROPE_THETA, active=IDX_DIM)
    tabs = dict(
        log_gamma=jnp.log1p(-jnp.exp2(-5.0 - jnp.arange(RET_HEADS, dtype=F32))),
        ret_cos=ret_cos, ret_sin=ret_sin, head_cos=head_cos, head_sin=head_sin,
        idx_cos=idx_cos, idx_sin=idx_sin, tail_cos=tail_cos, tail_sin=tail_sin,
    )
    outs = []
    for b in range(batch):
        xs = x[b]
        xb = xs.astype(BF16)
        for i in range(DEPTH):
            j = i // 2
            if i % 2 == 0:
                xs, xb = _even_layer(xs, xb, j, i, even_w_in, even_w_out, ffn_w_gate, ffn_w_up, ffn_w_down,
                                     ln_g3, ln_b3, tabs)
            else:
                xs, xb = _odd_layer(xs, xb, j, i, odd_w_in, odd_w_out, moe_w_router, moe_w_gate, moe_w_up,
                                    moe_w_down, ln_g3, ln_b3, tabs)
        outs.append(xs)
    return jnp.stack(outs, axis=0)
```

```python
import functools
import math

import jax
import jax.numpy as jnp
import numpy as np
from jax import lax
from jax.experimental import pallas as pl
from jax.experimental.pallas import tpu as pltpu

F32 = jnp.float32
BF16 = jnp.bfloat16
I32 = jnp.int32

D_MODEL = 4096
DEPTH = 4
HEAD_DIM = 128
SB_HEADS = 16
SB_W = SB_HEADS * HEAD_DIM
RET_HEADS = 8
RET_QK_DIM = 128
RET_V_DIM = 256
RET_QK_W = RET_HEADS * RET_QK_DIM
RET_V_W = RET_HEADS * RET_V_DIM
RET_THETA = 10000.0
DSA_Q_HEADS = 32
DSA_KV_HEADS = 8
DSA_GROUP = DSA_Q_HEADS // DSA_KV_HEADS
DSA_Q_W = DSA_Q_HEADS * HEAD_DIM
DSA_KV_W = DSA_KV_HEADS * HEAD_DIM
IDX_HEADS = 16
IDX_DIM = 64
IDX_Q_W = IDX_HEADS * IDX_DIM
IDX_TOPK_MAX = 256
ROPE_THETA = 500000.0
ROT_FRACTION = 4
D_FF = 11008
N_EXPERTS = 8
D_FF_EXPERT = 1792
LN_EPS = 1e-5
DEEPNORM_ALPHA = (2.0 * DEPTH) ** 0.25

EVEN_IN = 3 * SB_W + 2 * RET_QK_W + 2 * RET_V_W
ODD_MAIN = DSA_Q_W + 2 * DSA_KV_W + IDX_Q_W
ODD_TAIL = IDX_DIM + IDX_HEADS

LANES = 128
VMEM_LIMIT_BYTES = 56 * 1024 * 1024

INT_MIN = -2147483648
MASK_NEG = -1e30
LOG2_E = math.log2(math.e)


def _cparams(*sem):
    return pltpu.CompilerParams(dimension_semantics=sem, vmem_limit_bytes=VMEM_LIMIT_BYTES)


def _sigmoid(x):
    return 1.0 / (1.0 + jnp.exp(-x))


def _dot(a, b):
    return jnp.dot(a, b, preferred_element_type=F32)


def _dot_nt(a, b):
    return lax.dot_general(a, b, (((1,), (1,)), ((), ())), preferred_element_type=F32)


def _mm_kernel(a_ref, w_ref, o_ref):
    @pl.when(pl.program_id(2) == 0)
    def _():
        o_ref[...] = jnp.zeros_like(o_ref)

    o_ref[...] += _dot(a_ref[...], w_ref[...].astype(BF16))


def _matmul(a, w, w_index, n_cols, k_dim, *, tm, tn, tk, name):
    m_dim = a.shape[0]
    tm = min(tm, m_dim)
    n_lead = w.ndim - 2
    grid = (m_dim // tm, n_cols // tn, k_dim // tk)
    return pl.pallas_call(
        _mm_kernel,
        grid=grid,
        in_specs=[
            pl.BlockSpec((tm, tk), lambda m, n, k: (m, k)),
            pl.BlockSpec((None,) * n_lead + (tk, tn), lambda m, n, k: w_index(n, k)),
        ],
        out_specs=pl.BlockSpec((tm, tn), lambda m, n, k: (m, n)),
        out_shape=jax.ShapeDtypeStruct((m_dim, n_cols), F32),
        compiler_params=_cparams("parallel", "parallel", "arbitrary"),
        name=name,
    )(a, w)


def _swiglu_kernel(*refs, blocks_per_expert):
    if blocks_per_expert is None:
        a_ref, wg_ref, wu_ref, o_ref, accg, accu = refs
        c_ref = None
    else:
        a_ref, wg_ref, wu_ref, c_ref, o_ref, accg, accu = refs
    k = pl.program_id(2)
    n = pl.program_id(1)
    last = pl.num_programs(2) - 1

    @pl.when(k == 0)
    def _():
        accg[...] = jnp.zeros_like(accg)
        accu[...] = jnp.zeros_like(accu)

    a = a_ref[...]
    accg[...] += _dot(a, wg_ref[...].astype(BF16))
    accu[...] += _dot(a, wu_ref[...].astype(BF16))

    @pl.when(k == last)
    def _():
        g = accg[...]
        h = g * _sigmoid(g) * accu[...]
        if c_ref is not None:
            e = n // blocks_per_expert
            c = c_ref[...]
            lane = lax.broadcasted_iota(I32, c.shape, 1)
            h = h * jnp.sum(jnp.where(lane == e, c, 0.0), axis=1, keepdims=True)
        o_ref[...] = h.astype(o_ref.dtype)


def _swiglu(a, wg, wu, w_index, n_cols, k_dim, *, tm, tn, tk, name, combine=None, blocks_per_expert=None):
    m_dim = a.shape[0]
    tm = min(tm, m_dim)
    n_lead = wg.ndim - 2
    grid = (m_dim // tm, n_cols // tn, k_dim // tk)
    w_spec = pl.BlockSpec((None,) * n_lead + (tk, tn), lambda m, n, k: w_index(n, k))
    in_specs = [pl.BlockSpec((tm, tk), lambda m, n, k: (m, k)), w_spec, w_spec]
    args = [a, wg, wu]
    if combine is not None:
        in_specs.append(pl.BlockSpec((tm, LANES), lambda m, n, k: (m, 0)))
        args.append(combine)
    return pl.pallas_call(
        functools.partial(_swiglu_kernel, blocks_per_expert=blocks_per_expert if combine is not None else None),
        grid=grid,
        in_specs=in_specs,
        out_specs=pl.BlockSpec((tm, tn), lambda m, n, k: (m, n)),
        out_shape=jax.ShapeDtypeStruct((m_dim, n_cols), BF16),
        scratch_shapes=[pltpu.VMEM((tm, tn), F32), pltpu.VMEM((tm, tn), F32)],
        compiler_params=_cparams("parallel", "parallel", "arbitrary"),
        name=name,
    )(*args)


def _ln_kernel(x_ref, h_ref, g_ref, b_ref, o_ref, ob_ref):
    y = DEEPNORM_ALPHA * x_ref[...] + h_ref[...]
    mu = jnp.mean(y, axis=-1, keepdims=True)
    d = y - mu
    var = jnp.mean(d * d, axis=-1, keepdims=True)
    out = d * lax.rsqrt(var + LN_EPS) * g_ref[...] + b_ref[...]
    o_ref[...] = out
    ob_ref[...] = out.astype(BF16)


def _deepnorm_ln(x, h, g3, b3, idx, *, tm=256):
    s_dim, d_dim = x.shape
    tm = min(tm, s_dim)
    row = pl.BlockSpec((tm, d_dim), lambda i: (i, 0))
    par = pl.BlockSpec((None, 1, d_dim), lambda i: (idx, 0, 0))
    return pl.pallas_call(
        _ln_kernel,
        grid=(s_dim // tm,),
        in_specs=[row, row, par, par],
        out_specs=[row, row],
        out_shape=[jax.ShapeDtypeStruct((s_dim, d_dim), F32), jax.ShapeDtypeStruct((s_dim, d_dim), BF16)],
        compiler_params=_cparams("parallel"),
        name="deepnorm_ln",
    )(x, h, g3, b3)


def _rope_tables(s_dim, period, rot_dim, theta, active=LANES):
    half = rot_dim // 2
    pos = jnp.arange(s_dim, dtype=F32)
    freqs = jnp.exp(-math.log(theta) * jnp.arange(half, dtype=F32) * (2.0 / rot_dim))
    ang = pos[:, None] * freqs[None, :]
    cos, sin = jnp.cos(ang), jnp.sin(ang)
    ones = jnp.ones((s_dim, period - rot_dim), F32)
    zeros = jnp.zeros((s_dim, period - rot_dim), F32)
    cos_p = jnp.concatenate([cos, cos, ones], axis=1)
    sin_p = jnp.concatenate([-sin, sin, zeros], axis=1)
    reps = LANES // period
    cos_t, sin_t = jnp.tile(cos_p, (1, reps)), jnp.tile(sin_p, (1, reps))
    if active < LANES:
        lane = jnp.arange(LANES)[None, :]
        cos_t = jnp.where(lane < active, cos_t, 1.0)
        sin_t = jnp.where(lane < active, sin_t, 0.0)
    return cos_t, sin_t


def _rope_tile(x, cos, sin, half, period=LANES):
    if 2 * half == LANES:
        return x * cos + pltpu.roll(x, half, 1) * sin
    lane = lax.broadcasted_iota(I32, x.shape, 1)
    first = (lane & (period - 1)) < half
    partner = jnp.where(first, pltpu.roll(x, LANES - half, 1), pltpu.roll(x, half, 1))
    return x * cos + partner * sin


SB_BLOCK = 256
SB_GROUP = 4


def _sb_kernel(q_ref, k_ref, v_ref, o_ref, kb_ref, vb_ref):
    i = pl.program_id(1)
    blk = SB_BLOCK
    span = SB_GROUP * blk

    @pl.when(i == 0)
    def _():
        kb_ref[...] = k_ref[...].astype(BF16)
        vb_ref[...] = v_ref[...].astype(BF16)

    q = (q_ref[...] * (HEAD_DIM ** -0.5 * LOG2_E)).astype(BF16)
    row = lax.broadcasted_iota(I32, (blk, blk), 0)
    col = lax.broadcasted_iota(I32, (blk, blk), 1)
    later = jnp.where(row > col, 1.0, 0.0).astype(BF16)

    def group(g, acc, run, diagonal):
        start = pl.multiple_of(g * span, span)
        z = _dot_nt(q, kb_ref[pl.ds(start, span), :])
        soft = jnp.log(1.0 + jnp.exp2(-jnp.abs(z))) * LOG2_E
        sp = jnp.maximum(z, 0.0) + soft
        log_beta = z - sp
        if diagonal:
            key_pos = start + lax.broadcasted_iota(I32, (blk, span), 1)
            q_pos = i * blk + lax.broadcasted_iota(I32, (blk, span), 0)
            causal = key_pos < q_pos
            sp = jnp.where(causal, sp, 0.0)
        parts = [None] * SB_GROUP
        for c in reversed(range(SB_GROUP)):
            sp_c = sp[:, c * blk:(c + 1) * blk]
            parts[c] = _dot(sp_c.astype(BF16), later) + run
            run = run + jnp.sum(sp_c, axis=1, keepdims=True)
        w = jnp.exp2(log_beta - jnp.concatenate(parts, axis=1))
        if diagonal:
            w = jnp.where(causal, w, 0.0)
        acc = acc + _dot(w.astype(BF16), vb_ref[pl.ds(start, span), :])
        return acc, run

    g_diag = (i * blk) // span
    acc0 = jnp.zeros((blk, HEAD_DIM), F32)
    run0 = jnp.zeros((blk, 1), F32)
    acc, run = group(g_diag, acc0, run0, True)

    def body(t, carry):
        return group(g_diag - 1 - t, carry[0], carry[1], False)

    acc, run = lax.fori_loop(0, g_diag, body, (acc, run))
    o_ref[...] = acc.astype(o_ref.dtype)


def _stick_breaking(p):
    s_dim = p.shape[0]
    blk = SB_BLOCK
    kv = lambda off: pl.BlockSpec((s_dim, HEAD_DIM), lambda h, i: (0, off + h))
    return pl.pallas_call(
        _sb_kernel,
        grid=(SB_HEADS, s_dim // blk),
        in_specs=[pl.BlockSpec((blk, HEAD_DIM), lambda h, i: (i, h)), kv(SB_HEADS), kv(2 * SB_HEADS)],
        out_specs=pl.BlockSpec((blk, HEAD_DIM), lambda h, i: (i, h)),
        out_shape=jax.ShapeDtypeStruct((s_dim, SB_W), BF16),
        scratch_shapes=[pltpu.VMEM((s_dim, HEAD_DIM), BF16), pltpu.VMEM((s_dim, HEAD_DIM), BF16)],
        compiler_params=_cparams("parallel", "arbitrary"),
        name="stick_breaking",
    )(p, p, p)


RET_CHUNK = 256


def _ret_kernel(lg_ref, q_ref, k_ref, v_ref, g_ref, cos_ref, sin_ref, o_ref, state):
    h = pl.program_id(0)
    c = pl.program_id(1)
    n = RET_CHUNK

    @pl.when(c == 0)
    def _():
        state[...] = jnp.zeros_like(state)

    lg = lg_ref[h]
    cos, sin = cos_ref[...], sin_ref[...]
    q = _rope_tile(q_ref[...], cos, sin, RET_QK_DIM // 2)
    k = _rope_tile(k_ref[...], cos, sin, RET_QK_DIM // 2) * (RET_QK_DIM ** -0.5)
    v = v_ref[...].astype(BF16)

    ii = lax.broadcasted_iota(I32, (n, n), 0)
    jj = lax.broadcasted_iota(I32, (n, n), 1)
    rel = (ii - jj).astype(F32)
    decay = jnp.where(rel >= 0.0, jnp.exp(lg * jnp.maximum(rel, 0.0)), 0.0)
    inner = _dot_nt(q.astype(BF16), k.astype(BF16)) * decay
    out = _dot(inner.astype(BF16), v)

    pos = lax.broadcasted_iota(I32, (n, 1), 0).astype(F32)
    q_decay = jnp.exp(lg * (pos + 1.0))
    k_decay = jnp.exp(lg * (n - 1.0 - pos))
    prev = state[...]
    out = out + _dot((q * q_decay).astype(BF16), prev.astype(BF16))
    kd_t = jnp.transpose(k * k_decay).astype(BF16)
    state[...] = jnp.exp(lg * jnp.full((1, 1), n, F32)) * prev + _dot(kd_t, v)

    mu = jnp.mean(out, axis=-1, keepdims=True)
    d = out - mu
    var = jnp.mean(d * d, axis=-1, keepdims=True)
    g = g_ref[...]
    o_ref[...] = (g * _sigmoid(g) * (d * lax.rsqrt(var + LN_EPS))).astype(o_ref.dtype)


def _retention(p, log_gamma, cos, sin):
    s_dim = p.shape[0]
    n = RET_CHUNK
    qk = lambda off: pl.BlockSpec((n, RET_QK_DIM), lambda h, c, lg: (c, off + h))
    vg = lambda off: pl.BlockSpec((n, RET_V_DIM), lambda h, c, lg: (c, off + h))
    tab = pl.BlockSpec((n, LANES), lambda h, c, lg: (c, 0))
    q_off = 3 * SB_W // RET_QK_DIM
    v_off = (3 * SB_W + 2 * RET_QK_W) // RET_V_DIM
    grid_spec = pltpu.PrefetchScalarGridSpec(
        num_scalar_prefetch=1,
        grid=(RET_HEADS, s_dim // n),
        in_specs=[qk(q_off), qk(q_off + RET_HEADS), vg(v_off), vg(v_off + RET_HEADS), tab, tab],
        out_specs=pl.BlockSpec((n, RET_V_DIM), lambda h, c, lg: (c, h)),
        scratch_shapes=[pltpu.VMEM((RET_QK_DIM, RET_V_DIM), F32)],
    )
    return pl.pallas_call(
        _ret_kernel,
        grid_spec=grid_spec,
        out_shape=jax.ShapeDtypeStruct((s_dim, RET_V_W), BF16),
        compiler_params=_cparams("parallel", "arbitrary"),
        name="retention",
    )(log_gamma, p, p, p, p, cos, sin)


def _rope_cast_kernel(x_ref, cos_ref, sin_ref, o_ref, *, half, scale):
    cos, sin = cos_ref[...], sin_ref[...]
    for t in range(x_ref.shape[1] // LANES):
        sl = slice(t * LANES, (t + 1) * LANES)
        o_ref[:, sl] = (_rope_tile(x_ref[:, sl], cos, sin, half) * scale).astype(o_ref.dtype)


def _rope_cast(p, col0, n_cols, cos, sin, *, half, scale, tm=512, tn=1024):
    s_dim = p.shape[0]
    tm = min(tm, s_dim)
    tn = min(tn, n_cols)
    tab = pl.BlockSpec((tm, LANES), lambda i, j: (i, 0))
    return pl.pallas_call(
        functools.partial(_rope_cast_kernel, half=half, scale=scale),
        grid=(s_dim // tm, n_cols // tn),
        in_specs=[pl.BlockSpec((tm, tn), lambda i, j: (i, col0 // tn + j)), tab, tab],
        out_specs=pl.BlockSpec((tm, tn), lambda i, j: (i, j)),
        out_shape=jax.ShapeDtypeStruct((s_dim, n_cols), BF16),
        compiler_params=_cparams("parallel", "parallel"),
        name="rope_cast",
    )(p, cos, sin)


def _cast_kernel(x_ref, o_ref):
    o_ref[...] = x_ref[...].astype(o_ref.dtype)


def _cast(p, col0, n_cols, *, tm=512, tn=1024):
    s_dim = p.shape[0]
    tm = min(tm, s_dim)
    return pl.pallas_call(
        _cast_kernel,
        grid=(s_dim // tm, n_cols // tn),
        in_specs=[pl.BlockSpec((tm, tn), lambda i, j: (i, col0 // tn + j))],
        out_specs=pl.BlockSpec((tm, tn), lambda i, j: (i, j)),
        out_shape=jax.ShapeDtypeStruct((s_dim, n_cols), BF16),
        compiler_params=_cparams("parallel", "parallel"),
        name="cast_bf16",
    )(p)


def _idx_q_kernel(x_ref, cos_ref, sin_ref, o_ref):
    cos, sin = cos_ref[...], sin_ref[...]
    lane = lax.broadcasted_iota(I32, cos.shape, 1)
    low = lane < IDX_DIM
    for t in range(x_ref.shape[1] // LANES):
        y = _rope_tile(x_ref[:, t * LANES:(t + 1) * LANES], cos, sin, IDX_DIM // ROT_FRACTION // 2, IDX_DIM)
        o_ref[:, (2 * t) * LANES:(2 * t + 1) * LANES] = jnp.where(low, y, 0.0).astype(o_ref.dtype)
        o_ref[:, (2 * t + 1) * LANES:(2 * t + 2) * LANES] = jnp.where(
            low, pltpu.roll(y, IDX_DIM, 1), 0.0).astype(o_ref.dtype)


def _idx_q_prep(p, col0, cos, sin, *, tm=512):
    s_dim = p.shape[0]
    tm = min(tm, s_dim)
    tab = pl.BlockSpec((tm, LANES), lambda i: (i, 0))
    return pl.pallas_call(
        _idx_q_kernel,
        grid=(s_dim // tm,),
        in_specs=[pl.BlockSpec((tm, IDX_Q_W), lambda i: (i, col0 // IDX_Q_W)), tab, tab],
        out_specs=pl.BlockSpec((tm, IDX_HEADS * LANES), lambda i: (i, 0)),
        out_shape=jax.ShapeDtypeStruct((s_dim, IDX_HEADS * LANES), BF16),
        compiler_params=_cparams("parallel"),
        name="idx_q_prep",
    )(p, cos, sin)


def _idx_tail_kernel(x_ref, cos_ref, sin_ref, k_ref, w_ref):
    x = x_ref[...]
    lane = lax.broadcasted_iota(I32, x.shape, 1)
    y = _rope_tile(x, cos_ref[...], sin_ref[...], IDX_DIM // ROT_FRACTION // 2, IDX_DIM) * (IDX_DIM ** -0.5)
    k_ref[...] = jnp.where(lane < IDX_DIM, y, 0.0).astype(k_ref.dtype)
    w_ref[...] = x * (IDX_HEADS ** -0.5)


def _idx_tail_prep(tail, cos, sin, *, tm=512):
    s_dim = tail.shape[0]
    tm = min(tm, s_dim)
    blk = pl.BlockSpec((tm, LANES), lambda i: (i, 0))
    return pl.pallas_call(
        _idx_tail_kernel,
        grid=(s_dim // tm,),
        in_specs=[blk, blk, blk],
        out_specs=[blk, blk],
        out_shape=[jax.ShapeDtypeStruct((s_dim, LANES), BF16), jax.ShapeDtypeStruct((s_dim, LANES), F32)],
        compiler_params=_cparams("parallel"),
        name="idx_tail_prep",
    )(tail, cos, sin)


DSA_QB = 128
DSA_KB = 512
DSA_AQB = 256


def _select_kernel(qi_ref, ki_ref, w_ref, o_ref, keys_ref, *, topk, n_kblocks, idx_bits):
    i = pl.program_id(0)
    qb, kb = DSA_QB, DSA_KB
    n_live = ((i + 1) * qb + kb - 1) // kb
    q_pos = i * qb + lax.broadcasted_iota(I32, (qb, kb), 0)
    col0 = lax.broadcasted_iota(I32, (qb, kb), 1)
    w = w_ref[...]

    def score_block(jb, _):
        start = pl.multiple_of(jb * kb, kb)
        kj = ki_ref[pl.ds(start, kb), :]
        acc = jnp.zeros((qb, kb), F32)
        for h in range(IDX_HEADS):
            z = _dot_nt(qi_ref[:, h * LANES:(h + 1) * LANES], kj)
            acc = acc + jnp.maximum(z, 0.0) * w[:, IDX_DIM + h:IDX_DIM + h + 1]
        acc = jnp.where(acc == 0.0, 0.0, acc)
        bits = lax.bitcast_convert_type(acc, I32)
        key = jnp.where(bits >= 0, bits, bits ^ 0x7FFFFFFF)
        keys_ref[jb] = jnp.where(jb * kb + col0 <= q_pos, key, INT_MIN)
        return 0

    lax.fori_loop(0, n_live, score_block, 0)

    def count(pred):
        def body(jb, part):
            kk = keys_ref[jb]
            hit = jnp.where(pred(kk, jb * kb + col0), 1.0, 0.0)
            for t in range(kb // LANES):
                part = part + hit[:, t * LANES:(t + 1) * LANES]
            return part
        part = lax.fori_loop(0, n_live, body, jnp.zeros((qb, LANES), F32))
        return jnp.sum(part, axis=1, keepdims=True)

    k_f = float(topk)
    c_nonneg = count(lambda kk, cc: kk >= 0)
    thr = jnp.where(c_nonneg >= k_f, jnp.int32(0), jnp.int32(INT_MIN))

    def thr_bit(b, thr):
        cand = thr + jnp.left_shift(jnp.int32(1), 30 - b)
        c = count(lambda kk, cc: kk >= cand)
        return jnp.where(c >= k_f, cand, thr)

    thr = lax.fori_loop(0, 31, thr_bit, thr)

    need = k_f - count(lambda kk, cc: kk > thr)
    n_eq = count(lambda kk, cc: kk == thr)
    tied = (n_eq > need) & (thr != INT_MIN)

    def cut_bit(b, cut):
        cand = cut + jnp.left_shift(jnp.int32(1), idx_bits - 1 - b)
        c = count(lambda kk, cc: (kk == thr) & (cc < cand))
        return jnp.where(c < need, cand, cut)

    def cut_search():
        return lax.fori_loop(0, idx_bits, cut_bit, jnp.zeros((qb, 1), I32))

    def no_cut():
        return jnp.full((qb, 1), n_kblocks * kb, I32)

    cut = lax.cond(jnp.max(jnp.where(tied, 1.0, 0.0)) > 0.0, cut_search, no_cut)

    def write_live(jb, _):
        kk = keys_ref[jb]
        cc = jb * kb + col0
        sel = ((kk > thr) | ((kk == thr) & (cc <= cut))) & (kk != INT_MIN)
        o_ref[jb] = jnp.where(sel, 0.0, MASK_NEG).astype(o_ref.dtype)
        return 0

    lax.fori_loop(0, n_live, write_live, 0)

    def write_dead(jb, _):
        o_ref[jb] = jnp.full((qb, kb), MASK_NEG, o_ref.dtype)
        return 0

    lax.fori_loop(n_live, n_kblocks, write_dead, 0)


def _dsa_select(qi, ki, w, topk):
    s_dim = qi.shape[0]
    qb, kb = DSA_QB, DSA_KB
    n_kblocks = s_dim // kb
    idx_bits = max(1, (s_dim - 1).bit_length())
    return pl.pallas_call(
        functools.partial(_select_kernel, topk=topk, n_kblocks=n_kblocks, idx_bits=idx_bits),
        grid=(s_dim // qb,),
        in_specs=[
            pl.BlockSpec((qb, IDX_HEADS * LANES), lambda i: (i, 0)),
            pl.BlockSpec((s_dim, LANES), lambda i: (0, 0)),
            pl.BlockSpec((qb, LANES), lambda i: (i, 0)),
        ],
        out_specs=pl.BlockSpec((None, n_kblocks, qb, kb), lambda i: (i, 0, 0, 0)),
        out_shape=jax.ShapeDtypeStruct((s_dim // qb, n_kblocks, qb, kb), BF16),
        scratch_shapes=[pltpu.VMEM((n_kblocks, qb, kb), I32)],
        compiler_params=_cparams("parallel"),
        name="dsa_select",
    )(qi, ki, w)


def _dsa_attn_kernel(q_ref, k_ref, v_ref, b_ref, o_ref):
    i = pl.program_id(1)
    qb, kb, grp = DSA_AQB, DSA_KB, DSA_GROUP
    n_sel = qb // DSA_QB
    n_live = ((i + 1) * qb + kb - 1) // kb
    q = jnp.concatenate([q_ref[:, r * HEAD_DIM:(r + 1) * HEAD_DIM] for r in range(grp)], axis=0)

    def body(jb, carry):
        m, l, acc = carry
        start = pl.multiple_of(jb * kb, kb)
        kj = k_ref[pl.ds(start, kb), :]
        vj = v_ref[pl.ds(start, kb), :]
        z = _dot_nt(q, kj)
        bias = [b_ref[s, jb].astype(F32) for s in range(n_sel)]
        z = jnp.concatenate([z[(r * n_sel + s) * DSA_QB:(r * n_sel + s + 1) * DSA_QB, :] + bias[s]
                             for r in range(grp) for s in range(n_sel)], axis=0)
        m_new = jnp.maximum(m, jnp.max(z, axis=1, keepdims=True))
        p = jnp.exp2(z - m_new)
        alpha = jnp.exp2(m - m_new)
        l = alpha * l + jnp.sum(p, axis=1, keepdims=True)
        acc = alpha * acc + _dot(p.astype(BF16), vj)
        return m_new, l, acc

    rows = grp * qb
    init = (jnp.full((rows, 1), MASK_NEG, F32), jnp.zeros((rows, 1), F32), jnp.zeros((rows, HEAD_DIM), F32))
    _, l, acc = lax.fori_loop(0, n_live, body, init)
    out = acc / l
    for r in range(grp):
        o_ref[:, r * HEAD_DIM:(r + 1) * HEAD_DIM] = out[r * qb:(r + 1) * qb, :].astype(o_ref.dtype)


def _dsa_attention(q, k, v, bias):
    s_dim = q.shape[0]
    qb, kb = DSA_AQB, DSA_KB
    gw = DSA_GROUP * HEAD_DIM
    kv = pl.BlockSpec((s_dim, HEAD_DIM), lambda g, i: (0, g))
    return pl.pallas_call(
        _dsa_attn_kernel,
        grid=(DSA_KV_HEADS, s_dim // qb),
        in_specs=[
            pl.BlockSpec((qb, gw), lambda g, i: (i, g)),
            kv, kv,
            pl.BlockSpec((qb // DSA_QB, s_dim // kb, DSA_QB, kb), lambda g, i: (i, 0, 0, 0)),
        ],
        out_specs=pl.BlockSpec((qb, gw), lambda g, i: (i, g)),
        out_shape=jax.ShapeDtypeStruct((s_dim, DSA_Q_W), BF16),
        compiler_params=_cparams("parallel", "arbitrary"),
        name="dsa_attention",
    )(q, k, v, bias)


def _router_kernel(x_ref, w_ref, o_ref):
    x = x_ref[...]
    w = w_ref[...]
    xh = x.astype(BF16)
    xl = (x - xh.astype(F32)).astype(BF16)
    wh = w.astype(BF16)
    wl = (w - wh.astype(F32)).astype(BF16)
    logits = _dot(xh, wh) + (_dot(xh, wl) + _dot(xl, wh))
    lane = lax.broadcasted_iota(I32, logits.shape, 1).astype(F32)
    logits = jnp.where(lane < N_EXPERTS, logits, -jnp.inf)
    v1 = jnp.max(logits, axis=1, keepdims=True)
    i1 = jnp.min(jnp.where(logits == v1, lane, float(LANES)), axis=1, keepdims=True)
    rest = jnp.where(lane == i1, -jnp.inf, logits)
    v2 = jnp.max(rest, axis=1, keepdims=True)
    i2 = jnp.min(jnp.where(rest == v2, lane, float(LANES)), axis=1, keepdims=True)
    e2 = jnp.exp(v2 - v1)
    g1 = 1.0 / (1.0 + e2)
    g2 = e2 / (1.0 + e2)
    o_ref[...] = jnp.where(lane == i1, g1, jnp.where(lane == i2, g2, 0.0))


def _router(x, w_pad, *, tm=256):
    s_dim, d_dim = x.shape
    tm = min(tm, s_dim)
    return pl.pallas_call(
        _router_kernel,
        grid=(s_dim // tm,),
        in_specs=[pl.BlockSpec((tm, d_dim), lambda i: (i, 0)), pl.BlockSpec((d_dim, LANES), lambda i: (0, 0))],
        out_specs=pl.BlockSpec((tm, LANES), lambda i: (i, 0)),
        out_shape=jax.ShapeDtypeStruct((s_dim, LANES), F32),
        compiler_params=_cparams("parallel"),
        name="moe_router",
    )(x, w_pad)


MM_TILES = dict(tm=2048, tn=1024, tk=512)


def _even_layer(x, xb, j, i, even_w_in, even_w_out, ffn_w_gate, ffn_w_up, ffn_w_down, ln_g3, ln_b3, tabs):
    d = D_MODEL
    p = _matmul(xb, even_w_in, lambda n, k: (j, k, n), EVEN_IN, d, name="even_in_proj", **MM_TILES)
    a = _stick_breaking(p)
    r = _retention(p, tabs["log_gamma"], tabs["ret_cos"], tabs["ret_sin"])
    y = jnp.concatenate([a, r], axis=1)
    h = _matmul(y, even_w_out, lambda n, k: (j, k, n), d, d, name="out_proj", **MM_TILES)
    x, xb = _deepnorm_ln(x, h, ln_g3, ln_b3, 2 * i)
    hid = _swiglu(xb, ffn_w_gate, ffn_w_up, lambda n, k: (j, k, n), D_FF, d, tm=2048, tn=256, tk=1024,
                  name="ffn_up")
    f = _matmul(hid, ffn_w_down, lambda n, k: (j, k, n), d, D_FF, tm=2048, tn=2048, tk=256, name="ffn_down")
    return _deepnorm_ln(x, f, ln_g3, ln_b3, 2 * i + 1)


def _odd_layer(x, xb, j, i, odd_w_in, odd_w_out, moe_w_router, moe_w_gate, moe_w_up, moe_w_down,
               ln_g3, ln_b3, tabs):
    d = D_MODEL
    s_dim = x.shape[0]
    p = _matmul(xb, odd_w_in, lambda n, k: (j, k, n), ODD_MAIN, d, name="odd_in_proj", **MM_TILES)
    w_tail = jnp.pad(odd_w_in[j, :, ODD_MAIN:], ((0, 0), (0, LANES - ODD_TAIL)))
    tail = _matmul(xb, w_tail, lambda n, k: (k, n), LANES, d, tm=2048, tn=LANES, tk=2048, name="odd_in_tail")

    cos_h, sin_h = tabs["head_cos"], tabs["head_sin"]
    q = _rope_cast(p, 0, DSA_Q_W, cos_h, sin_h, half=HEAD_DIM // ROT_FRACTION // 2,
                   scale=HEAD_DIM ** -0.5 * LOG2_E)
    k = _rope_cast(p, DSA_Q_W, DSA_KV_W, cos_h, sin_h, half=HEAD_DIM // ROT_FRACTION // 2, scale=1.0)
    v = _cast(p, DSA_Q_W + DSA_KV_W, DSA_KV_W)
    qi = _idx_q_prep(p, DSA_Q_W + 2 * DSA_KV_W, tabs["idx_cos"], tabs["idx_sin"])
    ki, wi = _idx_tail_prep(tail, tabs["tail_cos"], tabs["tail_sin"])

    bias = _dsa_select(qi, ki, wi, min(IDX_TOPK_MAX, s_dim // 4))
    y = _dsa_attention(q, k, v, bias)
    h = _matmul(y, odd_w_out, lambda n, k: (j, k, n), d, d, name="out_proj", **MM_TILES)
    x, xb = _deepnorm_ln(x, h, ln_g3, ln_b3, 2 * i)

    w_router = jnp.pad(moe_w_router[j], ((0, 0), (0, LANES - N_EXPERTS)))
    combine = _router(x, w_router)
    bpe = D_FF_EXPERT // 256
    hid = _swiglu(xb, moe_w_gate, moe_w_up, lambda n, k: (j, n // bpe, k, n % bpe), N_EXPERTS * D_FF_EXPERT, d,
                  tm=2048, tn=256, tk=1024, name="moe_up", combine=combine, blocks_per_expert=bpe)
    f = _matmul(hid, moe_w_down, lambda n, k: (j, k // bpe, k % bpe, n), d, N_EXPERTS * D_FF_EXPERT,
                tm=2048, tn=2048, tk=256, name="moe_down")
    return _deepnorm_ln(x, f, ln_g3, ln_b3, 2 * i + 1)


def kernel(x, even_w_in, even_w_out, odd_w_in, odd_w_out, ffn_w_gate, ffn_w_up, ffn_w_down, moe_w_router,
           moe_w_gate, moe_w_up, moe_w_down, ln_g, ln_b):
    batch, s_dim, d = x.shape
    ln_g3 = ln_g.reshape(2 * DEPTH, 1, d)
    ln_b3 = ln_b.reshape(2 * DEPTH, 1, d)
    ret_cos, ret_sin = _rope_tables(s_dim, LANES, RET_QK_DIM, RET_THETA)
    head_cos, head_sin = _rope_tables(s_dim, LANES, HEAD_DIM // ROT_FRACTION, ROPE_THETA)
    idx_cos, idx_sin = _rope_tables(s_dim, IDX_DIM, IDX_DIM // ROT_FRACTION, ROPE_THETA)
    tail_cos, tail_sin = _rope_tables(s_dim, IDX_DIM, IDX_DIM // ROT_FRACTION, ROPE_THETA, active=IDX_DIM)
    tabs = dict(
        log_gamma=jnp.log1p(-jnp.exp2(-5.0 - jnp.arange(RET_HEADS, dtype=F32))),
        ret_cos=ret_cos, ret_sin=ret_sin, head_cos=head_cos, head_sin=head_sin,
        idx_cos=idx_cos, idx_sin=idx_sin, tail_cos=tail_cos, tail_sin=tail_sin,
    )
    outs = []
    for b in range(batch):
        xs = x[b]
        xb = xs.astype(BF16)
        for i in range(DEPTH):
            j = i // 2
            if i % 2 == 0:
                xs, xb = _even_layer(xs, xb, j, i, even_w_in, even_w_out, ffn_w_gate, ffn_w_up, ffn_w_down,
                                     ln_g3, ln_b3, tabs)
            else:
                xs, xb = _odd_layer(xs, xb, j, i, odd_w_in, odd_w_out, moe_w_router, moe_w_gate, moe_w_up,
                                    moe_w_down, ln_g3, ln_b3, tabs)
        outs.append(xs)
    return jnp.stack(outs, axis=0)
```

```python
import functools
import math

import jax
import jax.numpy as jnp
import numpy as np
from jax import lax
from jax.experimental import pallas as pl
from jax.experimental.pallas import tpu as pltpu

F32 = jnp.float32
BF16 = jnp.bfloat16
I32 = jnp.int32

D_MODEL = 4096
DEPTH = 4
HEAD_DIM = 128
SB_HEADS = 16
SB_W = SB_HEADS * HEAD_DIM
RET_HEADS = 8
RET_QK_DIM = 128
RET_V_DIM = 256
RET_QK_W = RET_HEADS * RET_QK_DIM
RET_V_W = RET_HEADS * RET_V_DIM
RET_THETA = 10000.0
DSA_Q_HEADS = 32
DSA_KV_HEADS = 8
DSA_GROUP = DSA_Q_HEADS // DSA_KV_HEADS
DSA_Q_W = DSA_Q_HEADS * HEAD_DIM
DSA_KV_W = DSA_KV_HEADS * HEAD_DIM
IDX_HEADS = 16
IDX_DIM = 64
IDX_Q_W = IDX_HEADS * IDX_DIM
IDX_TOPK_MAX = 256
ROPE_THETA = 500000.0
ROT_FRACTION = 4
D_FF = 11008
N_EXPERTS = 8
D_FF_EXPERT = 1792
LN_EPS = 1e-5
DEEPNORM_ALPHA = (2.0 * DEPTH) ** 0.25

EVEN_IN = 3 * SB_W + 2 * RET_QK_W + 2 * RET_V_W
ODD_MAIN = DSA_Q_W + 2 * DSA_KV_W + IDX_Q_W
ODD_TAIL = IDX_DIM + IDX_HEADS

LANES = 128
VMEM_LIMIT_BYTES = 56 * 1024 * 1024

INT_MIN = -2147483648
MASK_NEG = -1e30
LOG2_E = math.log2(math.e)


def _cparams(*sem):
    return pltpu.CompilerParams(dimension_semantics=sem, vmem_limit_bytes=VMEM_LIMIT_BYTES)


def _sigmoid(x):
    return 1.0 / (1.0 + jnp.exp(-x))


def _dot(a, b):
    return jnp.dot(a, b, preferred_element_type=F32)


def _dot_nt(a, b):
    return lax.dot_general(a, b, (((1,), (1,)), ((), ())), preferred_element_type=F32)


def _mm_kernel(a_ref, w_ref, o_ref):
    @pl.when(pl.program_id(2) == 0)
    def _():
        o_ref[...] = jnp.zeros_like(o_ref)

    o_ref[...] += _dot(a_ref[...], w_ref[...].astype(BF16))


def _matmul(a, w, w_index, n_cols, k_dim, *, tm, tn, tk, name):
    m_dim = a.shape[0]
    tm = min(tm, m_dim)
    n_lead = w.ndim - 2
    grid = (m_dim // tm, n_cols // tn, k_dim // tk)
    return pl.pallas_call(
        _mm_kernel,
        grid=grid,
        in_specs=[
            pl.BlockSpec((tm, tk), lambda m, n, k: (m, k)),
            pl.BlockSpec((None,) * n_lead + (tk, tn), lambda m, n, k: w_index(n, k)),
        ],
        out_specs=pl.BlockSpec((tm, tn), lambda m, n, k: (m, n)),
        out_shape=jax.ShapeDtypeStruct((m_dim, n_cols), F32),
        compiler_params=_cparams("parallel", "parallel", "arbitrary"),
        name=name,
    )(a, w)


def _swiglu_kernel(a_ref, wg_ref, wu_ref, o_ref, accg, accu):
    k = pl.program_id(2)
    last = pl.num_programs(2) - 1

    @pl.when(k == 0)
    def _():
        accg[...] = jnp.zeros_like(accg)
        accu[...] = jnp.zeros_like(accu)

    a = a_ref[...]
    accg[...] += _dot(a, wg_ref[...].astype(BF16))
    accu[...] += _dot(a, wu_ref[...].astype(BF16))

    @pl.when(k == last)
    def _():
        g = accg[...]
        o_ref[...] = (g * _sigmoid(g) * accu[...]).astype(o_ref.dtype)


def _swiglu(a, wg, wu, w_index, n_cols, k_dim, *, tm, tn, tk, name):
    m_dim = a.shape[0]
    tm = min(tm, m_dim)
    n_lead = wg.ndim - 2
    grid = (m_dim // tm, n_cols // tn, k_dim // tk)
    w_spec = pl.BlockSpec((None,) * n_lead + (tk, tn), lambda m, n, k: w_index(n, k))
    return pl.pallas_call(
        _swiglu_kernel,
        grid=grid,
        in_specs=[pl.BlockSpec((tm, tk), lambda m, n, k: (m, k)), w_spec, w_spec],
        out_specs=pl.BlockSpec((tm, tn), lambda m, n, k: (m, n)),
        out_shape=jax.ShapeDtypeStruct((m_dim, n_cols), BF16),
        scratch_shapes=[pltpu.VMEM((tm, tn), F32), pltpu.VMEM((tm, tn), F32)],
        compiler_params=_cparams("parallel", "parallel", "arbitrary"),
        name=name,
    )(a, wg, wu)


def _ln_kernel(x_ref, h_ref, g_ref, b_ref, o_ref, ob_ref):
    y = DEEPNORM_ALPHA * x_ref[...] + h_ref[...]
    mu = jnp.mean(y, axis=-1, keepdims=True)
    d = y - mu
    var = jnp.mean(d * d, axis=-1, keepdims=True)
    out = d * lax.rsqrt(var + LN_EPS) * g_ref[...] + b_ref[...]
    o_ref[...] = out
    ob_ref[...] = out.astype(BF16)


def _deepnorm_ln(x, h, g3, b3, idx, *, tm=256):
    s_dim, d_dim = x.shape
    tm = min(tm, s_dim)
    row = pl.BlockSpec((tm, d_dim), lambda i: (i, 0))
    par = pl.BlockSpec((None, 1, d_dim), lambda i: (idx, 0, 0))
    return pl.pallas_call(
        _ln_kernel,
        grid=(s_dim // tm,),
        in_specs=[row, row, par, par],
        out_specs=[row, row],
        out_shape=[jax.ShapeDtypeStruct((s_dim, d_dim), F32), jax.ShapeDtypeStruct((s_dim, d_dim), BF16)],
        compiler_params=_cparams("parallel"),
        name="deepnorm_ln",
    )(x, h, g3, b3)


def _rope_tables(s_dim, period, rot_dim, theta, active=LANES):
    half = rot_dim // 2
    pos = jnp.arange(s_dim, dtype=F32)
    freqs = jnp.exp(-math.log(theta) * jnp.arange(half, dtype=F32) * (2.0 / rot_dim))
    ang = pos[:, None] * freqs[None, :]
    cos, sin = jnp.cos(ang), jnp.sin(ang)
    ones = jnp.ones((s_dim, period - rot_dim), F32)
    zeros = jnp.zeros((s_dim, period - rot_dim), F32)
    cos_p = jnp.concatenate([cos, cos, ones], axis=1)
    sin_p = jnp.concatenate([-sin, sin, zeros], axis=1)
    reps = LANES // period
    cos_t, sin_t = jnp.tile(cos_p, (1, reps)), jnp.tile(sin_p, (1, reps))
    if active < LANES:
        lane = jnp.arange(LANES)[None, :]
        cos_t = jnp.where(lane < active, cos_t, 1.0)
        sin_t = jnp.where(lane < active, sin_t, 0.0)
    return cos_t, sin_t


def _rope_tile(x, cos, sin, half, period=LANES):
    if 2 * half == LANES:
        return x * cos + pltpu.roll(x, half, 1) * sin
    lane = lax.broadcasted_iota(I32, x.shape, 1)
    first = (lane & (period - 1)) < half
    partner = jnp.where(first, pltpu.roll(x, LANES - half, 1), pltpu.roll(x, half, 1))
    return x * cos + partner * sin


SB_BLOCK = 256
SB_GROUP = 4


def _sb_kernel(q_ref, k_ref, v_ref, o_ref, kb_ref, vb_ref):
    i = pl.program_id(1)
    blk = SB_BLOCK
    span = SB_GROUP * blk

    @pl.when(i == 0)
    def _():
        kb_ref[...] = k_ref[...].astype(BF16)
        vb_ref[...] = v_ref[...].astype(BF16)

    q = (q_ref[...] * (HEAD_DIM ** -0.5 * LOG2_E)).astype(BF16)
    row = lax.broadcasted_iota(I32, (blk, blk), 0)
    col = lax.broadcasted_iota(I32, (blk, blk), 1)
    later = jnp.where(row > col, 1.0, 0.0).astype(BF16)

    def group(g, acc, run, diagonal):
        start = pl.multiple_of(g * span, span)
        z = _dot_nt(q, kb_ref[pl.ds(start, span), :])
        soft = jnp.log(1.0 + jnp.exp2(-jnp.abs(z))) * LOG2_E
        sp = jnp.maximum(z, 0.0) + soft
        log_beta = z - sp
        if diagonal:
            key_pos = start + lax.broadcasted_iota(I32, (blk, span), 1)
            q_pos = i * blk + lax.broadcasted_iota(I32, (blk, span), 0)
            causal = key_pos < q_pos
            sp = jnp.where(causal, sp, 0.0)
        parts = [None] * SB_GROUP
        for c in reversed(range(SB_GROUP)):
            sp_c = sp[:, c * blk:(c + 1) * blk]
            parts[c] = _dot(sp_c.astype(BF16), later) + run
            run = run + jnp.sum(sp_c, axis=1, keepdims=True)
        w = jnp.exp2(log_beta - jnp.concatenate(parts, axis=1))
        if diagonal:
            w = jnp.where(causal, w, 0.0)
        acc = acc + _dot(w.astype(BF16), vb_ref[pl.ds(start, span), :])
        return acc, run

    g_diag = (i * blk) // span
    acc0 = jnp.zeros((blk, HEAD_DIM), F32)
    run0 = jnp.zeros((blk, 1), F32)
    acc, run = group(g_diag, acc0, run0, True)

    def body(t, carry):
        return group(g_diag - 1 - t, carry[0], carry[1], False)

    acc, run = lax.fori_loop(0, g_diag, body, (acc, run))
    o_ref[...] = acc.astype(o_ref.dtype)


def _stick_breaking(p):
    s_dim = p.shape[0]
    blk = SB_BLOCK
    kv = lambda off: pl.BlockSpec((s_dim, HEAD_DIM), lambda h, i: (0, off + h))
    return pl.pallas_call(
        _sb_kernel,
        grid=(SB_HEADS, s_dim // blk),
        in_specs=[pl.BlockSpec((blk, HEAD_DIM), lambda h, i: (i, h)), kv(SB_HEADS), kv(2 * SB_HEADS)],
        out_specs=pl.BlockSpec((blk, HEAD_DIM), lambda h, i: (i, h)),
        out_shape=jax.ShapeDtypeStruct((s_dim, SB_W), BF16),
        scratch_shapes=[pltpu.VMEM((s_dim, HEAD_DIM), BF16), pltpu.VMEM((s_dim, HEAD_DIM), BF16)],
        compiler_params=_cparams("parallel", "arbitrary"),
        name="stick_breaking",
    )(p, p, p)


RET_CHUNK = 256


def _ret_kernel(lg_ref, q_ref, k_ref, v_ref, g_ref, cos_ref, sin_ref, o_ref, state):
    h = pl.program_id(0)
    c = pl.program_id(1)
    n = RET_CHUNK

    @pl.when(c == 0)
    def _():
        state[...] = jnp.zeros_like(state)

    lg = lg_ref[h]
    cos, sin = cos_ref[...], sin_ref[...]
    q = _rope_tile(q_ref[...], cos, sin, RET_QK_DIM // 2)
    k = _rope_tile(k_ref[...], cos, sin, RET_QK_DIM // 2) * (RET_QK_DIM ** -0.5)
    v = v_ref[...].astype(BF16)

    ii = lax.broadcasted_iota(I32, (n, n), 0)
    jj = lax.broadcasted_iota(I32, (n, n), 1)
    rel = (ii - jj).astype(F32)
    decay = jnp.where(rel >= 0.0, jnp.exp(lg * jnp.maximum(rel, 0.0)), 0.0)
    inner = _dot_nt(q.astype(BF16), k.astype(BF16)) * decay
    out = _dot(inner.astype(BF16), v)

    pos = lax.broadcasted_iota(I32, (n, 1), 0).astype(F32)
    q_decay = jnp.exp(lg * (pos + 1.0))
    k_decay = jnp.exp(lg * (n - 1.0 - pos))
    prev = state[...]
    out = out + _dot((q * q_decay).astype(BF16), prev.astype(BF16))
    kd_t = jnp.transpose(k * k_decay).astype(BF16)
    state[...] = jnp.exp(lg * jnp.full((1, 1), n, F32)) * prev + _dot(kd_t, v)

    mu = jnp.mean(out, axis=-1, keepdims=True)
    d = out - mu
    var = jnp.mean(d * d, axis=-1, keepdims=True)
    g = g_ref[...]
    o_ref[...] = (g * _sigmoid(g) * (d * lax.rsqrt(var + LN_EPS))).astype(o_ref.dtype)


def _retention(p, log_gamma, cos, sin):
    s_dim = p.shape[0]
    n = RET_CHUNK
    qk = lambda off: pl.BlockSpec((n, RET_QK_DIM), lambda h, c, lg: (c, off + h))
    vg = lambda off: pl.BlockSpec((n, RET_V_DIM), lambda h, c, lg: (c, off + h))
    tab = pl.BlockSpec((n, LANES), lambda h, c, lg: (c, 0))
    q_off = 3 * SB_W // RET_QK_DIM
    v_off = (3 * SB_W + 2 * RET_QK_W) // RET_V_DIM
    grid_spec = pltpu.PrefetchScalarGridSpec(
        num_scalar_prefetch=1,
        grid=(RET_HEADS, s_dim // n),
        in_specs=[qk(q_off), qk(q_off + RET_HEADS), vg(v_off), vg(v_off + RET_HEADS), tab, tab],
        out_specs=pl.BlockSpec((n, RET_V_DIM), lambda h, c, lg: (c, h)),
        scratch_shapes=[pltpu.VMEM((RET_QK_DIM, RET_V_DIM), F32)],
    )
    return pl.pallas_call(
        _ret_kernel,
        grid_spec=grid_spec,
        out_shape=jax.ShapeDtypeStruct((s_dim, RET_V_W), BF16),
        compiler_params=_cparams("parallel", "arbitrary"),
        name="retention",
    )(log_gamma, p, p, p, p, cos, sin)


def _rope_cast_kernel(x_ref, cos_ref, sin_ref, o_ref, *, half, scale):
    cos, sin = cos_ref[...], sin_ref[...]
    for t in range(x_ref.shape[1] // LANES):
        sl = slice(t * LANES, (t + 1) * LANES)
        o_ref[:, sl] = (_rope_tile(x_ref[:, sl], cos, sin, half) * scale).astype(o_ref.dtype)


def _rope_cast(p, col0, n_cols, cos, sin, *, half, scale, tm=512, tn=1024):
    s_dim = p.shape[0]
    tm = min(tm, s_dim)
    tn = min(tn, n_cols)
    tab = pl.BlockSpec((tm, LANES), lambda i, j: (i, 0))
    return pl.pallas_call(
        functools.partial(_rope_cast_kernel, half=half, scale=scale),
        grid=(s_dim // tm, n_cols // tn),
        in_specs=[pl.BlockSpec((tm, tn), lambda i, j: (i, col0 // tn + j)), tab, tab],
        out_specs=pl.BlockSpec((tm, tn), lambda i, j: (i, j)),
        out_shape=jax.ShapeDtypeStruct((s_dim, n_cols), BF16),
        compiler_params=_cparams("parallel", "parallel"),
        name="rope_cast",
    )(p, cos, sin)


def _cast_kernel(x_ref, o_ref):
    o_ref[...] = x_ref[...].astype(o_ref.dtype)


def _cast(p, col0, n_cols, *, tm=512, tn=1024):
    s_dim = p.shape[0]
    tm = min(tm, s_dim)
    return pl.pallas_call(
        _cast_kernel,
        grid=(s_dim // tm, n_cols // tn),
        in_specs=[pl.BlockSpec((tm, tn), lambda i, j: (i, col0 // tn + j))],
        out_specs=pl.BlockSpec((tm, tn), lambda i, j: (i, j)),
        out_shape=jax.ShapeDtypeStruct((s_dim, n_cols), BF16),
        compiler_params=_cparams("parallel", "parallel"),
        name="cast_bf16",
    )(p)


def _idx_q_kernel(x_ref, cos_ref, sin_ref, o_ref):
    cos, sin = cos_ref[...], sin_ref[...]
    lane = lax.broadcasted_iota(I32, cos.shape, 1)
    low = lane < IDX_DIM
    for t in range(x_ref.shape[1] // LANES):
        y = _rope_tile(x_ref[:, t * LANES:(t + 1) * LANES], cos, sin, IDX_DIM // ROT_FRACTION // 2, IDX_DIM)
        o_ref[:, (2 * t) * LANES:(2 * t + 1) * LANES] = jnp.where(low, y, 0.0).astype(o_ref.dtype)
        o_ref[:, (2 * t + 1) * LANES:(2 * t + 2) * LANES] = jnp.where(
            low, pltpu.roll(y, IDX_DIM, 1), 0.0).astype(o_ref.dtype)


def _idx_q_prep(p, col0, cos, sin, *, tm=512):
    s_dim = p.shape[0]
    tm = min(tm, s_dim)
    tab = pl.BlockSpec((tm, LANES), lambda i: (i, 0))
    return pl.pallas_call(
        _idx_q_kernel,
        grid=(s_dim // tm,),
        in_specs=[pl.BlockSpec((tm, IDX_Q_W), lambda i: (i, col0 // IDX_Q_W)), tab, tab],
        out_specs=pl.BlockSpec((tm, IDX_HEADS * LANES), lambda i: (i, 0)),
        out_shape=jax.ShapeDtypeStruct((s_dim, IDX_HEADS * LANES), BF16),
        compiler_params=_cparams("parallel"),
        name="idx_q_prep",
    )(p, cos, sin)


def _idx_tail_kernel(x_ref, cos_ref, sin_ref, k_ref, w_ref):
    x = x_ref[...]
    lane = lax.broadcasted_iota(I32, x.shape, 1)
    y = _rope_tile(x, cos_ref[...], sin_ref[...], IDX_DIM // ROT_FRACTION // 2, IDX_DIM) * (IDX_DIM ** -0.5)
    k_ref[...] = jnp.where(lane < IDX_DIM, y, 0.0).astype(k_ref.dtype)
    w_ref[...] = x * (IDX_HEADS ** -0.5)


def _idx_tail_prep(tail, cos, sin, *, tm=512):
    s_dim = tail.shape[0]
    tm = min(tm, s_dim)
    blk = pl.BlockSpec((tm, LANES), lambda i: (i, 0))
    return pl.pallas_call(
        _idx_tail_kernel,
        grid=(s_dim // tm,),
        in_specs=[blk, blk, blk],
        out_specs=[blk, blk],
        out_shape=[jax.ShapeDtypeStruct((s_dim, LANES), BF16), jax.ShapeDtypeStruct((s_dim, LANES), F32)],
        compiler_params=_cparams("parallel"),
        name="idx_tail_prep",
    )(tail, cos, sin)


DSA_QB = 128
DSA_KB = 512
DSA_AQB = 256


def _select_kernel(qi_ref, ki_ref, w_ref, o_ref, keys_ref, *, topk, n_kblocks, idx_bits):
    i = pl.program_id(0)
    qb, kb = DSA_QB, DSA_KB
    n_live = ((i + 1) * qb + kb - 1) // kb
    q_pos = i * qb + lax.broadcasted_iota(I32, (qb, kb), 0)
    col0 = lax.broadcasted_iota(I32, (qb, kb), 1)
    w = w_ref[...]

    def score_block(jb, _):
        start = pl.multiple_of(jb * kb, kb)
        kj = ki_ref[pl.ds(start, kb), :]
        acc = jnp.zeros((qb, kb), F32)
        for h in range(IDX_HEADS):
            z = _dot_nt(qi_ref[:, h * LANES:(h + 1) * LANES], kj)
            acc = acc + jnp.maximum(z, 0.0) * w[:, IDX_DIM + h:IDX_DIM + h + 1]
        acc = jnp.where(acc == 0.0, 0.0, acc)
        bits = lax.bitcast_convert_type(acc, I32)
        key = jnp.where(bits >= 0, bits, bits ^ 0x7FFFFFFF)
        keys_ref[jb] = jnp.where(jb * kb + col0 <= q_pos, key, INT_MIN)
        return 0

    lax.fori_loop(0, n_live, score_block, 0)

    def count(pred):
        def body(jb, part):
            kk = keys_ref[jb]
            hit = jnp.where(pred(kk, jb * kb + col0), 1.0, 0.0)
            for t in range(kb // LANES):
                part = part + hit[:, t * LANES:(t + 1) * LANES]
            return part
        part = lax.fori_loop(0, n_live, body, jnp.zeros((qb, LANES), F32))
        return jnp.sum(part, axis=1, keepdims=True)

    k_f = float(topk)
    c_nonneg = count(lambda kk, cc: kk >= 0)
    thr = jnp.where(c_nonneg >= k_f, jnp.int32(0), jnp.int32(INT_MIN))

    def thr_bit(b, thr):
        cand = thr + jnp.left_shift(jnp.int32(1), 30 - b)
        c = count(lambda kk, cc: kk >= cand)
        return jnp.where(c >= k_f, cand, thr)

    thr = lax.fori_loop(0, 31, thr_bit, thr)

    need = k_f - count(lambda kk, cc: kk > thr)
    n_eq = count(lambda kk, cc: kk == thr)
    tied = (n_eq > need) & (thr != INT_MIN)

    def cut_bit(b, cut):
        cand = cut + jnp.left_shift(jnp.int32(1), idx_bits - 1 - b)
        c = count(lambda kk, cc: (kk == thr) & (cc < cand))
        return jnp.where(c < need, cand, cut)

    def cut_search():
        return lax.fori_loop(0, idx_bits, cut_bit, jnp.zeros((qb, 1), I32))

    def no_cut():
        return jnp.full((qb, 1), n_kblocks * kb, I32)

    cut = lax.cond(jnp.max(jnp.where(tied, 1.0, 0.0)) > 0.0, cut_search, no_cut)

    def write_live(jb, _):
        kk = keys_ref[jb]
        cc = jb * kb + col0
        sel = ((kk > thr) | ((kk == thr) & (cc <= cut))) & (kk != INT_MIN)
        o_ref[jb] = jnp.where(sel, 0.0, MASK_NEG).astype(o_ref.dtype)
        return 0

    lax.fori_loop(0, n_live, write_live, 0)

    def write_dead(jb, _):
        o_ref[jb] = jnp.full((qb, kb), MASK_NEG, o_ref.dtype)
        return 0

    lax.fori_loop(n_live, n_kblocks, write_dead, 0)


def _dsa_select(qi, ki, w, topk):
    s_dim = qi.shape[0]
    qb, kb = DSA_QB, DSA_KB
    n_kblocks = s_dim // kb
    idx_bits = max(1, (s_dim - 1).bit_length())
    return pl.pallas_call(
        functools.partial(_select_kernel, topk=topk, n_kblocks=n_kblocks, idx_bits=idx_bits),
        grid=(s_dim // qb,),
        in_specs=[
            pl.BlockSpec((qb, IDX_HEADS * LANES), lambda i: (i, 0)),
            pl.BlockSpec((s_dim, LANES), lambda i: (0, 0)),
            pl.BlockSpec((qb, LANES), lambda i: (i, 0)),
        ],
        out_specs=pl.BlockSpec((None, n_kblocks, qb, kb), lambda i: (i, 0, 0, 0)),
        out_shape=jax.ShapeDtypeStruct((s_dim // qb, n_kblocks, qb, kb), BF16),
        scratch_shapes=[pltpu.VMEM((n_kblocks, qb, kb), I32)],
        compiler_params=_cparams("parallel"),
        name="dsa_select",
    )(qi, ki, w)


def _dsa_attn_kernel(q_ref, k_ref, v_ref, b_ref, o_ref):
    i = pl.program_id(1)
    qb, kb, grp = DSA_AQB, DSA_KB, DSA_GROUP
    n_sel = qb // DSA_QB
    n_live = ((i + 1) * qb + kb - 1) // kb
    q = jnp.concatenate([q_ref[:, r * HEAD_DIM:(r + 1) * HEAD_DIM] for r in range(grp)], axis=0)

    def body(jb, carry):
        m, l, acc = carry
        start = pl.multiple_of(jb * kb, kb)
        kj = k_ref[pl.ds(start, kb), :]
        vj = v_ref[pl.ds(start, kb), :]
        z = _dot_nt(q, kj)
        bias = [b_ref[s, jb].astype(F32) for s in range(n_sel)]
        z = jnp.concatenate([z[(r * n_sel + s) * DSA_QB:(r * n_sel + s + 1) * DSA_QB, :] + bias[s]
                             for r in range(grp) for s in range(n_sel)], axis=0)
        m_new = jnp.maximum(m, jnp.max(z, axis=1, keepdims=True))
        p = jnp.exp2(z - m_new)
        alpha = jnp.exp2(m - m_new)
        l = alpha * l + jnp.sum(p, axis=1, keepdims=True)
        acc = alpha * acc + _dot(p.astype(BF16), vj)
        return m_new, l, acc

    rows = grp * qb
    init = (jnp.full((rows, 1), MASK_NEG, F32), jnp.zeros((rows, 1), F32), jnp.zeros((rows, HEAD_DIM), F32))
    _, l, acc = lax.fori_loop(0, n_live, body, init)
    out = acc / l
    for r in range(grp):
        o_ref[:, r * HEAD_DIM:(r + 1) * HEAD_DIM] = out[r * qb:(r + 1) * qb, :].astype(o_ref.dtype)


def _dsa_attention(q, k, v, bias):
    s_dim = q.shape[0]
    qb, kb = DSA_AQB, DSA_KB
    gw = DSA_GROUP * HEAD_DIM
    kv = pl.BlockSpec((s_dim, HEAD_DIM), lambda g, i: (0, g))
    return pl.pallas_call(
        _dsa_attn_kernel,
        grid=(DSA_KV_HEADS, s_dim // qb),
        in_specs=[
            pl.BlockSpec((qb, gw), lambda g, i: (i, g)),
            kv, kv,
            pl.BlockSpec((qb // DSA_QB, s_dim // kb, DSA_QB, kb), lambda g, i: (i, 0, 0, 0)),
        ],
        out_specs=pl.BlockSpec((qb, gw), lambda g, i: (i, g)),
        out_shape=jax.ShapeDtypeStruct((s_dim, DSA_Q_W), BF16),
        compiler_params=_cparams("parallel", "arbitrary"),
        name="dsa_attention",
    )(q, k, v, bias)


ROUTE_E1, ROUTE_E2, ROUTE_G1, ROUTE_G2, ROUTE_R1, ROUTE_R2 = range(6)


def _router_kernel(x_ref, w_ref, o_ref, cnt_ref, seen):
    @pl.when(pl.program_id(0) == 0)
    def _():
        seen[...] = jnp.zeros_like(seen)

    x = x_ref[...]
    w = w_ref[...]
    xh = x.astype(BF16)
    xl = (x - xh.astype(F32)).astype(BF16)
    wh = w.astype(BF16)
    wl = (w - wh.astype(F32)).astype(BF16)
    logits = _dot(xh, wh) + (_dot(xh, wl) + _dot(xl, wh))
    lane = lax.broadcasted_iota(I32, logits.shape, 1).astype(F32)
    logits = jnp.where(lane < N_EXPERTS, logits, -jnp.inf)
    v1 = jnp.max(logits, axis=1, keepdims=True)
    i1 = jnp.min(jnp.where(logits == v1, lane, float(LANES)), axis=1, keepdims=True)
    rest = jnp.where(lane == i1, -jnp.inf, logits)
    v2 = jnp.max(rest, axis=1, keepdims=True)
    i2 = jnp.min(jnp.where(rest == v2, lane, float(LANES)), axis=1, keepdims=True)
    e2 = jnp.exp(v2 - v1)
    g1 = 1.0 / (1.0 + e2)
    g2 = e2 / (1.0 + e2)

    tb = x.shape[0]
    member = jnp.where((lane == i1) | (lane == i2), 1.0, 0.0)
    earlier = jnp.where(lax.broadcasted_iota(I32, (tb, tb), 1) < lax.broadcasted_iota(I32, (tb, tb), 0), 1.0, 0.0)
    prefix = _dot(earlier.astype(BF16), member.astype(BF16)) + seen[0:1, :]
    r1 = jnp.sum(jnp.where(lane == i1, prefix, 0.0), axis=1, keepdims=True)
    r2 = jnp.sum(jnp.where(lane == i2, prefix, 0.0), axis=1, keepdims=True)
    seen[...] = seen[...] + jnp.sum(member, axis=0, keepdims=True)
    cnt_ref[...] = seen[...]

    rec = jnp.zeros_like(logits)
    for slot, val in ((ROUTE_E1, i1), (ROUTE_E2, i2), (ROUTE_G1, g1), (ROUTE_G2, g2), (ROUTE_R1, r1), (ROUTE_R2, r2)):
        rec = jnp.where(lane == float(slot), val, rec)
    o_ref[...] = rec


def _router(x, w_pad, *, tm=256):
    s_dim, d_dim = x.shape
    tm = min(tm, s_dim)
    return pl.pallas_call(
        _router_kernel,
        grid=(s_dim // tm,),
        in_specs=[pl.BlockSpec((tm, d_dim), lambda i: (i, 0)), pl.BlockSpec((d_dim, LANES), lambda i: (0, 0))],
        out_specs=[pl.BlockSpec((tm, LANES), lambda i: (i, 0)), pl.BlockSpec((8, LANES), lambda i: (0, 0))],
        out_shape=[jax.ShapeDtypeStruct((s_dim, LANES), F32), jax.ShapeDtypeStruct((8, LANES), F32)],
        scratch_shapes=[pltpu.VMEM((8, LANES), F32)],
        compiler_params=_cparams("arbitrary"),
        name="moe_router",
    )(x, w_pad)


MOE_TM = 512


def _moe_plan(route, counts, s_dim):
    tm = min(MOE_TM, s_dim)
    n_rows = 2 * s_dim + N_EXPERTS * tm
    e1 = route[:, ROUTE_E1].astype(I32)
    e2 = route[:, ROUTE_E2].astype(I32)
    cnt = counts[0, :N_EXPERTS].astype(I32)
    padded = (cnt + tm - 1) // tm * tm
    ends = jnp.cumsum(padded)
    starts = ends - padded
    dest1 = starts[e1] + route[:, ROUTE_R1].astype(I32)
    dest2 = starts[e2] + route[:, ROUTE_R2].astype(I32)
    tok = jnp.arange(s_dim, dtype=I32)
    src_tok = jnp.zeros((n_rows,), I32).at[dest1].set(tok).at[dest2].set(tok)
    tile_start = jnp.arange(n_rows // tm, dtype=I32) * tm
    tile_expert = jnp.minimum(jnp.searchsorted(ends, tile_start, side="right"), N_EXPERTS - 1).astype(I32)
    n_used = (ends[-1:] // tm).astype(I32)
    return dict(tm=tm, n_rows=n_rows, dest=jnp.concatenate([dest1, dest2]), src_tok=src_tok,
                tile_expert=tile_expert, n_used=n_used)


def _row_copy(src_ref, src_row, dst_ref, dst_row, sem):
    return pltpu.make_async_copy(src_ref.at[pl.ds(src_row, 1), :], dst_ref.at[pl.ds(dst_row, 1), :], sem)


def _dispatch_kernel(src_tok_ref, x_ref, o_ref, buf, sem):
    i = pl.program_id(0)
    tm = buf.shape[0]

    def issue(r, _):
        _row_copy(x_ref, src_tok_ref[i * tm + r], buf, r, sem).start()
        return 0

    lax.fori_loop(0, tm, issue, 0)
    pltpu.make_async_copy(x_ref.at[pl.ds(0, tm), :], buf, sem).wait()
    o_ref[...] = buf[...].astype(o_ref.dtype)


def _moe_dispatch(x, plan, *, tm=256):
    s_dim, d_dim = x.shape
    tm = min(tm, s_dim)
    n_rows = plan["n_rows"]
    return pl.pallas_call(
        _dispatch_kernel,
        grid_spec=pltpu.PrefetchScalarGridSpec(
            num_scalar_prefetch=1,
            grid=(n_rows // tm,),
            in_specs=[pl.BlockSpec(memory_space=pl.ANY)],
            out_specs=pl.BlockSpec((tm, d_dim), lambda i, st: (i, 0)),
            scratch_shapes=[pltpu.VMEM((tm, d_dim), F32), pltpu.SemaphoreType.DMA(())],
        ),
        out_shape=jax.ShapeDtypeStruct((n_rows, d_dim), BF16),
        compiler_params=_cparams("arbitrary"),
        name="moe_dispatch",
    )(plan["src_tok"], x)


def _moe_up_kernel(te_ref, nu_ref, a_ref, wg_ref, wu_ref, o_ref):
    m = pl.program_id(1)

    @pl.when(m < nu_ref[0])
    def _():
        a = a_ref[...]
        g = _dot(a, wg_ref[...].astype(BF16))
        u = _dot(a, wu_ref[...].astype(BF16))
        o_ref[...] = (g * _sigmoid(g) * u).astype(o_ref.dtype)

    @pl.when(m >= nu_ref[0])
    def _():
        o_ref[...] = jnp.zeros_like(o_ref)


def _moe_up(xs, wg, wu, layer, plan, *, tn=256):
    n_rows, d_dim = xs.shape
    tm = plan["tm"]
    w_spec = pl.BlockSpec((None, None, d_dim, tn), lambda n, m, te, nu: (layer, te[m], 0, n))
    return pl.pallas_call(
        _moe_up_kernel,
        grid_spec=pltpu.PrefetchScalarGridSpec(
            num_scalar_prefetch=2,
            grid=(D_FF_EXPERT // tn, n_rows // tm),
            in_specs=[pl.BlockSpec((tm, d_dim), lambda n, m, te, nu: (m, 0)), w_spec, w_spec],
            out_specs=pl.BlockSpec((tm, tn), lambda n, m, te, nu: (m, n)),
        ),
        out_shape=jax.ShapeDtypeStruct((n_rows, D_FF_EXPERT), BF16),
        compiler_params=_cparams("parallel", "arbitrary"),
        name="moe_up",
    )(plan["tile_expert"], plan["n_used"], xs, wg, wu)


def _moe_down_kernel(te_ref, nu_ref, a_ref, w_ref, o_ref):
    m = pl.program_id(1)

    @pl.when(m < nu_ref[0])
    def _():
        o_ref[...] = _dot(a_ref[...], w_ref[...].astype(BF16))

    @pl.when(m >= nu_ref[0])
    def _():
        o_ref[...] = jnp.zeros_like(o_ref)


def _moe_down(hid, wd, layer, plan, *, tn=1024):
    n_rows, f_dim = hid.shape
    tm = plan["tm"]
    d_dim = wd.shape[-1]
    return pl.pallas_call(
        _moe_down_kernel,
        grid_spec=pltpu.PrefetchScalarGridSpec(
            num_scalar_prefetch=2,
            grid=(d_dim // tn, n_rows // tm),
            in_specs=[pl.BlockSpec((tm, f_dim), lambda n, m, te, nu: (m, 0)),
                      pl.BlockSpec((None, None, f_dim, tn), lambda n, m, te, nu: (layer, te[m], 0, n))],
            out_specs=pl.BlockSpec((tm, tn), lambda n, m, te, nu: (m, n)),
        ),
        out_shape=jax.ShapeDtypeStruct((n_rows, d_dim), F32),
        compiler_params=_cparams("parallel", "arbitrary"),
        name="moe_down",
    )(plan["tile_expert"], plan["n_used"], hid, wd)


def _combine_ln_kernel(dest_ref, x_ref, r_ref, ys_ref, g_ref, b_ref, o_ref, ob_ref, buf, sem, *, s_dim):
    i = pl.program_id(0)
    tb = x_ref.shape[0]

    def issue(r, _):
        t = i * tb + r
        _row_copy(ys_ref, dest_ref[t], buf.at[0], r, sem).start()
        _row_copy(ys_ref, dest_ref[s_dim + t], buf.at[1], r, sem).start()
        return 0

    lax.fori_loop(0, tb, issue, 0)
    for slot in range(2):
        pltpu.make_async_copy(ys_ref.at[pl.ds(0, tb), :], buf.at[slot], sem).wait()
    rec = r_ref[...]
    f = rec[:, ROUTE_G1:ROUTE_G1 + 1] * buf[0] + rec[:, ROUTE_G2:ROUTE_G2 + 1] * buf[1]
    y = DEEPNORM_ALPHA * x_ref[...] + f
    mu = jnp.mean(y, axis=-1, keepdims=True)
    d = y - mu
    var = jnp.mean(d * d, axis=-1, keepdims=True)
    out = d * lax.rsqrt(var + LN_EPS) * g_ref[...] + b_ref[...]
    o_ref[...] = out
    ob_ref[...] = out.astype(BF16)


def _moe_combine_ln(x, route, ys, plan, g3, b3, idx, *, tb=256):
    s_dim, d_dim = x.shape
    tb = min(tb, s_dim)
    row = pl.BlockSpec((tb, d_dim), lambda i, de: (i, 0))
    par = pl.BlockSpec((None, 1, d_dim), lambda i, de: (idx, 0, 0))
    return pl.pallas_call(
        functools.partial(_combine_ln_kernel, s_dim=s_dim),
        grid_spec=pltpu.PrefetchScalarGridSpec(
            num_scalar_prefetch=1,
            grid=(s_dim // tb,),
            in_specs=[row, pl.BlockSpec((tb, LANES), lambda i, de: (i, 0)), pl.BlockSpec(memory_space=pl.ANY),
                      par, par],
            out_specs=[row, row],
            scratch_shapes=[pltpu.VMEM((2, tb, d_dim), F32), pltpu.SemaphoreType.DMA(())],
        ),
        out_shape=[jax.ShapeDtypeStruct((s_dim, d_dim), F32), jax.ShapeDtypeStruct((s_dim, d_dim), BF16)],
        compiler_params=_cparams("arbitrary"),
        name="moe_combine_ln",
    )(plan["dest"], x, route, ys, g3, b3)


MM_TILES = dict(tm=2048, tn=1024, tk=512)


def _even_layer(x, xb, j, i, even_w_in, even_w_out, ffn_w_gate, ffn_w_up, ffn_w_down, ln_g3, ln_b3, tabs):
    d = D_MODEL
    p = _matmul(xb, even_w_in, lambda n, k: (j, k, n), EVEN_IN, d, name="even_in_proj", **MM_TILES)
    a = _stick_breaking(p)
    r = _retention(p, tabs["log_gamma"], tabs["ret_cos"], tabs["ret_sin"])
    y = jnp.concatenate([a, r], axis=1)
    h = _matmul(y, even_w_out, lambda n, k: (j, k, n), d, d, name="out_proj", **MM_TILES)
    x, xb = _deepnorm_ln(x, h, ln_g3, ln_b3, 2 * i)
    hid = _swiglu(xb, ffn_w_gate, ffn_w_up, lambda n, k: (j, k, n), D_FF, d, tm=2048, tn=256, tk=1024,
                  name="ffn_up")
    f = _matmul(hid, ffn_w_down, lambda n, k: (j, k, n), d, D_FF, tm=2048, tn=2048, tk=256, name="ffn_down")
    return _deepnorm_ln(x, f, ln_g3, ln_b3, 2 * i + 1)


def _odd_layer(x, xb, j, i, odd_w_in, odd_w_out, moe_w_router, moe_w_gate, moe_w_up, moe_w_down,
               ln_g3, ln_b3, tabs):
    d = D_MODEL
    s_dim = x.shape[0]
    p = _matmul(xb, odd_w_in, lambda n, k: (j, k, n), ODD_MAIN, d, name="odd_in_proj", **MM_TILES)
    w_tail = jnp.pad(odd_w_in[j, :, ODD_MAIN:], ((0, 0), (0, LANES - ODD_TAIL)))
    tail = _matmul(xb, w_tail, lambda n, k: (k, n), LANES, d, tm=2048, tn=LANES, tk=2048, name="odd_in_tail")

    cos_h, sin_h = tabs["head_cos"], tabs["head_sin"]
    q = _rope_cast(p, 0, DSA_Q_W, cos_h, sin_h, half=HEAD_DIM // ROT_FRACTION // 2,
                   scale=HEAD_DIM ** -0.5 * LOG2_E)
    k = _rope_cast(p, DSA_Q_W, DSA_KV_W, cos_h, sin_h, half=HEAD_DIM // ROT_FRACTION // 2, scale=1.0)
    v = _cast(p, DSA_Q_W + DSA_KV_W, DSA_KV_W)
    qi = _idx_q_prep(p, DSA_Q_W + 2 * DSA_KV_W, tabs["idx_cos"], tabs["idx_sin"])
    ki, wi = _idx_tail_prep(tail, tabs["tail_cos"], tabs["tail_sin"])

    bias = _dsa_select(qi, ki, wi, min(IDX_TOPK_MAX, s_dim // 4))
    y = _dsa_attention(q, k, v, bias)
    h = _matmul(y, odd_w_out, lambda n, k: (j, k, n), d, d, name="out_proj", **MM_TILES)
    x, xb = _deepnorm_ln(x, h, ln_g3, ln_b3, 2 * i)

    w_router = jnp.pad(moe_w_router[j], ((0, 0), (0, LANES - N_EXPERTS)))
    route, counts = _router(x, w_router)
    plan = _moe_plan(route, counts, s_dim)
    xs = _moe_dispatch(x, plan)
    hid = _moe_up(xs, moe_w_gate, moe_w_up, j, plan)
    ys = _moe_down(hid, moe_w_down, j, plan)
    return _moe_combine_ln(x, route, ys, plan, ln_g3, ln_b3, 2 * i + 1)


def kernel(x, even_w_in, even_w_out, odd_w_in, odd_w_out, ffn_w_gate, ffn_w_up, ffn_w_down, moe_w_router,
           moe_w_gate, moe_w_up, moe_w_down, ln_g, ln_b):
    batch, s_dim, d = x.shape
    ln_g3 = ln_g.reshape(2 * DEPTH, 1, d)
    ln_b3 = ln_b.reshape(2 * DEPTH, 1, d)
    ret_cos, ret_sin = _rope_tables(s_dim, LANES, RET_QK_DIM, RET_THETA)
    head_cos, head_sin = _rope_tables(s_dim, LANES, HEAD_DIM // ROT_FRACTION, ROPE_THETA)
    idx_cos, idx_sin = _rope_tables(s_dim, IDX_DIM, IDX_DIM // ROT_FRACTION, ROPE_THETA)
    tail_cos, tail_sin = _rope_tables(s_dim, IDX_DIM, IDX_DIM // ROT_FRACTION, ROPE_THETA, active=IDX_DIM)
    tabs = dict(
        log_gamma=jnp.log1p(-jnp.exp2(-5.0 - jnp.arange(RET_HEADS, dtype=F32))),
        ret_cos=ret_cos, ret_sin=ret_sin, head_cos=head_cos, head_sin=head_sin,
        idx_cos=idx_cos, idx_sin=idx_sin, tail_cos=tail_cos, tail_sin=tail_sin,
    )
    outs = []
    for b in range(batch):
        xs = x[b]
        xb = xs.astype(BF16)
        for i in range(DEPTH):
            j = i // 2
            if i % 2 == 0:
                xs, xb = _even_layer(xs, xb, j, i, even_w_in, even_w_out, ffn_w_gate, ffn_w_up, ffn_w_down,
                                     ln_g3, ln_b3, tabs)
            else:
                xs, xb = _odd_layer(xs, xb, j, i, odd_w_in, odd_w_out, moe_w_router, moe_w_gate, moe_w_up,
                                    moe_w_down, ln_g3, ln_b3, tabs)
        outs.append(xs)
    return jnp.stack(outs, axis=0)
```

```python
import functools
import math

import jax
import jax.numpy as jnp
import numpy as np
from jax import lax
from jax.experimental import pallas as pl
from jax.experimental.pallas import tpu as pltpu

F32 = jnp.float32
BF16 = jnp.bfloat16
I32 = jnp.int32

D_MODEL = 4096
DEPTH = 4
HEAD_DIM = 128
SB_HEADS = 16
SB_W = SB_HEADS * HEAD_DIM
RET_HEADS = 8
RET_QK_DIM = 128
RET_V_DIM = 256
RET_QK_W = RET_HEADS * RET_QK_DIM
RET_V_W = RET_HEADS * RET_V_DIM
RET_THETA = 10000.0
DSA_Q_HEADS = 32
DSA_KV_HEADS = 8
DSA_GROUP = DSA_Q_HEADS // DSA_KV_HEADS
DSA_Q_W = DSA_Q_HEADS * HEAD_DIM
DSA_KV_W = DSA_KV_HEADS * HEAD_DIM
IDX_HEADS = 16
IDX_DIM = 64
IDX_Q_W = IDX_HEADS * IDX_DIM
IDX_TOPK_MAX = 256
ROPE_THETA = 500000.0
ROT_FRACTION = 4
D_FF = 11008
N_EXPERTS = 8
D_FF_EXPERT = 1792
LN_EPS = 1e-5
DEEPNORM_ALPHA = (2.0 * DEPTH) ** 0.25

EVEN_IN = 3 * SB_W + 2 * RET_QK_W + 2 * RET_V_W
ODD_MAIN = DSA_Q_W + 2 * DSA_KV_W + IDX_Q_W
ODD_TAIL = IDX_DIM + IDX_HEADS

LANES = 128
VMEM_LIMIT_BYTES = 56 * 1024 * 1024

INT_MIN = -2147483648
MASK_NEG = -1e30
LOG2_E = math.log2(math.e)


def _cparams(*sem):
    return pltpu.CompilerParams(dimension_semantics=sem, vmem_limit_bytes=VMEM_LIMIT_BYTES)


def _sigmoid(x):
    return 1.0 / (1.0 + jnp.exp(-x))


def _dot(a, b):
    return jnp.dot(a, b, preferred_element_type=F32)


def _dot_nt(a, b):
    return lax.dot_general(a, b, (((1,), (1,)), ((), ())), preferred_element_type=F32)


def _mm_kernel(a_ref, w_ref, o_ref):
    @pl.when(pl.program_id(2) == 0)
    def _():
        o_ref[...] = jnp.zeros_like(o_ref)

    o_ref[...] += _dot(a_ref[...], w_ref[...].astype(BF16))


def _matmul(a, w, w_index, n_cols, k_dim, *, tm, tn, tk, name):
    m_dim = a.shape[0]
    tm = min(tm, m_dim)
    n_lead = w.ndim - 2
    grid = (m_dim // tm, n_cols // tn, k_dim // tk)
    return pl.pallas_call(
        _mm_kernel,
        grid=grid,
        in_specs=[
            pl.BlockSpec((tm, tk), lambda m, n, k: (m, k)),
            pl.BlockSpec((None,) * n_lead + (tk, tn), lambda m, n, k: w_index(n, k)),
        ],
        out_specs=pl.BlockSpec((tm, tn), lambda m, n, k: (m, n)),
        out_shape=jax.ShapeDtypeStruct((m_dim, n_cols), F32),
        compiler_params=_cparams("parallel", "parallel", "arbitrary"),
        name=name,
    )(a, w)


def _mm_rows_kernel(a_ref, w_ref, o_ref):
    o_ref[...] = _dot(a_ref[...], w_ref[...].astype(BF16))


def _matmul_rows(a, w, w_index, n_cols, *, tm, tn, name):
    m_dim, k_dim = a.shape
    tm = min(tm, m_dim)
    n_lead = w.ndim - 2
    return pl.pallas_call(
        _mm_rows_kernel,
        grid=(m_dim // tm, n_cols // tn),
        in_specs=[
            pl.BlockSpec((tm, k_dim), lambda m, n: (m, 0), pipeline_mode=pl.Buffered(1)),
            pl.BlockSpec((None,) * n_lead + (k_dim, tn), lambda m, n: w_index(n)),
        ],
        out_specs=pl.BlockSpec((tm, tn), lambda m, n: (m, n)),
        out_shape=jax.ShapeDtypeStruct((m_dim, n_cols), F32),
        compiler_params=_cparams("parallel", "arbitrary"),
        name=name,
    )(a, w)


def _swiglu_rows_kernel(a_ref, wg_ref, wu_ref, o_ref):
    a = a_ref[...]
    g = _dot(a, wg_ref[...].astype(BF16))
    u = _dot(a, wu_ref[...].astype(BF16))
    o_ref[...] = (g * _sigmoid(g) * u).astype(o_ref.dtype)


def _swiglu_rows(a, wg, wu, w_index, n_cols, *, tm, tn, name):
    m_dim, k_dim = a.shape
    tm = min(tm, m_dim)
    n_lead = wg.ndim - 2
    w_spec = pl.BlockSpec((None,) * n_lead + (k_dim, tn), lambda m, n: w_index(n))
    return pl.pallas_call(
        _swiglu_rows_kernel,
        grid=(m_dim // tm, n_cols // tn),
        in_specs=[pl.BlockSpec((tm, k_dim), lambda m, n: (m, 0), pipeline_mode=pl.Buffered(1)), w_spec, w_spec],
        out_specs=pl.BlockSpec((tm, tn), lambda m, n: (m, n)),
        out_shape=jax.ShapeDtypeStruct((m_dim, n_cols), BF16),
        compiler_params=_cparams("parallel", "arbitrary"),
        name=name,
    )(a, wg, wu)


def _ln_kernel(x_ref, h_ref, g_ref, b_ref, o_ref, ob_ref):
    y = DEEPNORM_ALPHA * x_ref[...] + h_ref[...]
    mu = jnp.mean(y, axis=-1, keepdims=True)
    d = y - mu
    var = jnp.mean(d * d, axis=-1, keepdims=True)
    out = d * lax.rsqrt(var + LN_EPS) * g_ref[...] + b_ref[...]
    o_ref[...] = out
    ob_ref[...] = out.astype(BF16)


def _deepnorm_ln(x, h, g3, b3, idx, *, tm=256):
    s_dim, d_dim = x.shape
    tm = min(tm, s_dim)
    row = pl.BlockSpec((tm, d_dim), lambda i: (i, 0))
    par = pl.BlockSpec((None, 1, d_dim), lambda i: (idx, 0, 0))
    return pl.pallas_call(
        _ln_kernel,
        grid=(s_dim // tm,),
        in_specs=[row, row, par, par],
        out_specs=[row, row],
        out_shape=[jax.ShapeDtypeStruct((s_dim, d_dim), F32), jax.ShapeDtypeStruct((s_dim, d_dim), BF16)],
        compiler_params=_cparams("parallel"),
        name="deepnorm_ln",
    )(x, h, g3, b3)


def _rope_tables(s_dim, period, rot_dim, theta, active=LANES):
    half = rot_dim // 2
    pos = jnp.arange(s_dim, dtype=F32)
    freqs = jnp.exp(-math.log(theta) * jnp.arange(half, dtype=F32) * (2.0 / rot_dim))
    ang = pos[:, None] * freqs[None, :]
    cos, sin = jnp.cos(ang), jnp.sin(ang)
    ones = jnp.ones((s_dim, period - rot_dim), F32)
    zeros = jnp.zeros((s_dim, period - rot_dim), F32)
    cos_p = jnp.concatenate([cos, cos, ones], axis=1)
    sin_p = jnp.concatenate([-sin, sin, zeros], axis=1)
    reps = LANES // period
    cos_t, sin_t = jnp.tile(cos_p, (1, reps)), jnp.tile(sin_p, (1, reps))
    if active < LANES:
        lane = jnp.arange(LANES)[None, :]
        cos_t = jnp.where(lane < active, cos_t, 1.0)
        sin_t = jnp.where(lane < active, sin_t, 0.0)
    return cos_t, sin_t


def _rope_tile(x, cos, sin, half, period=LANES):
    if 2 * half == LANES:
        return x * cos + pltpu.roll(x, half, 1) * sin
    lane = lax.broadcasted_iota(I32, x.shape, 1)
    first = (lane & (period - 1)) < half
    partner = jnp.where(first, pltpu.roll(x, LANES - half, 1), pltpu.roll(x, half, 1))
    return x * cos + partner * sin


SB_BLOCK = 256
SB_GROUP = 4


def _sb_kernel(q_ref, k_ref, v_ref, o_ref, kb_ref, vb_ref):
    i = pl.program_id(1)
    blk = SB_BLOCK
    span = SB_GROUP * blk

    @pl.when(i == 0)
    def _():
        kb_ref[...] = k_ref[...].astype(BF16)
        vb_ref[...] = v_ref[...].astype(BF16)

    q = (q_ref[...] * (HEAD_DIM ** -0.5 * LOG2_E)).astype(BF16)
    row = lax.broadcasted_iota(I32, (blk, blk), 0)
    col = lax.broadcasted_iota(I32, (blk, blk), 1)
    later = jnp.where(row > col, 1.0, 0.0).astype(BF16)

    def group(g, acc, run, diagonal):
        start = pl.multiple_of(g * span, span)
        z = _dot_nt(q, kb_ref[pl.ds(start, span), :])
        soft = jnp.log(1.0 + jnp.exp2(-jnp.abs(z))) * LOG2_E
        sp = jnp.maximum(z, 0.0) + soft
        log_beta = z - sp
        if diagonal:
            key_pos = start + lax.broadcasted_iota(I32, (blk, span), 1)
            q_pos = i * blk + lax.broadcasted_iota(I32, (blk, span), 0)
            causal = key_pos < q_pos
            sp = jnp.where(causal, sp, 0.0)
        parts = [None] * SB_GROUP
        for c in reversed(range(SB_GROUP)):
            sp_c = sp[:, c * blk:(c + 1) * blk]
            parts[c] = _dot(sp_c.astype(BF16), later) + run
            run = run + jnp.sum(sp_c, axis=1, keepdims=True)
        w = jnp.exp2(log_beta - jnp.concatenate(parts, axis=1))
        if diagonal:
            w = jnp.where(causal, w, 0.0)
        acc = acc + _dot(w.astype(BF16), vb_ref[pl.ds(start, span), :])
        return acc, run

    g_diag = (i * blk) // span
    acc0 = jnp.zeros((blk, HEAD_DIM), F32)
    run0 = jnp.zeros((blk, 1), F32)
    acc, run = group(g_diag, acc0, run0, True)

    def body(t, carry):
        return group(g_diag - 1 - t, carry[0], carry[1], False)

    acc, run = lax.fori_loop(0, g_diag, body, (acc, run))
    o_ref[...] = acc.astype(o_ref.dtype)


def _stick_breaking(p):
    s_dim = p.shape[0]
    blk = SB_BLOCK
    kv = lambda off: pl.BlockSpec((s_dim, HEAD_DIM), lambda h, i: (0, off + h))
    return pl.pallas_call(
        _sb_kernel,
        grid=(SB_HEADS, s_dim // blk),
        in_specs=[pl.BlockSpec((blk, HEAD_DIM), lambda h, i: (i, h)), kv(SB_HEADS), kv(2 * SB_HEADS)],
        out_specs=pl.BlockSpec((blk, HEAD_DIM), lambda h, i: (i, h)),
        out_shape=jax.ShapeDtypeStruct((s_dim, SB_W), BF16),
        scratch_shapes=[pltpu.VMEM((s_dim, HEAD_DIM), BF16), pltpu.VMEM((s_dim, HEAD_DIM), BF16)],
        compiler_params=_cparams("parallel", "arbitrary"),
        name="stick_breaking",
    )(p, p, p)


RET_CHUNK = 256


def _ret_kernel(lg_ref, q_ref, k_ref, v_ref, g_ref, cos_ref, sin_ref, o_ref, state):
    h = pl.program_id(0)
    c = pl.program_id(1)
    n = RET_CHUNK

    @pl.when(c == 0)
    def _():
        state[...] = jnp.zeros_like(state)

    lg = lg_ref[h]
    cos, sin = cos_ref[...], sin_ref[...]
    q = _rope_tile(q_ref[...], cos, sin, RET_QK_DIM // 2)
    k = _rope_tile(k_ref[...], cos, sin, RET_QK_DIM // 2) * (RET_QK_DIM ** -0.5)
    v = v_ref[...].astype(BF16)

    ii = lax.broadcasted_iota(I32, (n, n), 0)
    jj = lax.broadcasted_iota(I32, (n, n), 1)
    rel = (ii - jj).astype(F32)
    decay = jnp.where(rel >= 0.0, jnp.exp(lg * jnp.maximum(rel, 0.0)), 0.0)
    inner = _dot_nt(q.astype(BF16), k.astype(BF16)) * decay
    out = _dot(inner.astype(BF16), v)

    pos = lax.broadcasted_iota(I32, (n, 1), 0).astype(F32)
    q_decay = jnp.exp(lg * (pos + 1.0))
    k_decay = jnp.exp(lg * (n - 1.0 - pos))
    prev = state[...]
    out = out + _dot((q * q_decay).astype(BF16), prev.astype(BF16))
    kd_t = jnp.transpose(k * k_decay).astype(BF16)
    state[...] = jnp.exp(lg * jnp.full((1, 1), n, F32)) * prev + _dot(kd_t, v)

    mu = jnp.mean(out, axis=-1, keepdims=True)
    d = out - mu
    var = jnp.mean(d * d, axis=-1, keepdims=True)
    g = g_ref[...]
    o_ref[...] = (g * _sigmoid(g) * (d * lax.rsqrt(var + LN_EPS))).astype(o_ref.dtype)


def _retention(p, log_gamma, cos, sin):
    s_dim = p.shape[0]
    n = RET_CHUNK
    qk = lambda off: pl.BlockSpec((n, RET_QK_DIM), lambda h, c, lg: (c, off + h))
    vg = lambda off: pl.BlockSpec((n, RET_V_DIM), lambda h, c, lg: (c, off + h))
    tab = pl.BlockSpec((n, LANES), lambda h, c, lg: (c, 0))
    q_off = 3 * SB_W // RET_QK_DIM
    v_off = (3 * SB_W + 2 * RET_QK_W) // RET_V_DIM
    grid_spec = pltpu.PrefetchScalarGridSpec(
        num_scalar_prefetch=1,
        grid=(RET_HEADS, s_dim // n),
        in_specs=[qk(q_off), qk(q_off + RET_HEADS), vg(v_off), vg(v_off + RET_HEADS), tab, tab],
        out_specs=pl.BlockSpec((n, RET_V_DIM), lambda h, c, lg: (c, h)),
        scratch_shapes=[pltpu.VMEM((RET_QK_DIM, RET_V_DIM), F32)],
    )
    return pl.pallas_call(
        _ret_kernel,
        grid_spec=grid_spec,
        out_shape=jax.ShapeDtypeStruct((s_dim, RET_V_W), BF16),
        compiler_params=_cparams("parallel", "arbitrary"),
        name="retention",
    )(log_gamma, p, p, p, p, cos, sin)


def _rope_cast_kernel(x_ref, cos_ref, sin_ref, o_ref, *, half, scale):
    cos, sin = cos_ref[...], sin_ref[...]
    for t in range(x_ref.shape[1] // LANES):
        sl = slice(t * LANES, (t + 1) * LANES)
        o_ref[:, sl] = (_rope_tile(x_ref[:, sl], cos, sin, half) * scale).astype(o_ref.dtype)


def _rope_cast(p, col0, n_cols, cos, sin, *, half, scale, tm=512, tn=1024):
    s_dim = p.shape[0]
    tm = min(tm, s_dim)
    tn = min(tn, n_cols)
    tab = pl.BlockSpec((tm, LANES), lambda i, j: (i, 0))
    return pl.pallas_call(
        functools.partial(_rope_cast_kernel, half=half, scale=scale),
        grid=(s_dim // tm, n_cols // tn),
        in_specs=[pl.BlockSpec((tm, tn), lambda i, j: (i, col0 // tn + j)), tab, tab],
        out_specs=pl.BlockSpec((tm, tn), lambda i, j: (i, j)),
        out_shape=jax.ShapeDtypeStruct((s_dim, n_cols), BF16),
        compiler_params=_cparams("parallel", "parallel"),
        name="rope_cast",
    )(p, cos, sin)


def _cast_kernel(x_ref, o_ref):
    o_ref[...] = x_ref[...].astype(o_ref.dtype)


def _cast(p, col0, n_cols, *, tm=512, tn=1024):
    s_dim = p.shape[0]
    tm = min(tm, s_dim)
    return pl.pallas_call(
        _cast_kernel,
        grid=(s_dim // tm, n_cols // tn),
        in_specs=[pl.BlockSpec((tm, tn), lambda i, j: (i, col0 // tn + j))],
        out_specs=pl.BlockSpec((tm, tn), lambda i, j: (i, j)),
        out_shape=jax.ShapeDtypeStruct((s_dim, n_cols), BF16),
        compiler_params=_cparams("parallel", "parallel"),
        name="cast_bf16",
    )(p)


def _idx_q_kernel(x_ref, cos_ref, sin_ref, o_ref):
    cos, sin = cos_ref[...], sin_ref[...]
    lane = lax.broadcasted_iota(I32, cos.shape, 1)
    low = lane < IDX_DIM
    for t in range(x_ref.shape[1] // LANES):
        y = _rope_tile(x_ref[:, t * LANES:(t + 1) * LANES], cos, sin, IDX_DIM // ROT_FRACTION // 2, IDX_DIM)
        o_ref[:, (2 * t) * LANES:(2 * t + 1) * LANES] = jnp.where(low, y, 0.0).astype(o_ref.dtype)
        o_ref[:, (2 * t + 1) * LANES:(2 * t + 2) * LANES] = jnp.where(
            low, pltpu.roll(y, IDX_DIM, 1), 0.0).astype(o_ref.dtype)


def _idx_q_prep(p, col0, cos, sin, *, tm=512):
    s_dim = p.shape[0]
    tm = min(tm, s_dim)
    tab = pl.BlockSpec((tm, LANES), lambda i: (i, 0))
    return pl.pallas_call(
        _idx_q_kernel,
        grid=(s_dim // tm,),
        in_specs=[pl.BlockSpec((tm, IDX_Q_W), lambda i: (i, col0 // IDX_Q_W)), tab, tab],
        out_specs=pl.BlockSpec((tm, IDX_HEADS * LANES), lambda i: (i, 0)),
        out_shape=jax.ShapeDtypeStruct((s_dim, IDX_HEADS * LANES), BF16),
        compiler_params=_cparams("parallel"),
        name="idx_q_prep",
    )(p, cos, sin)


def _idx_tail_kernel(x_ref, cos_ref, sin_ref, k_ref, w_ref):
    x = x_ref[...]
    lane = lax.broadcasted_iota(I32, x.shape, 1)
    y = _rope_tile(x, cos_ref[...], sin_ref[...], IDX_DIM // ROT_FRACTION // 2, IDX_DIM) * (IDX_DIM ** -0.5)
    k_ref[...] = jnp.where(lane < IDX_DIM, y, 0.0).astype(k_ref.dtype)
    w_ref[...] = x * (IDX_HEADS ** -0.5)


def _idx_tail_prep(tail, cos, sin, *, tm=512):
    s_dim = tail.shape[0]
    tm = min(tm, s_dim)
    blk = pl.BlockSpec((tm, LANES), lambda i: (i, 0))
    return pl.pallas_call(
        _idx_tail_kernel,
        grid=(s_dim // tm,),
        in_specs=[blk, blk, blk],
        out_specs=[blk, blk],
        out_shape=[jax.ShapeDtypeStruct((s_dim, LANES), BF16), jax.ShapeDtypeStruct((s_dim, LANES), F32)],
        compiler_params=_cparams("parallel"),
        name="idx_tail_prep",
    )(tail, cos, sin)


DSA_QB = 128
DSA_KB = 512
DSA_AQB = 256


def _select_kernel(qi_ref, ki_ref, w_ref, o_ref, keys_ref, *, topk, n_kblocks, idx_bits):
    i = pl.program_id(0)
    qb, kb = DSA_QB, DSA_KB
    n_live = ((i + 1) * qb + kb - 1) // kb
    q_pos = i * qb + lax.broadcasted_iota(I32, (qb, kb), 0)
    col0 = lax.broadcasted_iota(I32, (qb, kb), 1)
    w = w_ref[...]

    def score_block(jb, _):
        start = pl.multiple_of(jb * kb, kb)
        kj = ki_ref[pl.ds(start, kb), :]
        acc = jnp.zeros((qb, kb), F32)
        for h in range(IDX_HEADS):
            z = _dot_nt(qi_ref[:, h * LANES:(h + 1) * LANES], kj)
            acc = acc + jnp.maximum(z, 0.0) * w[:, IDX_DIM + h:IDX_DIM + h + 1]
        acc = jnp.where(acc == 0.0, 0.0, acc)
        bits = lax.bitcast_convert_type(acc, I32)
        key = jnp.where(bits >= 0, bits, bits ^ 0x7FFFFFFF)
        keys_ref[jb] = jnp.where(jb * kb + col0 <= q_pos, key, INT_MIN)
        return 0

    lax.fori_loop(0, n_live, score_block, 0)

    def count(pred):
        def body(jb, part):
            kk = keys_ref[jb]
            hit = jnp.where(pred(kk, jb * kb + col0), 1.0, 0.0)
            for t in range(kb // LANES):
                part = part + hit[:, t * LANES:(t + 1) * LANES]
            return part
        part = lax.fori_loop(0, n_live, body, jnp.zeros((qb, LANES), F32))
        return jnp.sum(part, axis=1, keepdims=True)

    k_f = float(topk)
    c_nonneg = count(lambda kk, cc: kk >= 0)
    thr = jnp.where(c_nonneg >= k_f, jnp.int32(0), jnp.int32(INT_MIN))

    def thr_bit(b, thr):
        cand = thr + jnp.left_shift(jnp.int32(1), 30 - b)
        c = count(lambda kk, cc: kk >= cand)
        return jnp.where(c >= k_f, cand, thr)

    thr = lax.fori_loop(0, 31, thr_bit, thr)

    need = k_f - count(lambda kk, cc: kk > thr)
    n_eq = count(lambda kk, cc: kk == thr)
    tied = (n_eq > need) & (thr != INT_MIN)

    def cut_bit(b, cut):
        cand = cut + jnp.left_shift(jnp.int32(1), idx_bits - 1 - b)
        c = count(lambda kk, cc: (kk == thr) & (cc < cand))
        return jnp.where(c < need, cand, cut)

    def cut_search():
        return lax.fori_loop(0, idx_bits, cut_bit, jnp.zeros((qb, 1), I32))

    def no_cut():
        return jnp.full((qb, 1), n_kblocks * kb, I32)

    cut = lax.cond(jnp.max(jnp.where(tied, 1.0, 0.0)) > 0.0, cut_search, no_cut)

    def write_live(jb, _):
        kk = keys_ref[jb]
        cc = jb * kb + col0
        sel = ((kk > thr) | ((kk == thr) & (cc <= cut))) & (kk != INT_MIN)
        o_ref[jb] = jnp.where(sel, 0.0, MASK_NEG).astype(o_ref.dtype)
        return 0

    lax.fori_loop(0, n_live, write_live, 0)

    def write_dead(jb, _):
        o_ref[jb] = jnp.full((qb, kb), MASK_NEG, o_ref.dtype)
        return 0

    lax.fori_loop(n_live, n_kblocks, write_dead, 0)


def _dsa_select(qi, ki, w, topk):
    s_dim = qi.shape[0]
    qb, kb = DSA_QB, DSA_KB
    n_kblocks = s_dim // kb
    idx_bits = max(1, (s_dim - 1).bit_length())
    return pl.pallas_call(
        functools.partial(_select_kernel, topk=topk, n_kblocks=n_kblocks, idx_bits=idx_bits),
        grid=(s_dim // qb,),
        in_specs=[
            pl.BlockSpec((qb, IDX_HEADS * LANES), lambda i: (i, 0)),
            pl.BlockSpec((s_dim, LANES), lambda i: (0, 0)),
            pl.BlockSpec((qb, LANES), lambda i: (i, 0)),
        ],
        out_specs=pl.BlockSpec((None, n_kblocks, qb, kb), lambda i: (i, 0, 0, 0)),
        out_shape=jax.ShapeDtypeStruct((s_dim // qb, n_kblocks, qb, kb), BF16),
        scratch_shapes=[pltpu.VMEM((n_kblocks, qb, kb), I32)],
        compiler_params=_cparams("parallel"),
        name="dsa_select",
    )(qi, ki, w)


def _dsa_attn_kernel(q_ref, k_ref, v_ref, b_ref, o_ref):
    i = pl.program_id(1)
    qb, kb, grp = DSA_AQB, DSA_KB, DSA_GROUP
    n_sel = qb // DSA_QB
    n_live = ((i + 1) * qb + kb - 1) // kb
    q = jnp.concatenate([q_ref[:, r * HEAD_DIM:(r + 1) * HEAD_DIM] for r in range(grp)], axis=0)

    def body(jb, carry):
        m, l, acc = carry
        start = pl.multiple_of(jb * kb, kb)
        kj = k_ref[pl.ds(start, kb), :]
        vj = v_ref[pl.ds(start, kb), :]
        z = _dot_nt(q, kj)
        bias = b_ref[:, jb].astype(F32).reshape(1, qb, kb)
        z = (z.reshape(grp, qb, kb) + bias).reshape(grp * qb, kb)
        m_new = jnp.maximum(m, jnp.max(z, axis=1, keepdims=True))
        p = jnp.exp2(z - m_new)
        alpha = jnp.exp2(m - m_new)
        l = alpha * l + jnp.sum(p, axis=1, keepdims=True)
        acc = alpha * acc + _dot(p.astype(BF16), vj)
        return m_new, l, acc

    rows = grp * qb
    init = (jnp.full((rows, 1), MASK_NEG, F32), jnp.zeros((rows, 1), F32), jnp.zeros((rows, HEAD_DIM), F32))
    _, l, acc = lax.fori_loop(0, n_live, body, init)
    out = acc / l
    for r in range(grp):
        o_ref[:, r * HEAD_DIM:(r + 1) * HEAD_DIM] = out[r * qb:(r + 1) * qb, :].astype(o_ref.dtype)


def _dsa_attention(q, k, v, bias):
    s_dim = q.shape[0]
    qb, kb = DSA_AQB, DSA_KB
    gw = DSA_GROUP * HEAD_DIM
    kv = pl.BlockSpec((s_dim, HEAD_DIM), lambda g, i: (0, g))
    return pl.pallas_call(
        _dsa_attn_kernel,
        grid=(DSA_KV_HEADS, s_dim // qb),
        in_specs=[
            pl.BlockSpec((qb, gw), lambda g, i: (i, g)),
            kv, kv,
            pl.BlockSpec((qb // DSA_QB, s_dim // kb, DSA_QB, kb), lambda g, i: (i, 0, 0, 0)),
        ],
        out_specs=pl.BlockSpec((qb, gw), lambda g, i: (i, g)),
        out_shape=jax.ShapeDtypeStruct((s_dim, DSA_Q_W), BF16),
        compiler_params=_cparams("parallel", "arbitrary"),
        name="dsa_attention",
    )(q, k, v, bias)


ROUTE_E1, ROUTE_E2, ROUTE_G1, ROUTE_G2, ROUTE_R1, ROUTE_R2 = range(6)


def _router_kernel(x_ref, w_ref, o_ref, cnt_ref, seen):
    @pl.when(pl.program_id(0) == 0)
    def _():
        seen[...] = jnp.zeros_like(seen)

    x = x_ref[...]
    w = w_ref[...]
    xh = x.astype(BF16)
    xl = (x - xh.astype(F32)).astype(BF16)
    wh = w.astype(BF16)
    wl = (w - wh.astype(F32)).astype(BF16)
    logits = _dot(xh, wh) + (_dot(xh, wl) + _dot(xl, wh))
    lane = lax.broadcasted_iota(I32, logits.shape, 1).astype(F32)
    logits = jnp.where(lane < N_EXPERTS, logits, -jnp.inf)
    v1 = jnp.max(logits, axis=1, keepdims=True)
    i1 = jnp.min(jnp.where(logits == v1, lane, float(LANES)), axis=1, keepdims=True)
    rest = jnp.where(lane == i1, -jnp.inf, logits)
    v2 = jnp.max(rest, axis=1, keepdims=True)
    i2 = jnp.min(jnp.where(rest == v2, lane, float(LANES)), axis=1, keepdims=True)
    e2 = jnp.exp(v2 - v1)
    g1 = 1.0 / (1.0 + e2)
    g2 = e2 / (1.0 + e2)

    tb = x.shape[0]
    member = jnp.where((lane == i1) | (lane == i2), 1.0, 0.0)
    earlier = jnp.where(lax.broadcasted_iota(I32, (tb, tb), 1) < lax.broadcasted_iota(I32, (tb, tb), 0), 1.0, 0.0)
    prefix = _dot(earlier.astype(BF16), member.astype(BF16)) + seen[0:1, :]
    r1 = jnp.sum(jnp.where(lane == i1, prefix, 0.0), axis=1, keepdims=True)
    r2 = jnp.sum(jnp.where(lane == i2, prefix, 0.0), axis=1, keepdims=True)
    seen[...] = seen[...] + jnp.sum(member, axis=0, keepdims=True)
    cnt_ref[...] = seen[...]

    rec = jnp.zeros_like(logits)
    for slot, val in ((ROUTE_E1, i1), (ROUTE_E2, i2), (ROUTE_G1, g1), (ROUTE_G2, g2), (ROUTE_R1, r1), (ROUTE_R2, r2)):
        rec = jnp.where(lane == float(slot), val, rec)
    o_ref[...] = rec


def _router(x, w_pad, *, tm=256):
    s_dim, d_dim = x.shape
    tm = min(tm, s_dim)
    return pl.pallas_call(
        _router_kernel,
        grid=(s_dim // tm,),
        in_specs=[pl.BlockSpec((tm, d_dim), lambda i: (i, 0)), pl.BlockSpec((d_dim, LANES), lambda i: (0, 0))],
        out_specs=[pl.BlockSpec((tm, LANES), lambda i: (i, 0)), pl.BlockSpec((8, LANES), lambda i: (0, 0))],
        out_shape=[jax.ShapeDtypeStruct((s_dim, LANES), F32), jax.ShapeDtypeStruct((8, LANES), F32)],
        scratch_shapes=[pltpu.VMEM((8, LANES), F32)],
        compiler_params=_cparams("arbitrary"),
        name="moe_router",
    )(x, w_pad)


MOE_TM = 512


def _moe_plan(route, counts, s_dim):
    tm = min(MOE_TM, s_dim)
    n_rows = 2 * s_dim + N_EXPERTS * tm
    e1 = route[:, ROUTE_E1].astype(I32)
    e2 = route[:, ROUTE_E2].astype(I32)
    cnt = counts[0, :N_EXPERTS].astype(I32)
    padded = (cnt + tm - 1) // tm * tm
    ends = jnp.cumsum(padded)
    starts = ends - padded
    dest1 = starts[e1] + route[:, ROUTE_R1].astype(I32)
    dest2 = starts[e2] + route[:, ROUTE_R2].astype(I32)
    tok = jnp.arange(s_dim, dtype=I32)
    src_tok = jnp.zeros((n_rows,), I32).at[dest1].set(tok).at[dest2].set(tok)
    tile_start = jnp.arange(n_rows // tm, dtype=I32) * tm
    tile_expert = jnp.minimum(jnp.searchsorted(ends, tile_start, side="right"), N_EXPERTS - 1).astype(I32)
    n_used = (ends[-1:] // tm).astype(I32)
    return dict(tm=tm, n_rows=n_rows, dest=jnp.concatenate([dest1, dest2]), src_tok=src_tok,
                tile_expert=tile_expert, n_used=n_used)


def _row_copy(src_ref, src_row, dst_ref, dst_row, sem):
    return pltpu.make_async_copy(src_ref.at[pl.ds(src_row, 1), :], dst_ref.at[pl.ds(dst_row, 1), :], sem)


def _dispatch_kernel(src_tok_ref, x_ref, o_ref, buf, sem):
    i = pl.program_id(0)
    n = pl.num_programs(0)
    tm = buf.shape[1]

    def fetch(tile, slot):
        def issue(r, _):
            _row_copy(x_ref, src_tok_ref[tile * tm + r], buf.at[slot], r, sem.at[slot]).start()
            return 0
        lax.fori_loop(0, tm, issue, 0)

    @pl.when(i == 0)
    def _():
        fetch(0, 0)

    @pl.when(i + 1 < n)
    def _():
        fetch(i + 1, (i + 1) % 2)

    slot = i % 2
    pltpu.make_async_copy(x_ref.at[pl.ds(0, tm), :], buf.at[slot], sem.at[slot]).wait()
    o_ref[...] = buf[slot].astype(o_ref.dtype)


def _moe_dispatch(x, plan, *, tm=256):
    s_dim, d_dim = x.shape
    tm = min(tm, s_dim)
    n_rows = plan["n_rows"]
    return pl.pallas_call(
        _dispatch_kernel,
        grid_spec=pltpu.PrefetchScalarGridSpec(
            num_scalar_prefetch=1,
            grid=(n_rows // tm,),
            in_specs=[pl.BlockSpec(memory_space=pl.ANY)],
            out_specs=pl.BlockSpec((tm, d_dim), lambda i, st: (i, 0)),
            scratch_shapes=[pltpu.VMEM((2, tm, d_dim), F32), pltpu.SemaphoreType.DMA((2,))],
        ),
        out_shape=jax.ShapeDtypeStruct((n_rows, d_dim), BF16),
        compiler_params=_cparams("arbitrary"),
        name="moe_dispatch",
    )(plan["src_tok"], x)


def _moe_up_kernel(te_ref, nu_ref, a_ref, wg_ref, wu_ref, o_ref):
    m = pl.program_id(1)

    @pl.when(m < nu_ref[0])
    def _():
        a = a_ref[...]
        g = _dot(a, wg_ref[...].astype(BF16))
        u = _dot(a, wu_ref[...].astype(BF16))
        o_ref[...] = (g * _sigmoid(g) * u).astype(o_ref.dtype)

    @pl.when(m >= nu_ref[0])
    def _():
        o_ref[...] = jnp.zeros_like(o_ref)


def _moe_up(xs, wg, wu, layer, plan, *, tn=256):
    n_rows, d_dim = xs.shape
    tm = plan["tm"]
    w_spec = pl.BlockSpec((None, None, d_dim, tn), lambda n, m, te, nu: (layer, te[m], 0, n))
    return pl.pallas_call(
        _moe_up_kernel,
        grid_spec=pltpu.PrefetchScalarGridSpec(
            num_scalar_prefetch=2,
            grid=(D_FF_EXPERT // tn, n_rows // tm),
            in_specs=[pl.BlockSpec((tm, d_dim), lambda n, m, te, nu: (m, 0)), w_spec, w_spec],
            out_specs=pl.BlockSpec((tm, tn), lambda n, m, te, nu: (m, n)),
        ),
        out_shape=jax.ShapeDtypeStruct((n_rows, D_FF_EXPERT), BF16),
        compiler_params=_cparams("parallel", "arbitrary"),
        name="moe_up",
    )(plan["tile_expert"], plan["n_used"], xs, wg, wu)


def _moe_down_kernel(te_ref, nu_ref, a_ref, w_ref, o_ref):
    m = pl.program_id(1)

    @pl.when(m < nu_ref[0])
    def _():
        o_ref[...] = _dot(a_ref[...], w_ref[...].astype(BF16))

    @pl.when(m >= nu_ref[0])
    def _():
        o_ref[...] = jnp.zeros_like(o_ref)


def _moe_down(hid, wd, layer, plan, *, tn=1024):
    n_rows, f_dim = hid.shape
    tm = plan["tm"]
    d_dim = wd.shape[-1]
    return pl.pallas_call(
        _moe_down_kernel,
        grid_spec=pltpu.PrefetchScalarGridSpec(
            num_scalar_prefetch=2,
            grid=(d_dim // tn, n_rows // tm),
            in_specs=[pl.BlockSpec((tm, f_dim), lambda n, m, te, nu: (m, 0)),
                      pl.BlockSpec((None, None, f_dim, tn), lambda n, m, te, nu: (layer, te[m], 0, n))],
            out_specs=pl.BlockSpec((tm, tn), lambda n, m, te, nu: (m, n)),
        ),
        out_shape=jax.ShapeDtypeStruct((n_rows, d_dim), F32),
        compiler_params=_cparams("parallel", "arbitrary"),
        name="moe_down",
    )(plan["tile_expert"], plan["n_used"], hid, wd)


def _combine_ln_kernel(dest_ref, x_ref, r_ref, ys_ref, g_ref, b_ref, o_ref, ob_ref, buf, sem, *, s_dim):
    i = pl.program_id(0)
    n = pl.num_programs(0)
    tb = x_ref.shape[0]

    def fetch(blk, slot):
        def issue(r, _):
            t = blk * tb + r
            _row_copy(ys_ref, dest_ref[t], buf.at[slot, 0], r, sem.at[slot]).start()
            _row_copy(ys_ref, dest_ref[s_dim + t], buf.at[slot, 1], r, sem.at[slot]).start()
            return 0
        lax.fori_loop(0, tb, issue, 0)

    @pl.when(i == 0)
    def _():
        fetch(0, 0)

    @pl.when(i + 1 < n)
    def _():
        fetch(i + 1, (i + 1) % 2)

    slot = i % 2
    for k in range(2):
        pltpu.make_async_copy(ys_ref.at[pl.ds(0, tb), :], buf.at[slot, k], sem.at[slot]).wait()
    rec = r_ref[...]
    f = rec[:, ROUTE_G1:ROUTE_G1 + 1] * buf[slot, 0] + rec[:, ROUTE_G2:ROUTE_G2 + 1] * buf[slot, 1]
    y = DEEPNORM_ALPHA * x_ref[...] + f
    mu = jnp.mean(y, axis=-1, keepdims=True)
    d = y - mu
    var = jnp.mean(d * d, axis=-1, keepdims=True)
    out = d * lax.rsqrt(var + LN_EPS) * g_ref[...] + b_ref[...]
    o_ref[...] = out
    ob_ref[...] = out.astype(BF16)


def _moe_combine_ln(x, route, ys, plan, g3, b3, idx, *, tb=256):
    s_dim, d_dim = x.shape
    tb = min(tb, s_dim)
    row = pl.BlockSpec((tb, d_dim), lambda i, de: (i, 0))
    par = pl.BlockSpec((None, 1, d_dim), lambda i, de: (idx, 0, 0))
    return pl.pallas_call(
        functools.partial(_combine_ln_kernel, s_dim=s_dim),
        grid_spec=pltpu.PrefetchScalarGridSpec(
            num_scalar_prefetch=1,
            grid=(s_dim // tb,),
            in_specs=[row, pl.BlockSpec((tb, LANES), lambda i, de: (i, 0)), pl.BlockSpec(memory_space=pl.ANY),
                      par, par],
            out_specs=[row, row],
            scratch_shapes=[pltpu.VMEM((2, 2, tb, d_dim), F32), pltpu.SemaphoreType.DMA((2,))],
        ),
        out_shape=[jax.ShapeDtypeStruct((s_dim, d_dim), F32), jax.ShapeDtypeStruct((s_dim, d_dim), BF16)],
        compiler_params=_cparams("arbitrary"),
        name="moe_combine_ln",
    )(plan["dest"], x, route, ys, g3, b3)


PROJ_TILES = dict(tm=2048, tn=512)


def _even_layer(x, xb, j, i, even_w_in, even_w_out, ffn_w_gate, ffn_w_up, ffn_w_down, ln_g3, ln_b3, tabs):
    d = D_MODEL
    p = _matmul_rows(xb, even_w_in, lambda n: (j, 0, n), EVEN_IN, name="even_in_proj", **PROJ_TILES)
    a = _stick_breaking(p)
    r = _retention(p, tabs["log_gamma"], tabs["ret_cos"], tabs["ret_sin"])
    y = jnp.concatenate([a, r], axis=1)
    h = _matmul_rows(y, even_w_out, lambda n: (j, 0, n), d, name="out_proj", **PROJ_TILES)
    x, xb = _deepnorm_ln(x, h, ln_g3, ln_b3, 2 * i)
    hid = _swiglu_rows(xb, ffn_w_gate, ffn_w_up, lambda n: (j, 0, n), D_FF, tm=2048, tn=256, name="ffn_up")
    f = _matmul(hid, ffn_w_down, lambda n, k: (j, k, n), d, D_FF, tm=2048, tn=2048, tk=256, name="ffn_down")
    return _deepnorm_ln(x, f, ln_g3, ln_b3, 2 * i + 1)


def _odd_layer(x, xb, j, i, odd_w_in, odd_w_out, moe_w_router, moe_w_gate, moe_w_up, moe_w_down,
               ln_g3, ln_b3, tabs):
    d = D_MODEL
    s_dim = x.shape[0]
    p = _matmul_rows(xb, odd_w_in, lambda n: (j, 0, n), ODD_MAIN, name="odd_in_proj", **PROJ_TILES)
    w_tail = jnp.pad(odd_w_in[j, :, ODD_MAIN:], ((0, 0), (0, LANES - ODD_TAIL)))
    tail = _matmul(xb, w_tail, lambda n, k: (k, n), LANES, d, tm=2048, tn=LANES, tk=2048, name="odd_in_tail")

    cos_h, sin_h = tabs["head_cos"], tabs["head_sin"]
    q = _rope_cast(p, 0, DSA_Q_W, cos_h, sin_h, half=HEAD_DIM // ROT_FRACTION // 2,
                   scale=HEAD_DIM ** -0.5 * LOG2_E)
    k = _rope_cast(p, DSA_Q_W, DSA_KV_W, cos_h, sin_h, half=HEAD_DIM // ROT_FRACTION // 2, scale=1.0)
    v = _cast(p, DSA_Q_W + DSA_KV_W, DSA_KV_W)
    qi = _idx_q_prep(p, DSA_Q_W + 2 * DSA_KV_W, tabs["idx_cos"], tabs["idx_sin"])
    ki, wi = _idx_tail_prep(tail, tabs["tail_cos"], tabs["tail_sin"])

    bias = _dsa_select(qi, ki, wi, min(IDX_TOPK_MAX, s_dim // 4))
    y = _dsa_attention(q, k, v, bias)
    h = _matmul_rows(y, odd_w_out, lambda n: (j, 0, n), d, name="out_proj", **PROJ_TILES)
    x, xb = _deepnorm_ln(x, h, ln_g3, ln_b3, 2 * i)

    w_router = jnp.pad(moe_w_router[j], ((0, 0), (0, LANES - N_EXPERTS)))
    route, counts = _router(x, w_router)
    plan = _moe_plan(route, counts, s_dim)
    xs = _moe_dispatch(x, plan)
    hid = _moe_up(xs, moe_w_gate, moe_w_up, j, plan)
    ys = _moe_down(hid, moe_w_down, j, plan)
    return _moe_combine_ln(x, route, ys, plan, ln_g3, ln_b3, 2 * i + 1)


def kernel(x, even_w_in, even_w_out, odd_w_in, odd_w_out, ffn_w_gate, ffn_w_up, ffn_w_down, moe_w_router,
           moe_w_gate, moe_w_up, moe_w_down, ln_g, ln_b):
    batch, s_dim, d = x.shape
    ln_g3 = ln_g.reshape(2 * DEPTH, 1, d)
    ln_b3 = ln_b.reshape(2 * DEPTH, 1, d)
    ret_cos, ret_sin = _rope_tables(s_dim, LANES, RET_QK_DIM, RET_THETA)
    head_cos, head_sin = _rope_tables(s_dim, LANES, HEAD_DIM // ROT_FRACTION, ROPE_THETA)
    idx_cos, idx_sin = _rope_tables(s_dim, IDX_DIM, IDX_DIM // ROT_FRACTION, ROPE_THETA)
    tail_cos, tail_sin = _rope_tables(s_dim, IDX_DIM, IDX_DIM // ROT_FRACTION, ROPE_THETA, active=IDX_DIM)
    tabs = dict(
        log_gamma=jnp.log1p(-jnp.exp2(-5.0 - jnp.arange(RET_HEADS, dtype=F32))),
        ret_cos=ret_cos, ret_sin=ret_sin, head_cos=head_cos, head_sin=head_sin,
        idx_cos=idx_cos, idx_sin=idx_sin, tail_cos=tail_cos, tail_sin=tail_sin,
    )
    outs = []
    for b in range(batch):
        xs = x[b]
        xb = xs.astype(BF16)
        for i in range(DEPTH):
            j = i // 2
            if i % 2 == 0:
                xs, xb = _even_layer(xs, xb, j, i, even_w_in, even_w_out, ffn_w_gate, ffn_w_up, ffn_w_down,
                                     ln_g3, ln_b3, tabs)
            else:
                xs, xb = _odd_layer(xs, xb, j, i, odd_w_in, odd_w_out, moe_w_router, moe_w_gate, moe_w_up,
                                    moe_w_down, ln_g3, ln_b3, tabs)
        outs.append(xs)
    return jnp.stack(outs, axis=0)
```

```python
import functools
import math

import jax
import jax.numpy as jnp
import numpy as np
from jax import lax
from jax.experimental import pallas as pl
from jax.experimental.pallas import tpu as pltpu

F32 = jnp.float32
BF16 = jnp.bfloat16
I32 = jnp.int32

D_MODEL = 4096
DEPTH = 4
HEAD_DIM = 128
SB_HEADS = 16
SB_W = SB_HEADS * HEAD_DIM
RET_HEADS = 8
RET_QK_DIM = 128
RET_V_DIM = 256
RET_QK_W = RET_HEADS * RET_QK_DIM
RET_V_W = RET_HEADS * RET_V_DIM
RET_THETA = 10000.0
DSA_Q_HEADS = 32
DSA_KV_HEADS = 8
DSA_GROUP = DSA_Q_HEADS // DSA_KV_HEADS
DSA_Q_W = DSA_Q_HEADS * HEAD_DIM
DSA_KV_W = DSA_KV_HEADS * HEAD_DIM
IDX_HEADS = 16
IDX_DIM = 64
IDX_Q_W = IDX_HEADS * IDX_DIM
IDX_TOPK_MAX = 256
ROPE_THETA = 500000.0
ROT_FRACTION = 4
D_FF = 11008
N_EXPERTS = 8
D_FF_EXPERT = 1792
LN_EPS = 1e-5
DEEPNORM_ALPHA = (2.0 * DEPTH) ** 0.25

EVEN_IN = 3 * SB_W + 2 * RET_QK_W + 2 * RET_V_W
ODD_MAIN = DSA_Q_W + 2 * DSA_KV_W + IDX_Q_W
ODD_TAIL = IDX_DIM + IDX_HEADS

LANES = 128
VMEM_LIMIT_BYTES = 56 * 1024 * 1024

INT_MIN = -2147483648
MASK_NEG = -(2.0 ** 100)
LOG2_E = math.log2(math.e)


def _cparams(*sem):
    return pltpu.CompilerParams(dimension_semantics=sem, vmem_limit_bytes=VMEM_LIMIT_BYTES)


def _sigmoid(x):
    return 1.0 / (1.0 + jnp.exp(-x))


def _dot(a, b):
    return jnp.dot(a, b, preferred_element_type=F32)


def _dot_nt(a, b):
    return lax.dot_general(a, b, (((1,), (1,)), ((), ())), preferred_element_type=F32)


def _mm_rows_kernel(*refs, w_is_nk):
    *a_refs, w_ref, o_ref = refs
    k0, acc = 0, None
    for a_ref in a_refs:
        kw = a_ref.shape[1]
        if w_is_nk:
            part = _dot_nt(a_ref[...], w_ref[:, k0:k0 + kw].astype(BF16))
        else:
            part = _dot(a_ref[...], w_ref[k0:k0 + kw, :].astype(BF16))
        acc = part if acc is None else acc + part
        k0 += kw
    o_ref[...] = acc


def _matmul_rows(panels, w, w_index, n_cols, *, tm, tn, name, w_is_nk=False):
    m_dim = panels[0].shape[0]
    k_dim = sum(a.shape[1] for a in panels)
    tm = min(tm, m_dim)
    n_lead = w.ndim - 2
    w_block = (tn, k_dim) if w_is_nk else (k_dim, tn)
    return pl.pallas_call(
        functools.partial(_mm_rows_kernel, w_is_nk=w_is_nk),
        grid=(m_dim // tm, n_cols // tn),
        in_specs=[pl.BlockSpec((tm, a.shape[1]), lambda m, n: (m, 0), pipeline_mode=pl.Buffered(1))
                  for a in panels]
        + [pl.BlockSpec((None,) * n_lead + w_block, lambda m, n: w_index(n))],
        out_specs=pl.BlockSpec((tm, tn), lambda m, n: (m, n)),
        out_shape=jax.ShapeDtypeStruct((m_dim, n_cols), F32),
        compiler_params=_cparams("parallel", "arbitrary"),
        name=name,
    )(*panels, w)


def _swiglu_rows_kernel(a_ref, wg_ref, wu_ref, o_ref):
    a = a_ref[...]
    g = _dot(a, wg_ref[...].astype(BF16))
    u = _dot(a, wu_ref[...].astype(BF16))
    o_ref[...] = (g * _sigmoid(g) * u).astype(o_ref.dtype)


def _swiglu_rows(a, wg, wu, w_index, n_cols, *, tm, tn, name):
    m_dim, k_dim = a.shape
    tm = min(tm, m_dim)
    n_lead = wg.ndim - 2
    w_spec = pl.BlockSpec((None,) * n_lead + (k_dim, tn), lambda m, n: w_index(n))
    return pl.pallas_call(
        _swiglu_rows_kernel,
        grid=(m_dim // tm, n_cols // tn),
        in_specs=[pl.BlockSpec((tm, k_dim), lambda m, n: (m, 0), pipeline_mode=pl.Buffered(1)), w_spec, w_spec],
        out_specs=pl.BlockSpec((tm, tn), lambda m, n: (m, n)),
        out_shape=jax.ShapeDtypeStruct((m_dim, n_cols), BF16),
        compiler_params=_cparams("parallel", "arbitrary"),
        name=name,
    )(a, wg, wu)


def _ln_kernel(x_ref, h_ref, g_ref, b_ref, o_ref, ob_ref):
    y = DEEPNORM_ALPHA * x_ref[...] + h_ref[...]
    mu = jnp.mean(y, axis=-1, keepdims=True)
    d = y - mu
    var = jnp.mean(d * d, axis=-1, keepdims=True)
    out = d * lax.rsqrt(var + LN_EPS) * g_ref[...] + b_ref[...]
    o_ref[...] = out
    ob_ref[...] = out.astype(BF16)


def _deepnorm_ln(x, h, g3, b3, idx, *, tm=256):
    s_dim, d_dim = x.shape
    tm = min(tm, s_dim)
    row = pl.BlockSpec((tm, d_dim), lambda i: (i, 0))
    par = pl.BlockSpec((None, 1, d_dim), lambda i: (idx, 0, 0))
    return pl.pallas_call(
        _ln_kernel,
        grid=(s_dim // tm,),
        in_specs=[row, row, par, par],
        out_specs=[row, row],
        out_shape=[jax.ShapeDtypeStruct((s_dim, d_dim), F32), jax.ShapeDtypeStruct((s_dim, d_dim), BF16)],
        compiler_params=_cparams("parallel"),
        name="deepnorm_ln",
    )(x, h, g3, b3)


def _rope_tables(s_dim, period, rot_dim, theta, active=LANES):
    half = rot_dim // 2
    pos = jnp.arange(s_dim, dtype=F32)
    freqs = jnp.exp(-math.log(theta) * jnp.arange(half, dtype=F32) * (2.0 / rot_dim))
    ang = pos[:, None] * freqs[None, :]
    cos, sin = jnp.cos(ang), jnp.sin(ang)
    ones = jnp.ones((s_dim, period - rot_dim), F32)
    zeros = jnp.zeros((s_dim, period - rot_dim), F32)
    cos_p = jnp.concatenate([cos, cos, ones], axis=1)
    sin_p = jnp.concatenate([-sin, sin, zeros], axis=1)
    reps = LANES // period
    cos_t, sin_t = jnp.tile(cos_p, (1, reps)), jnp.tile(sin_p, (1, reps))
    if active < LANES:
        lane = jnp.arange(LANES)[None, :]
        cos_t = jnp.where(lane < active, cos_t, 1.0)
        sin_t = jnp.where(lane < active, sin_t, 0.0)
    return cos_t, sin_t


def _rope_tile(x, cos, sin, half, period=LANES):
    if 2 * half == LANES:
        return x * cos + pltpu.roll(x, half, 1) * sin
    lane = lax.broadcasted_iota(I32, x.shape, 1)
    first = (lane & (period - 1)) < half
    partner = jnp.where(first, pltpu.roll(x, LANES - half, 1), pltpu.roll(x, half, 1))
    return x * cos + partner * sin


SB_BLOCK = 256
SB_GROUP = 2
SB_DEAD = 160.0


def _sb_kernel(q_ref, k_ref, v_ref, o_ref, kb_ref, vb_ref):
    i = pl.program_id(1)
    blk = SB_BLOCK
    span = SB_GROUP * blk

    @pl.when(i == 0)
    def _():
        kb_ref[...] = k_ref[...].astype(BF16)
        vb_ref[...] = v_ref[...].astype(BF16)

    q = (q_ref[...] * (HEAD_DIM ** -0.5 * LOG2_E)).astype(BF16)
    row = lax.broadcasted_iota(I32, (blk, blk), 0)
    col = lax.broadcasted_iota(I32, (blk, blk), 1)
    later = jnp.where(row > col, 1.0, 0.0).astype(BF16)

    def group(start, acc, run, key_end):
        z = _dot_nt(q, kb_ref[pl.ds(start, span), :])
        soft = jnp.log(1.0 + jnp.exp2(-jnp.abs(z))) * LOG2_E
        sp = jnp.maximum(z, 0.0) + soft
        log_beta = z - sp
        if key_end is not None:
            key_pos = start + lax.broadcasted_iota(I32, (blk, span), 1)
            q_pos = i * blk + lax.broadcasted_iota(I32, (blk, span), 0)
            causal = (key_pos < q_pos) & (key_pos < key_end)
            sp = jnp.where(causal, sp, 0.0)
        parts = [None] * SB_GROUP
        for c in reversed(range(SB_GROUP)):
            sp_c = sp[:, c * blk:(c + 1) * blk]
            parts[c] = _dot(sp_c.astype(BF16), later) + run
            run = run + jnp.sum(sp_c, axis=1, keepdims=True)
        w = jnp.exp2(log_beta - jnp.concatenate(parts, axis=1))
        if key_end is not None:
            w = jnp.where(causal, w, 0.0)
        acc = acc + _dot(w.astype(BF16), vb_ref[pl.ds(start, span), :])
        return acc, run

    below = jnp.maximum(i - (SB_GROUP - 1), 0)
    n_full = below // SB_GROUP
    n_left = below - n_full * SB_GROUP
    acc0 = jnp.zeros((blk, HEAD_DIM), F32)
    run0 = jnp.zeros((blk, 1), F32)
    acc, run = group(pl.multiple_of(below * blk, blk), acc0, run0, (i + 1) * blk)

    def live(run):
        return jnp.min(run) < SB_DEAD

    def cond(carry):
        return jnp.logical_and(carry[0] < n_full, live(carry[2]))

    def body(carry):
        t, acc, run = carry
        start = pl.multiple_of((below - SB_GROUP * (t + 1)) * blk, blk)
        acc, run = group(start, acc, run, None)
        return t + 1, acc, run

    t, acc, run = lax.while_loop(cond, body, (jnp.int32(0), acc, run))
    tail = jnp.logical_and(jnp.logical_and(t == n_full, n_left > 0), live(run))
    acc, run = lax.cond(tail, lambda: group(0, acc, run, n_left * blk), lambda: (acc, run))
    o_ref[...] = acc.astype(o_ref.dtype)


def _stick_breaking(p):
    s_dim = p.shape[0]
    blk = SB_BLOCK
    kv = lambda off: pl.BlockSpec((s_dim, HEAD_DIM), lambda h, i: (0, off + h))
    return pl.pallas_call(
        _sb_kernel,
        grid=(SB_HEADS, s_dim // blk),
        in_specs=[pl.BlockSpec((blk, HEAD_DIM), lambda h, i: (i, h)), kv(SB_HEADS), kv(2 * SB_HEADS)],
        out_specs=pl.BlockSpec((blk, HEAD_DIM), lambda h, i: (i, h)),
        out_shape=jax.ShapeDtypeStruct((s_dim, SB_W), BF16),
        scratch_shapes=[pltpu.VMEM((s_dim, HEAD_DIM), BF16), pltpu.VMEM((s_dim, HEAD_DIM), BF16)],
        compiler_params=_cparams("parallel", "arbitrary"),
        name="stick_breaking",
    )(p, p, p)


RET_CHUNK = 512


def _ret_kernel(lg_ref, q_ref, k_ref, v_ref, g_ref, cos_ref, sin_ref, o_ref, state):
    h = pl.program_id(0)
    c = pl.program_id(1)
    n = RET_CHUNK

    @pl.when(c == 0)
    def _():
        state[...] = jnp.zeros_like(state)

    lg = lg_ref[h]
    cos, sin = cos_ref[...], sin_ref[...]
    q = _rope_tile(q_ref[...], cos, sin, RET_QK_DIM // 2)
    k = _rope_tile(k_ref[...], cos, sin, RET_QK_DIM // 2) * (RET_QK_DIM ** -0.5)
    v = v_ref[...].astype(BF16)

    ii = lax.broadcasted_iota(I32, (n, n), 0)
    jj = lax.broadcasted_iota(I32, (n, n), 1)
    rel = (ii - jj).astype(F32)
    decay = jnp.where(rel >= 0.0, jnp.exp(lg * jnp.maximum(rel, 0.0)), 0.0)
    inner = _dot_nt(q.astype(BF16), k.astype(BF16)) * decay
    out = _dot(inner.astype(BF16), v)

    pos = lax.broadcasted_iota(I32, (n, 1), 0).astype(F32)
    q_decay = jnp.exp(lg * (pos + 1.0))
    k_decay = jnp.exp(lg * (n - 1.0 - pos))
    prev = state[...]
    out = out + _dot((q * q_decay).astype(BF16), prev.astype(BF16))
    kd_t = jnp.transpose(k * k_decay).astype(BF16)
    state[...] = jnp.exp(lg * jnp.full((1, 1), n, F32)) * prev + _dot(kd_t, v)

    mu = jnp.mean(out, axis=-1, keepdims=True)
    d = out - mu
    var = jnp.mean(d * d, axis=-1, keepdims=True)
    g = g_ref[...]
    o_ref[...] = (g * _sigmoid(g) * (d * lax.rsqrt(var + LN_EPS))).astype(o_ref.dtype)


def _retention(p, log_gamma, cos, sin):
    s_dim = p.shape[0]
    n = RET_CHUNK
    qk = lambda off: pl.BlockSpec((n, RET_QK_DIM), lambda h, c, lg: (c, off + h))
    vg = lambda off: pl.BlockSpec((n, RET_V_DIM), lambda h, c, lg: (c, off + h))
    tab = pl.BlockSpec((n, LANES), lambda h, c, lg: (c, 0))
    q_off = 3 * SB_W // RET_QK_DIM
    v_off = (3 * SB_W + 2 * RET_QK_W) // RET_V_DIM
    grid_spec = pltpu.PrefetchScalarGridSpec(
        num_scalar_prefetch=1,
        grid=(RET_HEADS, s_dim // n),
        in_specs=[qk(q_off), qk(q_off + RET_HEADS), vg(v_off), vg(v_off + RET_HEADS), tab, tab],
        out_specs=pl.BlockSpec((n, RET_V_DIM), lambda h, c, lg: (c, h)),
        scratch_shapes=[pltpu.VMEM((RET_QK_DIM, RET_V_DIM), F32)],
    )
    return pl.pallas_call(
        _ret_kernel,
        grid_spec=grid_spec,
        out_shape=jax.ShapeDtypeStruct((s_dim, RET_V_W), BF16),
        compiler_params=_cparams("parallel", "arbitrary"),
        name="retention",
    )(log_gamma, p, p, p, p, cos, sin)


def _rope_cast_kernel(x_ref, cos_ref, sin_ref, o_ref, *, half, scale):
    cos, sin = cos_ref[...], sin_ref[...]
    for t in range(x_ref.shape[1] // LANES):
        sl = slice(t * LANES, (t + 1) * LANES)
        o_ref[:, sl] = (_rope_tile(x_ref[:, sl], cos, sin, half) * scale).astype(o_ref.dtype)


def _rope_cast(p, col0, n_cols, cos, sin, *, half, scale, tm=512, tn=1024):
    s_dim = p.shape[0]
    tm = min(tm, s_dim)
    tn = min(tn, n_cols)
    tab = pl.BlockSpec((tm, LANES), lambda i, j: (i, 0))
    return pl.pallas_call(
        functools.partial(_rope_cast_kernel, half=half, scale=scale),
        grid=(s_dim // tm, n_cols // tn),
        in_specs=[pl.BlockSpec((tm, tn), lambda i, j: (i, col0 // tn + j)), tab, tab],
        out_specs=pl.BlockSpec((tm, tn), lambda i, j: (i, j)),
        out_shape=jax.ShapeDtypeStruct((s_dim, n_cols), BF16),
        compiler_params=_cparams("parallel", "parallel"),
        name="rope_cast",
    )(p, cos, sin)


def _cast_kernel(x_ref, o_ref):
    o_ref[...] = x_ref[...].astype(o_ref.dtype)


def _cast(p, col0, n_cols, *, tm=512, tn=1024):
    s_dim = p.shape[0]
    tm = min(tm, s_dim)
    return pl.pallas_call(
        _cast_kernel,
        grid=(s_dim // tm, n_cols // tn),
        in_specs=[pl.BlockSpec((tm, tn), lambda i, j: (i, col0 // tn + j))],
        out_specs=pl.BlockSpec((tm, tn), lambda i, j: (i, j)),
        out_shape=jax.ShapeDtypeStruct((s_dim, n_cols), BF16),
        compiler_params=_cparams("parallel", "parallel"),
        name="cast_bf16",
    )(p)


def _idx_q_kernel(x_ref, cos_ref, sin_ref, o_ref):
    cos, sin = cos_ref[...], sin_ref[...]
    lane = lax.broadcasted_iota(I32, cos.shape, 1)
    low = lane < IDX_DIM
    for t in range(x_ref.shape[1] // LANES):
        y = _rope_tile(x_ref[:, t * LANES:(t + 1) * LANES], cos, sin, IDX_DIM // ROT_FRACTION // 2, IDX_DIM)
        o_ref[:, (2 * t) * LANES:(2 * t + 1) * LANES] = jnp.where(low, y, 0.0).astype(o_ref.dtype)
        o_ref[:, (2 * t + 1) * LANES:(2 * t + 2) * LANES] = jnp.where(
            low, pltpu.roll(y, IDX_DIM, 1), 0.0).astype(o_ref.dtype)


def _idx_q_prep(p, col0, cos, sin, *, tm=512):
    s_dim = p.shape[0]
    tm = min(tm, s_dim)
    tab = pl.BlockSpec((tm, LANES), lambda i: (i, 0))
    return pl.pallas_call(
        _idx_q_kernel,
        grid=(s_dim // tm,),
        in_specs=[pl.BlockSpec((tm, IDX_Q_W), lambda i: (i, col0 // IDX_Q_W)), tab, tab],
        out_specs=pl.BlockSpec((tm, IDX_HEADS * LANES), lambda i: (i, 0)),
        out_shape=jax.ShapeDtypeStruct((s_dim, IDX_HEADS * LANES), BF16),
        compiler_params=_cparams("parallel"),
        name="idx_q_prep",
    )(p, cos, sin)


def _idx_tail_kernel(x_ref, cos_ref, sin_ref, k_ref, w_ref):
    x = x_ref[...]
    lane = lax.broadcasted_iota(I32, x.shape, 1)
    y = _rope_tile(x, cos_ref[...], sin_ref[...], IDX_DIM // ROT_FRACTION // 2, IDX_DIM) * (IDX_DIM ** -0.5)
    k_ref[...] = jnp.where(lane < IDX_DIM, y, 0.0).astype(k_ref.dtype)
    w_ref[...] = x * (IDX_HEADS ** -0.5)


def _idx_tail_prep(tail, cos, sin, *, tm=512):
    s_dim = tail.shape[0]
    tm = min(tm, s_dim)
    blk = pl.BlockSpec((tm, LANES), lambda i: (i, 0))
    return pl.pallas_call(
        _idx_tail_kernel,
        grid=(s_dim // tm,),
        in_specs=[blk, blk, blk],
        out_specs=[blk, blk],
        out_shape=[jax.ShapeDtypeStruct((s_dim, LANES), BF16), jax.ShapeDtypeStruct((s_dim, LANES), F32)],
        compiler_params=_cparams("parallel"),
        name="idx_tail_prep",
    )(tail, cos, sin)


DSA_QB = 128
DSA_KB = 512
DSA_AQB = 512


def _select_kernel(qi_ref, ki_ref, w_ref, o_ref, keys_ref, *, topk, n_kblocks, idx_bits):
    i = pl.program_id(0)
    qb, kb = DSA_QB, DSA_KB
    n_live = ((i + 1) * qb + kb - 1) // kb
    q_pos = i * qb + lax.broadcasted_iota(I32, (qb, kb), 0)
    col0 = lax.broadcasted_iota(I32, (qb, kb), 1)
    w = w_ref[...]

    def score_block(jb, _):
        start = pl.multiple_of(jb * kb, kb)
        kj = ki_ref[pl.ds(start, kb), :]
        acc = jnp.zeros((qb, kb), F32)
        for h in range(IDX_HEADS):
            z = _dot_nt(qi_ref[:, h * LANES:(h + 1) * LANES], kj)
            acc = acc + jnp.maximum(z, 0.0) * w[:, IDX_DIM + h:IDX_DIM + h + 1]
        acc = jnp.where(acc == 0.0, 0.0, acc)
        bits = lax.bitcast_convert_type(acc, I32)
        key = jnp.where(bits >= 0, bits, bits ^ 0x7FFFFFFF)
        keys_ref[jb] = jnp.where(jb * kb + col0 <= q_pos, key, INT_MIN)
        return 0

    lax.fori_loop(0, n_live, score_block, 0)

    def count(pred):
        def body(jb, part):
            kk = keys_ref[jb]
            hit = jnp.where(pred(kk, jb * kb + col0), 1.0, 0.0)
            for t in range(kb // LANES):
                part = part + hit[:, t * LANES:(t + 1) * LANES]
            return part
        part = lax.fori_loop(0, n_live, body, jnp.zeros((qb, LANES), F32))
        return jnp.sum(part, axis=1, keepdims=True)

    k_f = float(topk)
    c_nonneg = count(lambda kk, cc: kk >= 0)
    thr = jnp.where(c_nonneg >= k_f, jnp.int32(0), jnp.int32(INT_MIN))

    def thr_bit(b, thr):
        cand = thr + jnp.left_shift(jnp.int32(1), 30 - b)
        c = count(lambda kk, cc: kk >= cand)
        return jnp.where(c >= k_f, cand, thr)

    thr = lax.fori_loop(0, 31, thr_bit, thr)

    need = k_f - count(lambda kk, cc: kk > thr)
    n_eq = count(lambda kk, cc: kk == thr)
    tied = (n_eq > need) & (thr != INT_MIN)

    def cut_bit(b, cut):
        cand = cut + jnp.left_shift(jnp.int32(1), idx_bits - 1 - b)
        c = count(lambda kk, cc: (kk == thr) & (cc < cand))
        return jnp.where(c < need, cand, cut)

    def cut_search():
        return lax.fori_loop(0, idx_bits, cut_bit, jnp.zeros((qb, 1), I32))

    def no_cut():
        return jnp.full((qb, 1), n_kblocks * kb, I32)

    cut = lax.cond(jnp.max(jnp.where(tied, 1.0, 0.0)) > 0.0, cut_search, no_cut)

    def write_live(jb, _):
        kk = keys_ref[jb]
        cc = jb * kb + col0
        sel = ((kk > thr) | ((kk == thr) & (cc <= cut))) & (kk != INT_MIN)
        o_ref[jb] = jnp.where(sel, 0.0, MASK_NEG).astype(o_ref.dtype)
        return 0

    lax.fori_loop(0, n_live, write_live, 0)

    def write_dead(jb, _):
        o_ref[jb] = jnp.full((qb, kb), MASK_NEG, o_ref.dtype)
        return 0

    lax.fori_loop(n_live, n_kblocks, write_dead, 0)


def _dsa_select(qi, ki, w, topk):
    s_dim = qi.shape[0]
    qb, kb = DSA_QB, DSA_KB
    n_kblocks = s_dim // kb
    idx_bits = max(1, (s_dim - 1).bit_length())
    return pl.pallas_call(
        functools.partial(_select_kernel, topk=topk, n_kblocks=n_kblocks, idx_bits=idx_bits),
        grid=(s_dim // qb,),
        in_specs=[
            pl.BlockSpec((qb, IDX_HEADS * LANES), lambda i: (i, 0)),
            pl.BlockSpec((s_dim, LANES), lambda i: (0, 0)),
            pl.BlockSpec((qb, LANES), lambda i: (i, 0)),
        ],
        out_specs=pl.BlockSpec((None, n_kblocks, qb, kb), lambda i: (i, 0, 0, 0)),
        out_shape=jax.ShapeDtypeStruct((s_dim // qb, n_kblocks, qb, kb), BF16),
        scratch_shapes=[pltpu.VMEM((n_kblocks, qb, kb), I32)],
        compiler_params=_cparams("parallel"),
        name="dsa_select",
    )(qi, ki, w)


def _dsa_attn_kernel(q_ref, k_ref, v_ref, b_ref, o_ref):
    i = pl.program_id(1)
    qb, kb, grp = DSA_AQB, DSA_KB, DSA_GROUP
    n_sel = qb // DSA_QB
    n_live = ((i + 1) * qb + kb - 1) // kb
    q = jnp.concatenate([q_ref[:, r * HEAD_DIM:(r + 1) * HEAD_DIM] for r in range(grp)], axis=0)

    def body(jb, carry):
        m, l, acc = carry
        start = pl.multiple_of(jb * kb, kb)
        kj = k_ref[pl.ds(start, kb), :]
        vj = v_ref[pl.ds(start, kb), :]
        z = _dot_nt(q, kj).astype(BF16)
        bias = b_ref[:, jb].reshape(1, qb, kb)
        z = (z.reshape(grp, qb, kb) + bias).reshape(grp * qb, kb)
        m_new = jnp.maximum(m, jnp.max(z, axis=1, keepdims=True).astype(F32))
        p = jnp.exp2(z - m_new.astype(BF16))
        alpha = jnp.exp2(m - m_new)
        l = alpha * l + jnp.sum(p.astype(F32), axis=1, keepdims=True)
        acc = alpha * acc + _dot(p, vj)
        return m_new, l, acc

    rows = grp * qb
    init = (jnp.full((rows, 1), MASK_NEG, F32), jnp.zeros((rows, 1), F32), jnp.zeros((rows, HEAD_DIM), F32))
    _, l, acc = lax.fori_loop(0, n_live, body, init)
    out = acc / l
    for r in range(grp):
        o_ref[:, r * HEAD_DIM:(r + 1) * HEAD_DIM] = out[r * qb:(r + 1) * qb, :].astype(o_ref.dtype)


def _dsa_attention(q, k, v, bias):
    s_dim = q.shape[0]
    qb, kb = DSA_AQB, DSA_KB
    gw = DSA_GROUP * HEAD_DIM
    kv = pl.BlockSpec((s_dim, HEAD_DIM), lambda g, i: (0, g))
    return pl.pallas_call(
        _dsa_attn_kernel,
        grid=(DSA_KV_HEADS, s_dim // qb),
        in_specs=[
            pl.BlockSpec((qb, gw), lambda g, i: (i, g)),
            kv, kv,
            pl.BlockSpec((qb // DSA_QB, s_dim // kb, DSA_QB, kb), lambda g, i: (i, 0, 0, 0)),
        ],
        out_specs=pl.BlockSpec((qb, gw), lambda g, i: (i, g)),
        out_shape=jax.ShapeDtypeStruct((s_dim, DSA_Q_W), BF16),
        compiler_params=_cparams("parallel", "arbitrary"),
        name="dsa_attention",
    )(q, k, v, bias)


ROUTE_E1, ROUTE_E2, ROUTE_G1, ROUTE_G2, ROUTE_R1, ROUTE_R2 = range(6)


def _router_kernel(x_ref, w_ref, o_ref, cnt_ref, seen):
    @pl.when(pl.program_id(0) == 0)
    def _():
        seen[...] = jnp.zeros_like(seen)

    x = x_ref[...]
    w = w_ref[...]
    xh = x.astype(BF16)
    xl = (x - xh.astype(F32)).astype(BF16)
    wh = w.astype(BF16)
    wl = (w - wh.astype(F32)).astype(BF16)
    logits = _dot(xh, wh) + (_dot(xh, wl) + _dot(xl, wh))
    lane = lax.broadcasted_iota(I32, logits.shape, 1).astype(F32)
    logits = jnp.where(lane < N_EXPERTS, logits, -jnp.inf)
    v1 = jnp.max(logits, axis=1, keepdims=True)
    i1 = jnp.min(jnp.where(logits == v1, lane, float(LANES)), axis=1, keepdims=True)
    rest = jnp.where(lane == i1, -jnp.inf, logits)
    v2 = jnp.max(rest, axis=1, keepdims=True)
    i2 = jnp.min(jnp.where(rest == v2, lane, float(LANES)), axis=1, keepdims=True)
    e2 = jnp.exp(v2 - v1)
    g1 = 1.0 / (1.0 + e2)
    g2 = e2 / (1.0 + e2)

    tb = x.shape[0]
    member = jnp.where((lane == i1) | (lane == i2), 1.0, 0.0)
    earlier = jnp.where(lax.broadcasted_iota(I32, (tb, tb), 1) < lax.broadcasted_iota(I32, (tb, tb), 0), 1.0, 0.0)
    prefix = _dot(earlier.astype(BF16), member.astype(BF16)) + seen[0:1, :]
    r1 = jnp.sum(jnp.where(lane == i1, prefix, 0.0), axis=1, keepdims=True)
    r2 = jnp.sum(jnp.where(lane == i2, prefix, 0.0), axis=1, keepdims=True)
    seen[...] = seen[...] + jnp.sum(member, axis=0, keepdims=True)
    cnt_ref[...] = seen[...]

    rec = jnp.zeros_like(logits)
    for slot, val in ((ROUTE_E1, i1), (ROUTE_E2, i2), (ROUTE_G1, g1), (ROUTE_G2, g2), (ROUTE_R1, r1), (ROUTE_R2, r2)):
        rec = jnp.where(lane == float(slot), val, rec)
    o_ref[...] = rec


def _router(x, w_pad, *, tm=256):
    s_dim, d_dim = x.shape
    tm = min(tm, s_dim)
    return pl.pallas_call(
        _router_kernel,
        grid=(s_dim // tm,),
        in_specs=[pl.BlockSpec((tm, d_dim), lambda i: (i, 0)), pl.BlockSpec((d_dim, LANES), lambda i: (0, 0))],
        out_specs=[pl.BlockSpec((tm, LANES), lambda i: (i, 0)), pl.BlockSpec((8, LANES), lambda i: (0, 0))],
        out_shape=[jax.ShapeDtypeStruct((s_dim, LANES), F32), jax.ShapeDtypeStruct((8, LANES), F32)],
        scratch_shapes=[pltpu.VMEM((8, LANES), F32)],
        compiler_params=_cparams("arbitrary"),
        name="moe_router",
    )(x, w_pad)


MOE_TM = 512


def _moe_plan(route, counts, s_dim):
    tm = min(MOE_TM, s_dim)
    n_rows = 2 * s_dim + N_EXPERTS * tm
    e1 = route[:, ROUTE_E1].astype(I32)
    e2 = route[:, ROUTE_E2].astype(I32)
    cnt = counts[0, :N_EXPERTS].astype(I32)
    padded = (cnt + tm - 1) // tm * tm
    ends = jnp.cumsum(padded)
    starts = ends - padded
    dest1 = starts[e1] + route[:, ROUTE_R1].astype(I32)
    dest2 = starts[e2] + route[:, ROUTE_R2].astype(I32)
    tok = jnp.arange(s_dim, dtype=I32)
    src_tok = jnp.zeros((n_rows,), I32).at[dest1].set(tok).at[dest2].set(tok)
    tile_start = jnp.arange(n_rows // tm, dtype=I32) * tm
    tile_expert = jnp.minimum(jnp.searchsorted(ends, tile_start, side="right"), N_EXPERTS - 1).astype(I32)
    n_used = (ends[-1:] // tm).astype(I32)
    return dict(tm=tm, n_rows=n_rows, dest=jnp.concatenate([dest1, dest2]), src_tok=src_tok,
                tile_expert=tile_expert, n_used=n_used)


def _row_copy(src_ref, src_row, dst_ref, dst_row, sem):
    return pltpu.make_async_copy(src_ref.at[pl.ds(src_row, 1), :], dst_ref.at[pl.ds(dst_row, 1), :], sem)


def _dispatch_kernel(src_tok_ref, x_ref, o_ref, buf, sem):
    i = pl.program_id(0)
    n = pl.num_programs(0)
    tm = buf.shape[1]

    def fetch(tile, slot):
        def issue(r, _):
            _row_copy(x_ref, src_tok_ref[tile * tm + r], buf.at[slot], r, sem.at[slot]).start()
            return 0
        lax.fori_loop(0, tm, issue, 0)

    @pl.when(i == 0)
    def _():
        fetch(0, 0)

    @pl.when(i + 1 < n)
    def _():
        fetch(i + 1, (i + 1) % 2)

    slot = i % 2
    pltpu.make_async_copy(x_ref.at[pl.ds(0, tm), :], buf.at[slot], sem.at[slot]).wait()
    o_ref[...] = buf[slot].astype(o_ref.dtype)


def _moe_dispatch(x, plan, *, tm=256):
    s_dim, d_dim = x.shape
    tm = min(tm, s_dim)
    n_rows = plan["n_rows"]
    return pl.pallas_call(
        _dispatch_kernel,
        grid_spec=pltpu.PrefetchScalarGridSpec(
            num_scalar_prefetch=1,
            grid=(n_rows // tm,),
            in_specs=[pl.BlockSpec(memory_space=pl.ANY)],
            out_specs=pl.BlockSpec((tm, d_dim), lambda i, st: (i, 0)),
            scratch_shapes=[pltpu.VMEM((2, tm, d_dim), F32), pltpu.SemaphoreType.DMA((2,))],
        ),
        out_shape=jax.ShapeDtypeStruct((n_rows, d_dim), BF16),
        compiler_params=_cparams("arbitrary"),
        name="moe_dispatch",
    )(plan["src_tok"], x)


def _moe_up_kernel(te_ref, nu_ref, a_ref, wg_ref, wu_ref, o_ref):
    m = pl.program_id(1)

    @pl.when(m < nu_ref[0])
    def _():
        a = a_ref[...]
        g = _dot(a, wg_ref[...].astype(BF16))
        u = _dot(a, wu_ref[...].astype(BF16))
        o_ref[...] = (g * _sigmoid(g) * u).astype(o_ref.dtype)

    @pl.when(m >= nu_ref[0])
    def _():
        o_ref[...] = jnp.zeros_like(o_ref)


def _moe_up(xs, wg, wu, layer, plan, *, tn=256):
    n_rows, d_dim = xs.shape
    tm = plan["tm"]
    w_spec = pl.BlockSpec((None, None, d_dim, tn), lambda n, m, te, nu: (layer, te[m], 0, n))
    return pl.pallas_call(
        _moe_up_kernel,
        grid_spec=pltpu.PrefetchScalarGridSpec(
            num_scalar_prefetch=2,
            grid=(D_FF_EXPERT // tn, n_rows // tm),
            in_specs=[pl.BlockSpec((tm, d_dim), lambda n, m, te, nu: (m, 0)), w_spec, w_spec],
            out_specs=pl.BlockSpec((tm, tn), lambda n, m, te, nu: (m, n)),
        ),
        out_shape=jax.ShapeDtypeStruct((n_rows, D_FF_EXPERT), BF16),
        compiler_params=_cparams("parallel", "arbitrary"),
        name="moe_up",
    )(plan["tile_expert"], plan["n_used"], xs, wg, wu)


def _moe_down_kernel(te_ref, nu_ref, a_ref, w_ref, o_ref):
    m = pl.program_id(1)

    @pl.when(m < nu_ref[0])
    def _():
        o_ref[...] = _dot(a_ref[...], w_ref[...].astype(BF16))

    @pl.when(m >= nu_ref[0])
    def _():
        o_ref[...] = jnp.zeros_like(o_ref)


def _moe_down(hid, wd, layer, plan, *, tn=1024):
    n_rows, f_dim = hid.shape
    tm = plan["tm"]
    d_dim = wd.shape[-1]
    return pl.pallas_call(
        _moe_down_kernel,
        grid_spec=pltpu.PrefetchScalarGridSpec(
            num_scalar_prefetch=2,
            grid=(d_dim // tn, n_rows // tm),
            in_specs=[pl.BlockSpec((tm, f_dim), lambda n, m, te, nu: (m, 0)),
                      pl.BlockSpec((None, None, f_dim, tn), lambda n, m, te, nu: (layer, te[m], 0, n))],
            out_specs=pl.BlockSpec((tm, tn), lambda n, m, te, nu: (m, n)),
        ),
        out_shape=jax.ShapeDtypeStruct((n_rows, d_dim), F32),
        compiler_params=_cparams("parallel", "arbitrary"),
        name="moe_down",
    )(plan["tile_expert"], plan["n_used"], hid, wd)


def _combine_ln_kernel(dest_ref, x_ref, r_ref, ys_ref, g_ref, b_ref, o_ref, ob_ref, buf, sem, *, s_dim):
    i = pl.program_id(0)
    n = pl.num_programs(0)
    tb = x_ref.shape[0]

    def fetch(blk, slot):
        def issue(r, _):
            t = blk * tb + r
            _row_copy(ys_ref, dest_ref[t], buf.at[slot, 0], r, sem.at[slot]).start()
            _row_copy(ys_ref, dest_ref[s_dim + t], buf.at[slot, 1], r, sem.at[slot]).start()
            return 0
        lax.fori_loop(0, tb, issue, 0)

    @pl.when(i == 0)
    def _():
        fetch(0, 0)

    @pl.when(i + 1 < n)
    def _():
        fetch(i + 1, (i + 1) % 2)

    slot = i % 2
    for k in range(2):
        pltpu.make_async_copy(ys_ref.at[pl.ds(0, tb), :], buf.at[slot, k], sem.at[slot]).wait()
    rec = r_ref[...]
    f = rec[:, ROUTE_G1:ROUTE_G1 + 1] * buf[slot, 0] + rec[:, ROUTE_G2:ROUTE_G2 + 1] * buf[slot, 1]
    y = DEEPNORM_ALPHA * x_ref[...] + f
    mu = jnp.mean(y, axis=-1, keepdims=True)
    d = y - mu
    var = jnp.mean(d * d, axis=-1, keepdims=True)
    out = d * lax.rsqrt(var + LN_EPS) * g_ref[...] + b_ref[...]
    o_ref[...] = out
    ob_ref[...] = out.astype(BF16)


def _moe_combine_ln(x, route, ys, plan, g3, b3, idx, *, tb=256):
    s_dim, d_dim = x.shape
    tb = min(tb, s_dim)
    row = pl.BlockSpec((tb, d_dim), lambda i, de: (i, 0))
    par = pl.BlockSpec((None, 1, d_dim), lambda i, de: (idx, 0, 0))
    return pl.pallas_call(
        functools.partial(_combine_ln_kernel, s_dim=s_dim),
        grid_spec=pltpu.PrefetchScalarGridSpec(
            num_scalar_prefetch=1,
            grid=(s_dim // tb,),
            in_specs=[row, pl.BlockSpec((tb, LANES), lambda i, de: (i, 0)), pl.BlockSpec(memory_space=pl.ANY),
                      par, par],
            out_specs=[row, row],
            scratch_shapes=[pltpu.VMEM((2, 2, tb, d_dim), F32), pltpu.SemaphoreType.DMA((2,))],
        ),
        out_shape=[jax.ShapeDtypeStruct((s_dim, d_dim), F32), jax.ShapeDtypeStruct((s_dim, d_dim), BF16)],
        compiler_params=_cparams("arbitrary"),
        name="moe_combine_ln",
    )(plan["dest"], x, route, ys, g3, b3)


PROJ_TILES = dict(tm=2048, tn=512)


def _even_layer(x, xb, j, i, even_w_in, even_w_out, ffn_w_gate, ffn_w_up, ffn_w_down, ln_g3, ln_b3, tabs):
    d = D_MODEL
    p = _matmul_rows([xb], even_w_in, lambda n: (j, 0, n), EVEN_IN, name="even_in_proj", **PROJ_TILES)
    a = _stick_breaking(p)
    r = _retention(p, tabs["log_gamma"], tabs["ret_cos"], tabs["ret_sin"])
    h = _matmul_rows([a, r], even_w_out, lambda n: (j, 0, n), d, name="even_out_proj", **PROJ_TILES)
    x, xb = _deepnorm_ln(x, h, ln_g3, ln_b3, 2 * i)
    hid = _swiglu_rows(xb, ffn_w_gate, ffn_w_up, lambda n: (j, 0, n), D_FF, tm=2048, tn=256, name="ffn_up")
    f = _matmul_rows([hid], ffn_w_down, lambda n: (j, 0, n), d, tm=1024, tn=256, name="ffn_down")
    return _deepnorm_ln(x, f, ln_g3, ln_b3, 2 * i + 1)


def _odd_layer(x, xb, j, i, odd_w_in, odd_w_out, moe_w_router, moe_w_gate, moe_w_up, moe_w_down,
               ln_g3, ln_b3, tabs):
    d = D_MODEL
    s_dim = x.shape[0]
    w_nk = jnp.swapaxes(odd_w_in, 1, 2)
    p = _matmul_rows([xb], w_nk, lambda n: (j, n, 0), ODD_MAIN, name="odd_in_proj", w_is_nk=True, **PROJ_TILES)
    w_tail = jnp.pad(w_nk[j, ODD_MAIN:, :], ((0, LANES - ODD_TAIL), (0, 0)))
    tail = _matmul_rows([xb], w_tail, lambda n: (n, 0), LANES, tm=2048, tn=LANES, name="odd_in_tail",
                        w_is_nk=True)

    cos_h, sin_h = tabs["head_cos"], tabs["head_sin"]
    q = _rope_cast(p, 0, DSA_Q_W, cos_h, sin_h, half=HEAD_DIM // ROT_FRACTION // 2,
                   scale=HEAD_DIM ** -0.5 * LOG2_E)
    k = _rope_cast(p, DSA_Q_W, DSA_KV_W, cos_h, sin_h, half=HEAD_DIM // ROT_FRACTION // 2, scale=1.0)
    v = _cast(p, DSA_Q_W + DSA_KV_W, DSA_KV_W)
    qi = _idx_q_prep(p, DSA_Q_W + 2 * DSA_KV_W, tabs["idx_cos"], tabs["idx_sin"])
    ki, wi = _idx_tail_prep(tail, tabs["tail_cos"], tabs["tail_sin"])

    bias = _dsa_select(qi, ki, wi, min(IDX_TOPK_MAX, s_dim // 4))
    y = _dsa_attention(q, k, v, bias)
    h = _matmul_rows([y], odd_w_out, lambda n: (j, 0, n), d, name="odd_out_proj", **PROJ_TILES)
    x, xb = _deepnorm_ln(x, h, ln_g3, ln_b3, 2 * i)

    w_router = jnp.pad(moe_w_router[j], ((0, 0), (0, LANES - N_EXPERTS)))
    route, counts = _router(x, w_router)
    plan = _moe_plan(route, counts, s_dim)
    xs = _moe_dispatch(x, plan)
    hid = _moe_up(xs, moe_w_gate, moe_w_up, j, plan)
    ys = _moe_down(hid, moe_w_down, j, plan)
    return _moe_combine_ln(x, route, ys, plan, ln_g3, ln_b3, 2 * i + 1)


def kernel(x, even_w_in, even_w_out, odd_w_in, odd_w_out, ffn_w_gate, ffn_w_up, ffn_w_down, moe_w_router,
           moe_w_gate, moe_w_up, moe_w_down, ln_g, ln_b):
    batch, s_dim, d = x.shape
    ln_g3 = ln_g.reshape(2 * DEPTH, 1, d)
    ln_b3 = ln_b.reshape(2 * DEPTH, 1, d)
    ret_cos, ret_sin = _rope_tables(s_dim, LANES, RET_QK_DIM, RET_THETA)
    head_cos, head_sin = _rope_tables(s_dim, LANES, HEAD_DIM // ROT_FRACTION, ROPE_THETA)
    idx_cos, idx_sin = _rope_tables(s_dim, IDX_DIM, IDX_DIM // ROT_FRACTION, ROPE_THETA)
    tail_cos, tail_sin = _rope_tables(s_dim, IDX_DIM, IDX_DIM // ROT_FRACTION, ROPE_THETA, active=IDX_DIM)
    tabs = dict(
        log_gamma=jnp.log1p(-jnp.exp2(-5.0 - jnp.arange(RET_HEADS, dtype=F32))),
        ret_cos=ret_cos, ret_sin=ret_sin, head_cos=head_cos, head_sin=head_sin,
        idx_cos=idx_cos, idx_sin=idx_sin, tail_cos=tail_cos, tail_sin=tail_sin,
    )
    outs = []
    for b in range(batch):
        xs = x[b] if batch > 1 else x.reshape(s_dim, d)
        xb = xs.astype(BF16)
        for i in range(DEPTH):
            j = i // 2
            if i % 2 == 0:
                xs, xb = _even_layer(xs, xb, j, i, even_w_in, even_w_out, ffn_w_gate, ffn_w_up, ffn_w_down,
                                     ln_g3, ln_b3, tabs)
            else:
                xs, xb = _odd_layer(xs, xb, j, i, odd_w_in, odd_w_out, moe_w_router, moe_w_gate, moe_w_up,
                                    moe_w_down, ln_g3, ln_b3, tabs)
        outs.append(xs)
    return jnp.stack(outs, axis=0) if batch > 1 else outs[0].reshape(1, s_dim, d)
```

```python
import functools
import math

import jax
import jax.numpy as jnp
from jax import lax
from jax.experimental import pallas as pl
from jax.experimental.pallas import tpu as pltpu

F32 = jnp.float32
BF16 = jnp.bfloat16
I32 = jnp.int32

D_MODEL = 4096
DEPTH = 4
HEAD_DIM = 128
SB_HEADS = 16
SB_W = SB_HEADS * HEAD_DIM
RET_HEADS = 8
RET_QK_DIM = 128
RET_V_DIM = 256
RET_QK_W = RET_HEADS * RET_QK_DIM
RET_V_W = RET_HEADS * RET_V_DIM
RET_THETA = 10000.0
DSA_Q_HEADS = 32
DSA_KV_HEADS = 8
DSA_GROUP = DSA_Q_HEADS // DSA_KV_HEADS
DSA_Q_W = DSA_Q_HEADS * HEAD_DIM
DSA_KV_W = DSA_KV_HEADS * HEAD_DIM
IDX_HEADS = 16
IDX_DIM = 64
IDX_Q_W = IDX_HEADS * IDX_DIM
IDX_TOPK_MAX = 256
ROPE_THETA = 500000.0
ROT_FRACTION = 4
D_FF = 11008
N_EXPERTS = 8
D_FF_EXPERT = 1792
LN_EPS = 1e-5
DEEPNORM_ALPHA = (2.0 * DEPTH) ** 0.25

EVEN_IN = 3 * SB_W + 2 * RET_QK_W + 2 * RET_V_W
ODD_MAIN = DSA_Q_W + 2 * DSA_KV_W + IDX_Q_W
ODD_TAIL = IDX_DIM + IDX_HEADS

LANES = 128
V7X_VMEM_BYTES = 64 * 1024 * 1024
VMEM_LIMIT_BYTES = V7X_VMEM_BYTES * 7 // 8

INT_MIN = -2147483648
MASK_NEG = -(2.0 ** 100)
LOG2_E = math.log2(math.e)


def _cparams(*sem):
    return pltpu.CompilerParams(dimension_semantics=sem, vmem_limit_bytes=VMEM_LIMIT_BYTES)


def _sigmoid(x):
    return 1.0 / (1.0 + jnp.exp(-x))


def _dot(a, b):
    return jnp.dot(a, b, preferred_element_type=F32)


def _dot_nt(a, b):
    return lax.dot_general(a, b, (((1,), (1,)), ((), ())), preferred_element_type=F32)


def _mm_rows_kernel(*refs, w_is_nk):
    *a_refs, w_ref, o_ref = refs
    k0, acc = 0, None
    for a_ref in a_refs:
        kw = a_ref.shape[1]
        if w_is_nk:
            part = _dot_nt(a_ref[...], w_ref[:, k0:k0 + kw].astype(BF16))
        else:
            part = _dot(a_ref[...], w_ref[k0:k0 + kw, :].astype(BF16))
        acc = part if acc is None else acc + part
        k0 += kw
    o_ref[...] = acc


ROW_PANEL_MAX_ROWS = 2048
ROW_PANEL_VMEM_SHARE = 0.4
ROW_STEP_VMEM_SHARE = 0.85


def _row_panel_tiles(m_dim, k_dim, n_cols, n_weights, out_bytes):
    tm = min(ROW_PANEL_MAX_ROWS, m_dim)
    while tm * k_dim * 2 > ROW_PANEL_VMEM_SHARE * VMEM_LIMIT_BYTES and tm % 2 == 0 and m_dim % (tm // 2) == 0:
        tm //= 2
    for tn in (4 * LANES, 2 * LANES, LANES):
        step = tm * k_dim * 2 + 2 * (n_weights * k_dim * tn * 4 + tm * tn * out_bytes)
        if n_cols % tn == 0 and step <= ROW_STEP_VMEM_SHARE * VMEM_LIMIT_BYTES:
            return tm, tn
    raise ValueError(f"no row-panel tiling for K={k_dim}, N={n_cols}")


def _matmul_rows(panels, w, w_index, n_cols, *, name, w_is_nk=False):
    m_dim = panels[0].shape[0]
    k_dim = sum(a.shape[1] for a in panels)
    tm, tn = _row_panel_tiles(m_dim, k_dim, n_cols, 1, 4)
    n_lead = w.ndim - 2
    w_block = (tn, k_dim) if w_is_nk else (k_dim, tn)
    return pl.pallas_call(
        functools.partial(_mm_rows_kernel, w_is_nk=w_is_nk),
        grid=(m_dim // tm, n_cols // tn),
        in_specs=[pl.BlockSpec((tm, a.shape[1]), lambda m, n: (m, 0), pipeline_mode=pl.Buffered(1))
                  for a in panels]
        + [pl.BlockSpec((None,) * n_lead + w_block, lambda m, n: w_index(n))],
        out_specs=pl.BlockSpec((tm, tn), lambda m, n: (m, n)),
        out_shape=jax.ShapeDtypeStruct((m_dim, n_cols), F32),
        compiler_params=_cparams("parallel", "arbitrary"),
        name=name,
    )(*panels, w)


def _swiglu_rows_kernel(a_ref, wg_ref, wu_ref, o_ref):
    a = a_ref[...]
    g = _dot(a, wg_ref[...].astype(BF16))
    u = _dot(a, wu_ref[...].astype(BF16))
    o_ref[...] = (g * _sigmoid(g) * u).astype(o_ref.dtype)


def _swiglu_rows(a, wg, wu, w_index, n_cols, *, name):
    m_dim, k_dim = a.shape
    tm, tn = _row_panel_tiles(m_dim, k_dim, n_cols, 2, 2)
    n_lead = wg.ndim - 2
    w_spec = pl.BlockSpec((None,) * n_lead + (k_dim, tn), lambda m, n: w_index(n))
    return pl.pallas_call(
        _swiglu_rows_kernel,
        grid=(m_dim // tm, n_cols // tn),
        in_specs=[pl.BlockSpec((tm, k_dim), lambda m, n: (m, 0), pipeline_mode=pl.Buffered(1)), w_spec, w_spec],
        out_specs=pl.BlockSpec((tm, tn), lambda m, n: (m, n)),
        out_shape=jax.ShapeDtypeStruct((m_dim, n_cols), BF16),
        compiler_params=_cparams("parallel", "arbitrary"),
        name=name,
    )(a, wg, wu)


def _ln_kernel(x_ref, h_ref, g_ref, b_ref, o_ref, ob_ref):
    y = DEEPNORM_ALPHA * x_ref[...] + h_ref[...]
    mu = jnp.mean(y, axis=-1, keepdims=True)
    d = y - mu
    var = jnp.mean(d * d, axis=-1, keepdims=True)
    out = d * lax.rsqrt(var + LN_EPS) * g_ref[...] + b_ref[...]
    o_ref[...] = out
    ob_ref[...] = out.astype(BF16)


def _deepnorm_ln(x, h, g3, b3, idx, *, tm=256):
    s_dim, d_dim = x.shape
    tm = min(tm, s_dim)
    row = pl.BlockSpec((tm, d_dim), lambda i: (i, 0))
    par = pl.BlockSpec((None, 1, d_dim), lambda i: (idx, 0, 0))
    return pl.pallas_call(
        _ln_kernel,
        grid=(s_dim // tm,),
        in_specs=[row, row, par, par],
        out_specs=[row, row],
        out_shape=[jax.ShapeDtypeStruct((s_dim, d_dim), F32), jax.ShapeDtypeStruct((s_dim, d_dim), BF16)],
        compiler_params=_cparams("parallel"),
        name="deepnorm_ln",
    )(x, h, g3, b3)


def _rope_tables(s_dim, period, rot_dim, theta, active=LANES):
    half = rot_dim // 2
    pos = jnp.arange(s_dim, dtype=F32)
    freqs = jnp.exp(-math.log(theta) * jnp.arange(half, dtype=F32) * (2.0 / rot_dim))
    ang = pos[:, None] * freqs[None, :]
    cos, sin = jnp.cos(ang), jnp.sin(ang)
    ones = jnp.ones((s_dim, period - rot_dim), F32)
    zeros = jnp.zeros((s_dim, period - rot_dim), F32)
    cos_p = jnp.concatenate([cos, cos, ones], axis=1)
    sin_p = jnp.concatenate([-sin, sin, zeros], axis=1)
    reps = LANES // period
    cos_t, sin_t = jnp.tile(cos_p, (1, reps)), jnp.tile(sin_p, (1, reps))
    if active < LANES:
        lane = jnp.arange(LANES)[None, :]
        cos_t = jnp.where(lane < active, cos_t, 1.0)
        sin_t = jnp.where(lane < active, sin_t, 0.0)
    return cos_t, sin_t


def _rope_tile(x, cos, sin, half, period=LANES):
    if 2 * half == LANES:
        return x * cos + pltpu.roll(x, half, 1) * sin
    lane = lax.broadcasted_iota(I32, x.shape, 1)
    first = (lane & (period - 1)) < half
    partner = jnp.where(first, pltpu.roll(x, LANES - half, 1), pltpu.roll(x, half, 1))
    return x * cos + partner * sin


SB_BLOCK = 256
SB_GROUP = 2
SB_DEAD = 160.0


def _sb_kernel(q_ref, k_ref, v_ref, o_ref, kb_ref, vb_ref):
    i = pl.program_id(1)
    blk = SB_BLOCK
    span = SB_GROUP * blk

    @pl.when(i == 0)
    def _():
        kb_ref[...] = k_ref[...].astype(BF16)
        vb_ref[...] = v_ref[...].astype(BF16)

    q = (q_ref[...] * (HEAD_DIM ** -0.5 * LOG2_E)).astype(BF16)
    row = lax.broadcasted_iota(I32, (blk, blk), 0)
    col = lax.broadcasted_iota(I32, (blk, blk), 1)
    later = jnp.where(row > col, 1.0, 0.0).astype(BF16)

    def group(start, acc, run, key_end):
        z = _dot_nt(q, kb_ref[pl.ds(start, span), :])
        soft = jnp.log(1.0 + jnp.exp2(-jnp.abs(z))) * LOG2_E
        sp = jnp.maximum(z, 0.0) + soft
        log_beta = z - sp
        if key_end is not None:
            key_pos = start + lax.broadcasted_iota(I32, (blk, span), 1)
            q_pos = i * blk + lax.broadcasted_iota(I32, (blk, span), 0)
            causal = (key_pos < q_pos) & (key_pos < key_end)
            sp = jnp.where(causal, sp, 0.0)
        parts = [None] * SB_GROUP
        for c in reversed(range(SB_GROUP)):
            sp_c = sp[:, c * blk:(c + 1) * blk]
            parts[c] = _dot(sp_c.astype(BF16), later) + run
            run = run + jnp.sum(sp_c, axis=1, keepdims=True)
        w = jnp.exp2(log_beta - jnp.concatenate(parts, axis=1))
        if key_end is not None:
            w = jnp.where(causal, w, 0.0)
        acc = acc + _dot(w.astype(BF16), vb_ref[pl.ds(start, span), :])
        return acc, run

    below = jnp.maximum(i - (SB_GROUP - 1), 0)
    n_full = below // SB_GROUP
    n_left = below - n_full * SB_GROUP
    acc0 = jnp.zeros((blk, HEAD_DIM), F32)
    run0 = jnp.zeros((blk, 1), F32)
    acc, run = group(pl.multiple_of(below * blk, blk), acc0, run0, (i + 1) * blk)

    def live(run):
        return jnp.min(run) < SB_DEAD

    def cond(carry):
        return jnp.logical_and(carry[0] < n_full, live(carry[2]))

    def body(carry):
        t, acc, run = carry
        start = pl.multiple_of((below - SB_GROUP * (t + 1)) * blk, blk)
        acc, run = group(start, acc, run, None)
        return t + 1, acc, run

    t, acc, run = lax.while_loop(cond, body, (jnp.int32(0), acc, run))
    tail = jnp.logical_and(jnp.logical_and(t == n_full, n_left > 0), live(run))
    acc, run = lax.cond(tail, lambda: group(0, acc, run, n_left * blk), lambda: (acc, run))
    o_ref[...] = acc.astype(o_ref.dtype)


def _stick_breaking(p):
    s_dim = p.shape[0]
    blk = SB_BLOCK
    kv = lambda off: pl.BlockSpec((s_dim, HEAD_DIM), lambda h, i: (0, off + h))
    return pl.pallas_call(
        _sb_kernel,
        grid=(SB_HEADS, s_dim // blk),
        in_specs=[pl.BlockSpec((blk, HEAD_DIM), lambda h, i: (i, h)), kv(SB_HEADS), kv(2 * SB_HEADS)],
        out_specs=pl.BlockSpec((blk, HEAD_DIM), lambda h, i: (i, h)),
        out_shape=jax.ShapeDtypeStruct((s_dim, SB_W), BF16),
        scratch_shapes=[pltpu.VMEM((s_dim, HEAD_DIM), BF16), pltpu.VMEM((s_dim, HEAD_DIM), BF16)],
        compiler_params=_cparams("parallel", "arbitrary"),
        name="stick_breaking",
    )(p, p, p)


RET_CHUNK = 512


def _ret_kernel(lg_ref, q_ref, k_ref, v_ref, g_ref, cos_ref, sin_ref, o_ref, state):
    h = pl.program_id(0)
    c = pl.program_id(1)
    n = RET_CHUNK

    @pl.when(c == 0)
    def _():
        state[...] = jnp.zeros_like(state)

    lg = lg_ref[h]
    cos, sin = cos_ref[...], sin_ref[...]
    q = _rope_tile(q_ref[...], cos, sin, RET_QK_DIM // 2)
    k = _rope_tile(k_ref[...], cos, sin, RET_QK_DIM // 2) * (RET_QK_DIM ** -0.5)
    v = v_ref[...].astype(BF16)

    ii = lax.broadcasted_iota(I32, (n, n), 0)
    jj = lax.broadcasted_iota(I32, (n, n), 1)
    rel = (ii - jj).astype(F32)
    decay = jnp.where(rel >= 0.0, jnp.exp(lg * jnp.maximum(rel, 0.0)), 0.0)
    inner = _dot_nt(q.astype(BF16), k.astype(BF16)) * decay
    out = _dot(inner.astype(BF16), v)

    pos = lax.broadcasted_iota(I32, (n, 1), 0).astype(F32)
    q_decay = jnp.exp(lg * (pos + 1.0))
    k_decay = jnp.exp(lg * (n - 1.0 - pos))
    prev = state[...]
    out = out + _dot((q * q_decay).astype(BF16), prev.astype(BF16))
    kd_t = jnp.transpose(k * k_decay).astype(BF16)
    state[...] = jnp.exp(lg * jnp.full((1, 1), n, F32)) * prev + _dot(kd_t, v)

    mu = jnp.mean(out, axis=-1, keepdims=True)
    d = out - mu
    var = jnp.mean(d * d, axis=-1, keepdims=True)
    g = g_ref[...]
    o_ref[...] = (g * _sigmoid(g) * (d * lax.rsqrt(var + LN_EPS))).astype(o_ref.dtype)


def _retention(p, log_gamma, cos, sin):
    s_dim = p.shape[0]
    n = RET_CHUNK
    qk = lambda off: pl.BlockSpec((n, RET_QK_DIM), lambda h, c, lg: (c, off + h))
    vg = lambda off: pl.BlockSpec((n, RET_V_DIM), lambda h, c, lg: (c, off + h))
    tab = pl.BlockSpec((n, LANES), lambda h, c, lg: (c, 0))
    q_off = 3 * SB_W // RET_QK_DIM
    v_off = (3 * SB_W + 2 * RET_QK_W) // RET_V_DIM
    grid_spec = pltpu.PrefetchScalarGridSpec(
        num_scalar_prefetch=1,
        grid=(RET_HEADS, s_dim // n),
        in_specs=[qk(q_off), qk(q_off + RET_HEADS), vg(v_off), vg(v_off + RET_HEADS), tab, tab],
        out_specs=pl.BlockSpec((n, RET_V_DIM), lambda h, c, lg: (c, h)),
        scratch_shapes=[pltpu.VMEM((RET_QK_DIM, RET_V_DIM), F32)],
    )
    return pl.pallas_call(
        _ret_kernel,
        grid_spec=grid_spec,
        out_shape=jax.ShapeDtypeStruct((s_dim, RET_V_W), BF16),
        compiler_params=_cparams("parallel", "arbitrary"),
        name="retention",
    )(log_gamma, p, p, p, p, cos, sin)


def _rope_cast_kernel(x_ref, cos_ref, sin_ref, o_ref, *, half, scale):
    cos, sin = cos_ref[...], sin_ref[...]
    for t in range(x_ref.shape[1] // LANES):
        sl = slice(t * LANES, (t + 1) * LANES)
        o_ref[:, sl] = (_rope_tile(x_ref[:, sl], cos, sin, half) * scale).astype(o_ref.dtype)


def _rope_cast(p, col0, n_cols, cos, sin, *, half, scale, tm=512, tn=1024):
    s_dim = p.shape[0]
    tm = min(tm, s_dim)
    tn = min(tn, n_cols)
    tab = pl.BlockSpec((tm, LANES), lambda i, j: (i, 0))
    return pl.pallas_call(
        functools.partial(_rope_cast_kernel, half=half, scale=scale),
        grid=(s_dim // tm, n_cols // tn),
        in_specs=[pl.BlockSpec((tm, tn), lambda i, j: (i, col0 // tn + j)), tab, tab],
        out_specs=pl.BlockSpec((tm, tn), lambda i, j: (i, j)),
        out_shape=jax.ShapeDtypeStruct((s_dim, n_cols), BF16),
        compiler_params=_cparams("parallel", "parallel"),
        name="rope_cast",
    )(p, cos, sin)


def _cast_kernel(x_ref, o_ref):
    o_ref[...] = x_ref[...].astype(o_ref.dtype)


def _cast(p, col0, n_cols, *, tm=512, tn=1024):
    s_dim = p.shape[0]
    tm = min(tm, s_dim)
    return pl.pallas_call(
        _cast_kernel,
        grid=(s_dim // tm, n_cols // tn),
        in_specs=[pl.BlockSpec((tm, tn), lambda i, j: (i, col0 // tn + j))],
        out_specs=pl.BlockSpec((tm, tn), lambda i, j: (i, j)),
        out_shape=jax.ShapeDtypeStruct((s_dim, n_cols), BF16),
        compiler_params=_cparams("parallel", "parallel"),
        name="cast_bf16",
    )(p)


def _idx_q_kernel(x_ref, cos_ref, sin_ref, o_ref):
    cos, sin = cos_ref[...], sin_ref[...]
    lane = lax.broadcasted_iota(I32, cos.shape, 1)
    low = lane < IDX_DIM
    for t in range(x_ref.shape[1] // LANES):
        y = _rope_tile(x_ref[:, t * LANES:(t + 1) * LANES], cos, sin, IDX_DIM // ROT_FRACTION // 2, IDX_DIM)
        o_ref[:, (2 * t) * LANES:(2 * t + 1) * LANES] = jnp.where(low, y, 0.0).astype(o_ref.dtype)
        o_ref[:, (2 * t + 1) * LANES:(2 * t + 2) * LANES] = jnp.where(
            low, pltpu.roll(y, IDX_DIM, 1), 0.0).astype(o_ref.dtype)


def _idx_q_prep(p, col0, cos, sin, *, tm=512):
    s_dim = p.shape[0]
    tm = min(tm, s_dim)
    tab = pl.BlockSpec((tm, LANES), lambda i: (i, 0))
    return pl.pallas_call(
        _idx_q_kernel,
        grid=(s_dim // tm,),
        in_specs=[pl.BlockSpec((tm, IDX_Q_W), lambda i: (i, col0 // IDX_Q_W)), tab, tab],
        out_specs=pl.BlockSpec((tm, IDX_HEADS * LANES), lambda i: (i, 0)),
        out_shape=jax.ShapeDtypeStruct((s_dim, IDX_HEADS * LANES), BF16),
        compiler_params=_cparams("parallel"),
        name="idx_q_prep",
    )(p, cos, sin)


def _idx_tail_kernel(x_ref, cos_ref, sin_ref, k_ref, w_ref):
    x = x_ref[...]
    lane = lax.broadcasted_iota(I32, x.shape, 1)
    y = _rope_tile(x, cos_ref[...], sin_ref[...], IDX_DIM // ROT_FRACTION // 2, IDX_DIM) * (IDX_DIM ** -0.5)
    k_ref[...] = jnp.where(lane < IDX_DIM, y, 0.0).astype(k_ref.dtype)
    w_ref[...] = x * (IDX_HEADS ** -0.5)


def _idx_tail_prep(tail, cos, sin, *, tm=512):
    s_dim = tail.shape[0]
    tm = min(tm, s_dim)
    blk = pl.BlockSpec((tm, LANES), lambda i: (i, 0))
    return pl.pallas_call(
        _idx_tail_kernel,
        grid=(s_dim // tm,),
        in_specs=[blk, blk, blk],
        out_specs=[blk, blk],
        out_shape=[jax.ShapeDtypeStruct((s_dim, LANES), BF16), jax.ShapeDtypeStruct((s_dim, LANES), F32)],
        compiler_params=_cparams("parallel"),
        name="idx_tail_prep",
    )(tail, cos, sin)


DSA_QB = 128
DSA_KB = 512
DSA_AQB = 512


def _select_kernel(qi_ref, ki_ref, w_ref, o_ref, keys_ref, *, topk, n_kblocks, idx_bits):
    i = pl.program_id(0)
    qb, kb = DSA_QB, DSA_KB
    n_live = ((i + 1) * qb + kb - 1) // kb
    q_pos = i * qb + lax.broadcasted_iota(I32, (qb, kb), 0)
    col0 = lax.broadcasted_iota(I32, (qb, kb), 1)
    w = w_ref[...]

    def score_block(jb, _):
        start = pl.multiple_of(jb * kb, kb)
        kj = ki_ref[pl.ds(start, kb), :]
        acc = jnp.zeros((qb, kb), F32)
        for h in range(IDX_HEADS):
            z = _dot_nt(qi_ref[:, h * LANES:(h + 1) * LANES], kj)
            acc = acc + jnp.maximum(z, 0.0) * w[:, IDX_DIM + h:IDX_DIM + h + 1]
        acc = jnp.where(acc == 0.0, 0.0, acc)
        bits = lax.bitcast_convert_type(acc, I32)
        key = jnp.where(bits >= 0, bits, bits ^ 0x7FFFFFFF)
        keys_ref[jb] = jnp.where(jb * kb + col0 <= q_pos, key, INT_MIN)
        return 0

    lax.fori_loop(0, n_live, score_block, 0)

    def count(pred):
        def body(jb, part):
            kk = keys_ref[jb]
            hit = jnp.where(pred(kk, jb * kb + col0), 1.0, 0.0)
            for t in range(kb // LANES):
                part = part + hit[:, t * LANES:(t + 1) * LANES]
            return part
        part = lax.fori_loop(0, n_live, body, jnp.zeros((qb, LANES), F32))
        return jnp.sum(part, axis=1, keepdims=True)

    k_f = float(topk)
    c_nonneg = count(lambda kk, cc: kk >= 0)
    thr = jnp.where(c_nonneg >= k_f, jnp.int32(0), jnp.int32(INT_MIN))

    def thr_bit(b, thr):
        cand = thr + jnp.left_shift(jnp.int32(1), 30 - b)
        c = count(lambda kk, cc: kk >= cand)
        return jnp.where(c >= k_f, cand, thr)

    thr = lax.fori_loop(0, 31, thr_bit, thr)

    need = k_f - count(lambda kk, cc: kk > thr)
    n_eq = count(lambda kk, cc: kk == thr)
    tied = (n_eq > need) & (thr != INT_MIN)

    def cut_bit(b, cut):
        cand = cut + jnp.left_shift(jnp.int32(1), idx_bits - 1 - b)
        c = count(lambda kk, cc: (kk == thr) & (cc < cand))
        return jnp.where(c < need, cand, cut)

    def cut_search():
        return lax.fori_loop(0, idx_bits, cut_bit, jnp.zeros((qb, 1), I32))

    def no_cut():
        return jnp.full((qb, 1), n_kblocks * kb, I32)

    cut = lax.cond(jnp.max(jnp.where(tied, 1.0, 0.0)) > 0.0, cut_search, no_cut)

    def write_live(jb, _):
        kk = keys_ref[jb]
        cc = jb * kb + col0
        sel = ((kk > thr) | ((kk == thr) & (cc <= cut))) & (kk != INT_MIN)
        o_ref[jb] = jnp.where(sel, 0.0, MASK_NEG).astype(o_ref.dtype)
        return 0

    lax.fori_loop(0, n_live, write_live, 0)

    def write_dead(jb, _):
        o_ref[jb] = jnp.full((qb, kb), MASK_NEG, o_ref.dtype)
        return 0

    lax.fori_loop(n_live, n_kblocks, write_dead, 0)


def _dsa_select(qi, ki, w, topk):
    s_dim = qi.shape[0]
    qb, kb = DSA_QB, DSA_KB
    n_kblocks = s_dim // kb
    idx_bits = max(1, (s_dim - 1).bit_length())
    return pl.pallas_call(
        functools.partial(_select_kernel, topk=topk, n_kblocks=n_kblocks, idx_bits=idx_bits),
        grid=(s_dim // qb,),
        in_specs=[
            pl.BlockSpec((qb, IDX_HEADS * LANES), lambda i: (i, 0)),
            pl.BlockSpec((s_dim, LANES), lambda i: (0, 0)),
            pl.BlockSpec((qb, LANES), lambda i: (i, 0)),
        ],
        out_specs=pl.BlockSpec((None, n_kblocks, qb, kb), lambda i: (i, 0, 0, 0)),
        out_shape=jax.ShapeDtypeStruct((s_dim // qb, n_kblocks, qb, kb), BF16),
        scratch_shapes=[pltpu.VMEM((n_kblocks, qb, kb), I32)],
        compiler_params=_cparams("parallel"),
        name="dsa_select",
    )(qi, ki, w)


def _dsa_attn_kernel(q_ref, k_ref, v_ref, b_ref, o_ref):
    i = pl.program_id(1)
    qb, kb, grp = DSA_AQB, DSA_KB, DSA_GROUP
    n_sel = qb // DSA_QB
    n_live = ((i + 1) * qb + kb - 1) // kb
    q = jnp.concatenate([q_ref[:, r * HEAD_DIM:(r + 1) * HEAD_DIM] for r in range(grp)], axis=0)

    def body(jb, carry):
        m, l, acc = carry
        start = pl.multiple_of(jb * kb, kb)
        kj = k_ref[pl.ds(start, kb), :]
        vj = v_ref[pl.ds(start, kb), :]
        z = _dot_nt(q, kj).astype(BF16)
        bias = b_ref[:, jb].reshape(1, qb, kb)
        z = (z.reshape(grp, qb, kb) + bias).reshape(grp * qb, kb)
        m_new = jnp.maximum(m, jnp.max(z, axis=1, keepdims=True).astype(F32))
        p = jnp.exp2(z - m_new.astype(BF16))
        alpha = jnp.exp2(m - m_new)
        l = alpha * l + jnp.sum(p.astype(F32), axis=1, keepdims=True)
        acc = alpha * acc + _dot(p, vj)
        return m_new, l, acc

    rows = grp * qb
    init = (jnp.full((rows, 1), MASK_NEG, F32), jnp.zeros((rows, 1), F32), jnp.zeros((rows, HEAD_DIM), F32))
    _, l, acc = lax.fori_loop(0, n_live, body, init)
    out = acc / l
    for r in range(grp):
        o_ref[:, r * HEAD_DIM:(r + 1) * HEAD_DIM] = out[r * qb:(r + 1) * qb, :].astype(o_ref.dtype)


def _dsa_attention(q, k, v, bias):
    s_dim = q.shape[0]
    qb, kb = DSA_AQB, DSA_KB
    gw = DSA_GROUP * HEAD_DIM
    kv = pl.BlockSpec((s_dim, HEAD_DIM), lambda g, i: (0, g))
    return pl.pallas_call(
        _dsa_attn_kernel,
        grid=(DSA_KV_HEADS, s_dim // qb),
        in_specs=[
            pl.BlockSpec((qb, gw), lambda g, i: (i, g)),
            kv, kv,
            pl.BlockSpec((qb // DSA_QB, s_dim // kb, DSA_QB, kb), lambda g, i: (i, 0, 0, 0)),
        ],
        out_specs=pl.BlockSpec((qb, gw), lambda g, i: (i, g)),
        out_shape=jax.ShapeDtypeStruct((s_dim, DSA_Q_W), BF16),
        compiler_params=_cparams("parallel", "arbitrary"),
        name="dsa_attention",
    )(q, k, v, bias)


ROUTE_E1, ROUTE_E2, ROUTE_G1, ROUTE_G2, ROUTE_R1, ROUTE_R2 = range(6)


def _router_kernel(x_ref, w_ref, o_ref, cnt_ref, seen):
    @pl.when(pl.program_id(0) == 0)
    def _():
        seen[...] = jnp.zeros_like(seen)

    x = x_ref[...]
    w = w_ref[...]
    xh = x.astype(BF16)
    xl = (x - xh.astype(F32)).astype(BF16)
    wh = w.astype(BF16)
    wl = (w - wh.astype(F32)).astype(BF16)
    logits = _dot(xh, wh) + (_dot(xh, wl) + _dot(xl, wh))
    lane = lax.broadcasted_iota(I32, logits.shape, 1).astype(F32)
    logits = jnp.where(lane < N_EXPERTS, logits, -jnp.inf)
    v1 = jnp.max(logits, axis=1, keepdims=True)
    i1 = jnp.min(jnp.where(logits == v1, lane, float(LANES)), axis=1, keepdims=True)
    rest = jnp.where(lane == i1, -jnp.inf, logits)
    v2 = jnp.max(rest, axis=1, keepdims=True)
    i2 = jnp.min(jnp.where(rest == v2, lane, float(LANES)), axis=1, keepdims=True)
    e2 = jnp.exp(v2 - v1)
    g1 = 1.0 / (1.0 + e2)
    g2 = e2 / (1.0 + e2)

    tb = x.shape[0]
    member = jnp.where((lane == i1) | (lane == i2), 1.0, 0.0)
    earlier = jnp.where(lax.broadcasted_iota(I32, (tb, tb), 1) < lax.broadcasted_iota(I32, (tb, tb), 0), 1.0, 0.0)
    prefix = _dot(earlier.astype(BF16), member.astype(BF16)) + seen[0:1, :]
    r1 = jnp.sum(jnp.where(lane == i1, prefix, 0.0), axis=1, keepdims=True)
    r2 = jnp.sum(jnp.where(lane == i2, prefix, 0.0), axis=1, keepdims=True)
    seen[...] = seen[...] + jnp.sum(member, axis=0, keepdims=True)
    cnt_ref[...] = seen[...]

    rec = jnp.zeros_like(logits)
    for slot, val in ((ROUTE_E1, i1), (ROUTE_E2, i2), (ROUTE_G1, g1), (ROUTE_G2, g2), (ROUTE_R1, r1), (ROUTE_R2, r2)):
        rec = jnp.where(lane == float(slot), val, rec)
    o_ref[...] = rec


def _router(x, w_pad, *, tm=256):
    s_dim, d_dim = x.shape
    tm = min(tm, s_dim)
    return pl.pallas_call(
        _router_kernel,
        grid=(s_dim // tm,),
        in_specs=[pl.BlockSpec((tm, d_dim), lambda i: (i, 0)), pl.BlockSpec((d_dim, LANES), lambda i: (0, 0))],
        out_specs=[pl.BlockSpec((tm, LANES), lambda i: (i, 0)), pl.BlockSpec((8, LANES), lambda i: (0, 0))],
        out_shape=[jax.ShapeDtypeStruct((s_dim, LANES), F32), jax.ShapeDtypeStruct((8, LANES), F32)],
        scratch_shapes=[pltpu.VMEM((8, LANES), F32)],
        compiler_params=_cparams("arbitrary"),
        name="moe_router",
    )(x, w_pad)


MOE_TM = 512


def _moe_plan(route, counts, s_dim):
    tm = min(MOE_TM, s_dim)
    n_rows = 2 * s_dim + N_EXPERTS * tm
    e1 = route[:, ROUTE_E1].astype(I32)
    e2 = route[:, ROUTE_E2].astype(I32)
    cnt = counts[0, :N_EXPERTS].astype(I32)
    padded = (cnt + tm - 1) // tm * tm
    ends = jnp.cumsum(padded)
    starts = ends - padded
    dest1 = starts[e1] + route[:, ROUTE_R1].astype(I32)
    dest2 = starts[e2] + route[:, ROUTE_R2].astype(I32)
    tok = jnp.arange(s_dim, dtype=I32)
    src_tok = jnp.zeros((n_rows,), I32).at[dest1].set(tok).at[dest2].set(tok)
    tile_start = jnp.arange(n_rows // tm, dtype=I32) * tm
    tile_expert = jnp.minimum(jnp.searchsorted(ends, tile_start, side="right"), N_EXPERTS - 1).astype(I32)
    n_used = (ends[-1:] // tm).astype(I32)
    return dict(tm=tm, n_rows=n_rows, dest=jnp.concatenate([dest1, dest2]), src_tok=src_tok,
                tile_expert=tile_expert, n_used=n_used)


def _row_copy(src_ref, src_row, dst_ref, dst_row, sem):
    return pltpu.make_async_copy(src_ref.at[pl.ds(src_row, 1), :], dst_ref.at[pl.ds(dst_row, 1), :], sem)


def _dispatch_kernel(src_tok_ref, x_ref, o_ref, buf, sem):
    i = pl.program_id(0)
    n = pl.num_programs(0)
    tm = buf.shape[1]

    def fetch(tile, slot):
        def issue(r, _):
            _row_copy(x_ref, src_tok_ref[tile * tm + r], buf.at[slot], r, sem.at[slot]).start()
            return 0
        lax.fori_loop(0, tm, issue, 0)

    @pl.when(i == 0)
    def _():
        fetch(0, 0)

    @pl.when(i + 1 < n)
    def _():
        fetch(i + 1, (i + 1) % 2)

    slot = i % 2
    pltpu.make_async_copy(x_ref.at[pl.ds(0, tm), :], buf.at[slot], sem.at[slot]).wait()
    o_ref[...] = buf[slot].astype(o_ref.dtype)


def _moe_dispatch(x, plan, *, tm=256):
    s_dim, d_dim = x.shape
    tm = min(tm, s_dim)
    n_rows = plan["n_rows"]
    return pl.pallas_call(
        _dispatch_kernel,
        grid_spec=pltpu.PrefetchScalarGridSpec(
            num_scalar_prefetch=1,
            grid=(n_rows // tm,),
            in_specs=[pl.BlockSpec(memory_space=pl.ANY)],
            out_specs=pl.BlockSpec((tm, d_dim), lambda i, st: (i, 0)),
            scratch_shapes=[pltpu.VMEM((2, tm, d_dim), F32), pltpu.SemaphoreType.DMA((2,))],
        ),
        out_shape=jax.ShapeDtypeStruct((n_rows, d_dim), BF16),
        compiler_params=_cparams("arbitrary"),
        name="moe_dispatch",
    )(plan["src_tok"], x)


def _moe_up_kernel(te_ref, nu_ref, a_ref, wg_ref, wu_ref, o_ref):
    m = pl.program_id(1)

    @pl.when(m < nu_ref[0])
    def _():
        a = a_ref[...]
        g = _dot(a, wg_ref[...].astype(BF16))
        u = _dot(a, wu_ref[...].astype(BF16))
        o_ref[...] = (g * _sigmoid(g) * u).astype(o_ref.dtype)

    @pl.when(m >= nu_ref[0])
    def _():
        o_ref[...] = jnp.zeros_like(o_ref)


def _moe_up(xs, wg, wu, layer, plan, *, tn=256):
    n_rows, d_dim = xs.shape
    tm = plan["tm"]
    w_spec = pl.BlockSpec((None, None, d_dim, tn), lambda n, m, te, nu: (layer, te[m], 0, n))
    return pl.pallas_call(
        _moe_up_kernel,
        grid_spec=pltpu.PrefetchScalarGridSpec(
            num_scalar_prefetch=2,
            grid=(D_FF_EXPERT // tn, n_rows // tm),
            in_specs=[pl.BlockSpec((tm, d_dim), lambda n, m, te, nu: (m, 0)), w_spec, w_spec],
            out_specs=pl.BlockSpec((tm, tn), lambda n, m, te, nu: (m, n)),
        ),
        out_shape=jax.ShapeDtypeStruct((n_rows, D_FF_EXPERT), BF16),
        compiler_params=_cparams("parallel", "arbitrary"),
        name="moe_up",
    )(plan["tile_expert"], plan["n_used"], xs, wg, wu)


def _moe_down_kernel(te_ref, nu_ref, a_ref, w_ref, o_ref):
    m = pl.program_id(1)

    @pl.when(m < nu_ref[0])
    def _():
        o_ref[...] = _dot(a_ref[...], w_ref[...].astype(BF16))

    @pl.when(m >= nu_ref[0])
    def _():
        o_ref[...] = jnp.zeros_like(o_ref)


def _moe_down(hid, wd, layer, plan, *, tn=2048):
    n_rows, f_dim = hid.shape
    tm = plan["tm"]
    d_dim = wd.shape[-1]
    return pl.pallas_call(
        _moe_down_kernel,
        grid_spec=pltpu.PrefetchScalarGridSpec(
            num_scalar_prefetch=2,
            grid=(d_dim // tn, n_rows // tm),
            in_specs=[pl.BlockSpec((tm, f_dim), lambda n, m, te, nu: (m, 0)),
                      pl.BlockSpec((None, None, f_dim, tn), lambda n, m, te, nu: (layer, te[m], 0, n))],
            out_specs=pl.BlockSpec((tm, tn), lambda n, m, te, nu: (m, n)),
        ),
        out_shape=jax.ShapeDtypeStruct((n_rows, d_dim), F32),
        compiler_params=_cparams("parallel", "arbitrary"),
        name="moe_down",
    )(plan["tile_expert"], plan["n_used"], hid, wd)


def _combine_ln_kernel(dest_ref, x_ref, r_ref, ys_ref, g_ref, b_ref, o_ref, ob_ref, buf, sem, *, s_dim):
    i = pl.program_id(0)
    n = pl.num_programs(0)
    tb = x_ref.shape[0]

    def fetch(blk, slot):
        def issue(r, _):
            t = blk * tb + r
            _row_copy(ys_ref, dest_ref[t], buf.at[slot, 0], r, sem.at[slot]).start()
            _row_copy(ys_ref, dest_ref[s_dim + t], buf.at[slot, 1], r, sem.at[slot]).start()
            return 0
        lax.fori_loop(0, tb, issue, 0)

    @pl.when(i == 0)
    def _():
        fetch(0, 0)

    @pl.when(i + 1 < n)
    def _():
        fetch(i + 1, (i + 1) % 2)

    slot = i % 2
    for k in range(2):
        pltpu.make_async_copy(ys_ref.at[pl.ds(0, tb), :], buf.at[slot, k], sem.at[slot]).wait()
    rec = r_ref[...]
    f = rec[:, ROUTE_G1:ROUTE_G1 + 1] * buf[slot, 0] + rec[:, ROUTE_G2:ROUTE_G2 + 1] * buf[slot, 1]
    y = DEEPNORM_ALPHA * x_ref[...] + f
    mu = jnp.mean(y, axis=-1, keepdims=True)
    d = y - mu
    var = jnp.mean(d * d, axis=-1, keepdims=True)
    out = d * lax.rsqrt(var + LN_EPS) * g_ref[...] + b_ref[...]
    o_ref[...] = out
    ob_ref[...] = out.astype(BF16)


def _moe_combine_ln(x, route, ys, plan, g3, b3, idx, *, tb=256):
    s_dim, d_dim = x.shape
    tb = min(tb, s_dim)
    row = pl.BlockSpec((tb, d_dim), lambda i, de: (i, 0))
    par = pl.BlockSpec((None, 1, d_dim), lambda i, de: (idx, 0, 0))
    return pl.pallas_call(
        functools.partial(_combine_ln_kernel, s_dim=s_dim),
        grid_spec=pltpu.PrefetchScalarGridSpec(
            num_scalar_prefetch=1,
            grid=(s_dim // tb,),
            in_specs=[row, pl.BlockSpec((tb, LANES), lambda i, de: (i, 0)), pl.BlockSpec(memory_space=pl.ANY),
                      par, par],
            out_specs=[row, row],
            scratch_shapes=[pltpu.VMEM((2, 2, tb, d_dim), F32), pltpu.SemaphoreType.DMA((2,))],
        ),
        out_shape=[jax.ShapeDtypeStruct((s_dim, d_dim), F32), jax.ShapeDtypeStruct((s_dim, d_dim), BF16)],
        compiler_params=_cparams("arbitrary"),
        name="moe_combine_ln",
    )(plan["dest"], x, route, ys, g3, b3)


def _even_layer(x, xb, j, i, even_w_in, even_w_out, ffn_w_gate, ffn_w_up, ffn_w_down, ln_g3, ln_b3, tabs):
    d = D_MODEL
    p = _matmul_rows([xb], even_w_in, lambda n: (j, 0, n), EVEN_IN, name="even_in_proj")
    a = _stick_breaking(p)
    r = _retention(p, tabs["log_gamma"], tabs["ret_cos"], tabs["ret_sin"])
    h = _matmul_rows([a, r], even_w_out, lambda n: (j, 0, n), d, name="even_out_proj")
    x, xb = _deepnorm_ln(x, h, ln_g3, ln_b3, 2 * i)
    hid = _swiglu_rows(xb, ffn_w_gate, ffn_w_up, lambda n: (j, 0, n), D_FF, name="ffn_up")
    f = _matmul_rows([hid], ffn_w_down, lambda n: (j, 0, n), d, name="ffn_down")
    return _deepnorm_ln(x, f, ln_g3, ln_b3, 2 * i + 1)


def _odd_layer(x, xb, j, i, odd_w_in, odd_w_out, moe_w_router, moe_w_gate, moe_w_up, moe_w_down,
               ln_g3, ln_b3, tabs):
    d = D_MODEL
    s_dim = x.shape[0]
    w_nk = jnp.swapaxes(odd_w_in, 1, 2)
    p = _matmul_rows([xb], w_nk, lambda n: (j, n, 0), ODD_MAIN, name="odd_in_proj", w_is_nk=True)
    w_tail = jnp.pad(w_nk[j, ODD_MAIN:, :], ((0, LANES - ODD_TAIL), (0, 0)))
    tail = _matmul_rows([xb], w_tail, lambda n: (n, 0), LANES, name="odd_in_tail", w_is_nk=True)

    cos_h, sin_h = tabs["head_cos"], tabs["head_sin"]
    q = _rope_cast(p, 0, DSA_Q_W, cos_h, sin_h, half=HEAD_DIM // ROT_FRACTION // 2,
                   scale=HEAD_DIM ** -0.5 * LOG2_E)
    k = _rope_cast(p, DSA_Q_W, DSA_KV_W, cos_h, sin_h, half=HEAD_DIM // ROT_FRACTION // 2, scale=1.0)
    v = _cast(p, DSA_Q_W + DSA_KV_W, DSA_KV_W)
    qi = _idx_q_prep(p, DSA_Q_W + 2 * DSA_KV_W, tabs["idx_cos"], tabs["idx_sin"])
    ki, wi = _idx_tail_prep(tail, tabs["tail_cos"], tabs["tail_sin"])

    bias = _dsa_select(qi, ki, wi, min(IDX_TOPK_MAX, s_dim // 4))
    y = _dsa_attention(q, k, v, bias)
    h = _matmul_rows([y], odd_w_out, lambda n: (j, 0, n), d, name="odd_out_proj")
    x, xb = _deepnorm_ln(x, h, ln_g3, ln_b3, 2 * i)

    w_router = jnp.pad(moe_w_router[j], ((0, 0), (0, LANES - N_EXPERTS)))
    route, counts = _router(x, w_router)
    plan = _moe_plan(route, counts, s_dim)
    xs = _moe_dispatch(x, plan)
    hid = _moe_up(xs, moe_w_gate, moe_w_up, j, plan)
    ys = _moe_down(hid, moe_w_down, j, plan)
    return _moe_combine_ln(x, route, ys, plan, ln_g3, ln_b3, 2 * i + 1)


def kernel(x, even_w_in, even_w_out, odd_w_in, odd_w_out, ffn_w_gate, ffn_w_up, ffn_w_down, moe_w_router,
           moe_w_gate, moe_w_up, moe_w_down, ln_g, ln_b):
    batch, s_dim, d = x.shape
    ln_g3 = ln_g.reshape(2 * DEPTH, 1, d)
    ln_b3 = ln_b.reshape(2 * DEPTH, 1, d)
    ret_cos, ret_sin = _rope_tables(s_dim, LANES, RET_QK_DIM, RET_THETA)
    head_cos, head_sin = _rope_tables(s_dim, LANES, HEAD_DIM // ROT_FRACTION, ROPE_THETA)
    idx_cos, idx_sin = _rope_tables(s_dim, IDX_DIM, IDX_DIM // ROT_FRACTION, ROPE_THETA)
    tail_cos, tail_sin = _rope_tables(s_dim, IDX_DIM, IDX_DIM // ROT_FRACTION, ROPE_THETA, active=IDX_DIM)
    tabs = dict(
        log_gamma=jnp.log1p(-jnp.exp2(-5.0 - jnp.arange(RET_HEADS, dtype=F32))),
        ret_cos=ret_cos, ret_sin=ret_sin, head_cos=head_cos, head_sin=head_sin,
        idx_cos=idx_cos, idx_sin=idx_sin, tail_cos=tail_cos, tail_sin=tail_sin,
    )
    outs = []
    for b in range(batch):
        xs = x[b] if batch > 1 else x.reshape(s_dim, d)
        xb = xs.astype(BF16)
        for i in range(DEPTH):
            j = i // 2
            if i % 2 == 0:
                xs, xb = _even_layer(xs, xb, j, i, even_w_in, even_w_out, ffn_w_gate, ffn_w_up, ffn_w_down,
                                     ln_g3, ln_b3, tabs)
            else:
                xs, xb = _odd_layer(xs, xb, j, i, odd_w_in, odd_w_out, moe_w_router, moe_w_gate, moe_w_up,
                                    moe_w_down, ln_g3, ln_b3, tabs)
        outs.append(xs)
    return jnp.stack(outs, axis=0) if batch > 1 else outs[0].reshape(1, s_dim, d)
```

```python
import functools
import math

import jax
import jax.numpy as jnp
from jax import lax
from jax.experimental import pallas as pl
from jax.experimental.pallas import tpu as pltpu

F32 = jnp.float32
BF16 = jnp.bfloat16
I32 = jnp.int32

D_MODEL = 4096
DEPTH = 4
HEAD_DIM = 128
SB_HEADS = 16
SB_W = SB_HEADS * HEAD_DIM
RET_HEADS = 8
RET_QK_DIM = 128
RET_V_DIM = 256
RET_QK_W = RET_HEADS * RET_QK_DIM
RET_V_W = RET_HEADS * RET_V_DIM
RET_THETA = 10000.0
DSA_Q_HEADS = 32
DSA_KV_HEADS = 8
DSA_GROUP = DSA_Q_HEADS // DSA_KV_HEADS
DSA_Q_W = DSA_Q_HEADS * HEAD_DIM
DSA_KV_W = DSA_KV_HEADS * HEAD_DIM
IDX_HEADS = 16
IDX_DIM = 64
IDX_Q_W = IDX_HEADS * IDX_DIM
IDX_TOPK_MAX = 256
ROPE_THETA = 500000.0
ROT_FRACTION = 4
D_FF = 11008
N_EXPERTS = 8
D_FF_EXPERT = 1792
LN_EPS = 1e-5
DEEPNORM_ALPHA = (2.0 * DEPTH) ** 0.25

EVEN_IN = 3 * SB_W + 2 * RET_QK_W + 2 * RET_V_W
ODD_MAIN = DSA_Q_W + 2 * DSA_KV_W + IDX_Q_W
ODD_TAIL = IDX_DIM + IDX_HEADS

LANES = 128
V7X_VMEM_BYTES = 64 * 1024 * 1024
VMEM_LIMIT_BYTES = V7X_VMEM_BYTES * 7 // 8

INT_MIN = -2147483648
MASK_NEG = -(2.0 ** 100)
LOG2_E = math.log2(math.e)


def _cparams(*sem):
    return pltpu.CompilerParams(dimension_semantics=sem, vmem_limit_bytes=VMEM_LIMIT_BYTES)


def _sigmoid(x):
    return 1.0 / (1.0 + jnp.exp(-x))


def _dot(a, b):
    return jnp.dot(a, b, preferred_element_type=F32)


def _dot_nt(a, b):
    return lax.dot_general(a, b, (((1,), (1,)), ((), ())), preferred_element_type=F32)


def _mm_rows_kernel(*refs, w_is_nk):
    *a_refs, w_ref, o_ref = refs
    k0, acc = 0, None
    for a_ref in a_refs:
        kw = a_ref.shape[1]
        if w_is_nk:
            part = _dot_nt(a_ref[...], w_ref[:, k0:k0 + kw].astype(BF16))
        else:
            part = _dot(a_ref[...], w_ref[k0:k0 + kw, :].astype(BF16))
        acc = part if acc is None else acc + part
        k0 += kw
    o_ref[...] = acc


ROW_PANEL_MAX_ROWS = 2048
ROW_PANEL_VMEM_SHARE = 0.4
ROW_STEP_VMEM_SHARE = 0.85


def _row_panel_tiles(m_dim, k_dim, n_cols, n_weights, out_bytes):
    tm = min(ROW_PANEL_MAX_ROWS, m_dim)
    while tm * k_dim * 2 > ROW_PANEL_VMEM_SHARE * VMEM_LIMIT_BYTES and tm % 2 == 0 and m_dim % (tm // 2) == 0:
        tm //= 2
    for tn in (4 * LANES, 2 * LANES, LANES):
        step = tm * k_dim * 2 + 2 * (n_weights * k_dim * tn * 4 + tm * tn * out_bytes)
        if n_cols % tn == 0 and step <= ROW_STEP_VMEM_SHARE * VMEM_LIMIT_BYTES:
            return tm, tn
    raise ValueError(f"no row-panel tiling for K={k_dim}, N={n_cols}")


def _matmul_rows(panels, w, w_index, n_cols, *, name, w_is_nk=False):
    m_dim = panels[0].shape[0]
    k_dim = sum(a.shape[1] for a in panels)
    tm, tn = _row_panel_tiles(m_dim, k_dim, n_cols, 1, 4)
    n_lead = w.ndim - 2
    w_block = (tn, k_dim) if w_is_nk else (k_dim, tn)
    return pl.pallas_call(
        functools.partial(_mm_rows_kernel, w_is_nk=w_is_nk),
        grid=(m_dim // tm, n_cols // tn),
        in_specs=[pl.BlockSpec((tm, a.shape[1]), lambda m, n: (m, 0), pipeline_mode=pl.Buffered(1))
                  for a in panels]
        + [pl.BlockSpec((None,) * n_lead + w_block, lambda m, n: w_index(n))],
        out_specs=pl.BlockSpec((tm, tn), lambda m, n: (m, n)),
        out_shape=jax.ShapeDtypeStruct((m_dim, n_cols), F32),
        compiler_params=_cparams("parallel", "arbitrary"),
        name=name,
    )(*panels, w)


def _swiglu_rows_kernel(a_ref, wg_ref, wu_ref, o_ref):
    a = a_ref[...]
    g = _dot(a, wg_ref[...].astype(BF16))
    u = _dot(a, wu_ref[...].astype(BF16))
    o_ref[...] = (g * _sigmoid(g) * u).astype(o_ref.dtype)


def _swiglu_rows(a, wg, wu, w_index, n_cols, *, name):
    m_dim, k_dim = a.shape
    tm, tn = _row_panel_tiles(m_dim, k_dim, n_cols, 2, 2)
    n_lead = wg.ndim - 2
    w_spec = pl.BlockSpec((None,) * n_lead + (k_dim, tn), lambda m, n: w_index(n))
    return pl.pallas_call(
        _swiglu_rows_kernel,
        grid=(m_dim // tm, n_cols // tn),
        in_specs=[pl.BlockSpec((tm, k_dim), lambda m, n: (m, 0), pipeline_mode=pl.Buffered(1)), w_spec, w_spec],
        out_specs=pl.BlockSpec((tm, tn), lambda m, n: (m, n)),
        out_shape=jax.ShapeDtypeStruct((m_dim, n_cols), BF16),
        compiler_params=_cparams("parallel", "arbitrary"),
        name=name,
    )(a, wg, wu)


def _ln_kernel(x_ref, h_ref, g_ref, b_ref, o_ref, ob_ref):
    y = DEEPNORM_ALPHA * x_ref[...] + h_ref[...]
    mu = jnp.mean(y, axis=-1, keepdims=True)
    d = y - mu
    var = jnp.mean(d * d, axis=-1, keepdims=True)
    out = d * lax.rsqrt(var + LN_EPS) * g_ref[...] + b_ref[...]
    o_ref[...] = out
    ob_ref[...] = out.astype(BF16)


def _deepnorm_ln(x, h, g3, b3, idx, *, tm=256):
    s_dim, d_dim = x.shape
    tm = min(tm, s_dim)
    row = pl.BlockSpec((tm, d_dim), lambda i: (i, 0))
    par = pl.BlockSpec((None, 1, d_dim), lambda i: (idx, 0, 0))
    return pl.pallas_call(
        _ln_kernel,
        grid=(s_dim // tm,),
        in_specs=[row, row, par, par],
        out_specs=[row, row],
        out_shape=[jax.ShapeDtypeStruct((s_dim, d_dim), F32), jax.ShapeDtypeStruct((s_dim, d_dim), BF16)],
        compiler_params=_cparams("parallel"),
        name="deepnorm_ln",
    )(x, h, g3, b3)


def _rope_tables(s_dim, period, rot_dim, theta, active=LANES):
    half = rot_dim // 2
    pos = jnp.arange(s_dim, dtype=F32)
    freqs = jnp.exp(-math.log(theta) * jnp.arange(half, dtype=F32) * (2.0 / rot_dim))
    ang = pos[:, None] * freqs[None, :]
    cos, sin = jnp.cos(ang), jnp.sin(ang)
    ones = jnp.ones((s_dim, period - rot_dim), F32)
    zeros = jnp.zeros((s_dim, period - rot_dim), F32)
    cos_p = jnp.concatenate([cos, cos, ones], axis=1)
    sin_p = jnp.concatenate([-sin, sin, zeros], axis=1)
    reps = LANES // period
    cos_t, sin_t = jnp.tile(cos_p, (1, reps)), jnp.tile(sin_p, (1, reps))
    if active < LANES:
        lane = jnp.arange(LANES)[None, :]
        cos_t = jnp.where(lane < active, cos_t, 1.0)
        sin_t = jnp.where(lane < active, sin_t, 0.0)
    return cos_t, sin_t


def _rope_tile(x, cos, sin, half, period=LANES):
    if 2 * half == LANES:
        return x * cos + pltpu.roll(x, half, 1) * sin
    lane = lax.broadcasted_iota(I32, x.shape, 1)
    first = (lane & (period - 1)) < half
    partner = jnp.where(first, pltpu.roll(x, LANES - half, 1), pltpu.roll(x, half, 1))
    return x * cos + partner * sin


SB_BLOCK = 256
SB_GROUP = 2
SB_DEAD = 160.0


def _sb_kernel(q_ref, k_ref, v_ref, o_ref, kb_ref, vb_ref):
    i = pl.program_id(1)
    blk = SB_BLOCK
    span = SB_GROUP * blk

    @pl.when(i == 0)
    def _():
        kb_ref[...] = k_ref[...].astype(BF16)
        vb_ref[...] = v_ref[...].astype(BF16)

    q = (q_ref[...] * (HEAD_DIM ** -0.5 * LOG2_E)).astype(BF16)
    row = lax.broadcasted_iota(I32, (blk, blk), 0)
    col = lax.broadcasted_iota(I32, (blk, blk), 1)
    later = jnp.where(row > col, 1.0, 0.0).astype(BF16)

    def group(start, acc, run, key_end):
        z = _dot_nt(q, kb_ref[pl.ds(start, span), :])
        soft = jnp.log(1.0 + jnp.exp2(-jnp.abs(z))) * LOG2_E
        sp = jnp.maximum(z, 0.0) + soft
        log_beta = z - sp
        if key_end is not None:
            key_pos = start + lax.broadcasted_iota(I32, (blk, span), 1)
            q_pos = i * blk + lax.broadcasted_iota(I32, (blk, span), 0)
            causal = (key_pos < q_pos) & (key_pos < key_end)
            sp = jnp.where(causal, sp, 0.0)
        parts = [None] * SB_GROUP
        for c in reversed(range(SB_GROUP)):
            sp_c = sp[:, c * blk:(c + 1) * blk]
            parts[c] = _dot(sp_c.astype(BF16), later) + run
            run = run + jnp.sum(sp_c, axis=1, keepdims=True)
        w = jnp.exp2(log_beta - jnp.concatenate(parts, axis=1))
        if key_end is not None:
            w = jnp.where(causal, w, 0.0)
        acc = acc + _dot(w.astype(BF16), vb_ref[pl.ds(start, span), :])
        return acc, run

    below = jnp.maximum(i - (SB_GROUP - 1), 0)
    n_full = below // SB_GROUP
    n_left = below - n_full * SB_GROUP
    acc0 = jnp.zeros((blk, HEAD_DIM), F32)
    run0 = jnp.zeros((blk, 1), F32)
    acc, run = group(pl.multiple_of(below * blk, blk), acc0, run0, (i + 1) * blk)

    def live(run):
        return jnp.min(run) < SB_DEAD

    def cond(carry):
        return jnp.logical_and(carry[0] < n_full, live(carry[2]))

    def body(carry):
        t, acc, run = carry
        start = pl.multiple_of((below - SB_GROUP * (t + 1)) * blk, blk)
        acc, run = group(start, acc, run, None)
        return t + 1, acc, run

    t, acc, run = lax.while_loop(cond, body, (jnp.int32(0), acc, run))
    tail = jnp.logical_and(jnp.logical_and(t == n_full, n_left > 0), live(run))
    acc, run = lax.cond(tail, lambda: group(0, acc, run, n_left * blk), lambda: (acc, run))
    o_ref[...] = acc.astype(o_ref.dtype)


def _stick_breaking(p):
    s_dim = p.shape[0]
    blk = SB_BLOCK
    kv = lambda off: pl.BlockSpec((s_dim, HEAD_DIM), lambda h, i: (0, off + h))
    return pl.pallas_call(
        _sb_kernel,
        grid=(SB_HEADS, s_dim // blk),
        in_specs=[pl.BlockSpec((blk, HEAD_DIM), lambda h, i: (i, h)), kv(SB_HEADS), kv(2 * SB_HEADS)],
        out_specs=pl.BlockSpec((blk, HEAD_DIM), lambda h, i: (i, h)),
        out_shape=jax.ShapeDtypeStruct((s_dim, SB_W), BF16),
        scratch_shapes=[pltpu.VMEM((s_dim, HEAD_DIM), BF16), pltpu.VMEM((s_dim, HEAD_DIM), BF16)],
        compiler_params=_cparams("parallel", "arbitrary"),
        name="stick_breaking",
    )(p, p, p)


RET_CHUNK = 512


def _ret_kernel(lg_ref, q_ref, k_ref, v_ref, g_ref, cos_ref, sin_ref, o_ref, state):
    h = pl.program_id(0)
    c = pl.program_id(1)
    n = RET_CHUNK

    @pl.when(c == 0)
    def _():
        state[...] = jnp.zeros_like(state)

    lg = lg_ref[h]
    cos, sin = cos_ref[...], sin_ref[...]
    q = _rope_tile(q_ref[...], cos, sin, RET_QK_DIM // 2)
    k = _rope_tile(k_ref[...], cos, sin, RET_QK_DIM // 2) * (RET_QK_DIM ** -0.5)
    v = v_ref[...].astype(BF16)

    ii = lax.broadcasted_iota(I32, (n, n), 0)
    jj = lax.broadcasted_iota(I32, (n, n), 1)
    rel = (ii - jj).astype(F32)
    decay = jnp.where(rel >= 0.0, jnp.exp(lg * jnp.maximum(rel, 0.0)), 0.0)
    inner = _dot_nt(q.astype(BF16), k.astype(BF16)) * decay
    out = _dot(inner.astype(BF16), v)

    pos = lax.broadcasted_iota(I32, (n, 1), 0).astype(F32)
    q_decay = jnp.exp(lg * (pos + 1.0))
    k_decay = jnp.exp(lg * (n - 1.0 - pos))
    prev = state[...]
    out = out + _dot((q * q_decay).astype(BF16), prev.astype(BF16))
    kd_t = jnp.transpose(k * k_decay).astype(BF16)
    state[...] = jnp.exp(lg * jnp.full((1, 1), n, F32)) * prev + _dot(kd_t, v)

    mu = jnp.mean(out, axis=-1, keepdims=True)
    d = out - mu
    var = jnp.mean(d * d, axis=-1, keepdims=True)
    g = g_ref[...]
    o_ref[...] = (g * _sigmoid(g) * (d * lax.rsqrt(var + LN_EPS))).astype(o_ref.dtype)


def _retention(p, log_gamma, cos, sin):
    s_dim = p.shape[0]
    n = RET_CHUNK
    qk = lambda off: pl.BlockSpec((n, RET_QK_DIM), lambda h, c, lg: (c, off + h))
    vg = lambda off: pl.BlockSpec((n, RET_V_DIM), lambda h, c, lg: (c, off + h))
    tab = pl.BlockSpec((n, LANES), lambda h, c, lg: (c, 0))
    q_off = 3 * SB_W // RET_QK_DIM
    v_off = (3 * SB_W + 2 * RET_QK_W) // RET_V_DIM
    grid_spec = pltpu.PrefetchScalarGridSpec(
        num_scalar_prefetch=1,
        grid=(RET_HEADS, s_dim // n),
        in_specs=[qk(q_off), qk(q_off + RET_HEADS), vg(v_off), vg(v_off + RET_HEADS), tab, tab],
        out_specs=pl.BlockSpec((n, RET_V_DIM), lambda h, c, lg: (c, h)),
        scratch_shapes=[pltpu.VMEM((RET_QK_DIM, RET_V_DIM), F32)],
    )
    return pl.pallas_call(
        _ret_kernel,
        grid_spec=grid_spec,
        out_shape=jax.ShapeDtypeStruct((s_dim, RET_V_W), BF16),
        compiler_params=_cparams("parallel", "arbitrary"),
        name="retention",
    )(log_gamma, p, p, p, p, cos, sin)


def _rope_cast_kernel(x_ref, cos_ref, sin_ref, o_ref, *, half, scale):
    cos, sin = cos_ref[...], sin_ref[...]
    for t in range(x_ref.shape[1] // LANES):
        sl = slice(t * LANES, (t + 1) * LANES)
        o_ref[:, sl] = (_rope_tile(x_ref[:, sl], cos, sin, half) * scale).astype(o_ref.dtype)


def _rope_cast(p, col0, n_cols, cos, sin, *, half, scale, tm=512, tn=1024):
    s_dim = p.shape[0]
    tm = min(tm, s_dim)
    tn = min(tn, n_cols)
    tab = pl.BlockSpec((tm, LANES), lambda i, j: (i, 0))
    return pl.pallas_call(
        functools.partial(_rope_cast_kernel, half=half, scale=scale),
        grid=(s_dim // tm, n_cols // tn),
        in_specs=[pl.BlockSpec((tm, tn), lambda i, j: (i, col0 // tn + j)), tab, tab],
        out_specs=pl.BlockSpec((tm, tn), lambda i, j: (i, j)),
        out_shape=jax.ShapeDtypeStruct((s_dim, n_cols), BF16),
        compiler_params=_cparams("parallel", "parallel"),
        name="rope_cast",
    )(p, cos, sin)


def _cast_kernel(x_ref, o_ref):
    o_ref[...] = x_ref[...].astype(o_ref.dtype)


def _cast(p, col0, n_cols, *, tm=512, tn=1024):
    s_dim = p.shape[0]
    tm = min(tm, s_dim)
    return pl.pallas_call(
        _cast_kernel,
        grid=(s_dim // tm, n_cols // tn),
        in_specs=[pl.BlockSpec((tm, tn), lambda i, j: (i, col0 // tn + j))],
        out_specs=pl.BlockSpec((tm, tn), lambda i, j: (i, j)),
        out_shape=jax.ShapeDtypeStruct((s_dim, n_cols), BF16),
        compiler_params=_cparams("parallel", "parallel"),
        name="cast_bf16",
    )(p)


def _idx_q_kernel(x_ref, cos_ref, sin_ref, o_ref):
    cos, sin = cos_ref[...], sin_ref[...]
    lane = lax.broadcasted_iota(I32, cos.shape, 1)
    low = lane < IDX_DIM
    for t in range(x_ref.shape[1] // LANES):
        y = _rope_tile(x_ref[:, t * LANES:(t + 1) * LANES], cos, sin, IDX_DIM // ROT_FRACTION // 2, IDX_DIM)
        o_ref[:, (2 * t) * LANES:(2 * t + 1) * LANES] = jnp.where(low, y, 0.0).astype(o_ref.dtype)
        o_ref[:, (2 * t + 1) * LANES:(2 * t + 2) * LANES] = jnp.where(
            low, pltpu.roll(y, IDX_DIM, 1), 0.0).astype(o_ref.dtype)


def _idx_q_prep(p, col0, cos, sin, *, tm=512):
    s_dim = p.shape[0]
    tm = min(tm, s_dim)
    tab = pl.BlockSpec((tm, LANES), lambda i: (i, 0))
    return pl.pallas_call(
        _idx_q_kernel,
        grid=(s_dim // tm,),
        in_specs=[pl.BlockSpec((tm, IDX_Q_W), lambda i: (i, col0 // IDX_Q_W)), tab, tab],
        out_specs=pl.BlockSpec((tm, IDX_HEADS * LANES), lambda i: (i, 0)),
        out_shape=jax.ShapeDtypeStruct((s_dim, IDX_HEADS * LANES), BF16),
        compiler_params=_cparams("parallel"),
        name="idx_q_prep",
    )(p, cos, sin)


def _idx_tail_kernel(x_ref, cos_ref, sin_ref, k_ref, w_ref):
    x = x_ref[...]
    lane = lax.broadcasted_iota(I32, x.shape, 1)
    y = _rope_tile(x, cos_ref[...], sin_ref[...], IDX_DIM // ROT_FRACTION // 2, IDX_DIM) * (IDX_DIM ** -0.5)
    k_ref[...] = jnp.where(lane < IDX_DIM, y, 0.0).astype(k_ref.dtype)
    w_ref[...] = x * (IDX_HEADS ** -0.5)


def _idx_tail_prep(tail, cos, sin, *, tm=512):
    s_dim = tail.shape[0]
    tm = min(tm, s_dim)
    blk = pl.BlockSpec((tm, LANES), lambda i: (i, 0))
    return pl.pallas_call(
        _idx_tail_kernel,
        grid=(s_dim // tm,),
        in_specs=[blk, blk, blk],
        out_specs=[blk, blk],
        out_shape=[jax.ShapeDtypeStruct((s_dim, LANES), BF16), jax.ShapeDtypeStruct((s_dim, LANES), F32)],
        compiler_params=_cparams("parallel"),
        name="idx_tail_prep",
    )(tail, cos, sin)


DSA_QB = 128
DSA_KB = 512
DSA_AQB = 512


def _select_kernel(qi_ref, ki_ref, w_ref, o_ref, keys_ref, *, topk, n_kblocks, idx_bits):
    i = pl.program_id(0)
    qb, kb = DSA_QB, DSA_KB
    n_live = ((i + 1) * qb + kb - 1) // kb
    q_pos = i * qb + lax.broadcasted_iota(I32, (qb, kb), 0)
    col0 = lax.broadcasted_iota(I32, (qb, kb), 1)
    w = w_ref[...]

    def score_block(jb, _):
        start = pl.multiple_of(jb * kb, kb)
        kj = ki_ref[pl.ds(start, kb), :]
        acc = jnp.zeros((qb, kb), F32)
        for h in range(IDX_HEADS):
            z = _dot_nt(qi_ref[:, h * LANES:(h + 1) * LANES], kj)
            acc = acc + jnp.maximum(z, 0.0) * w[:, IDX_DIM + h:IDX_DIM + h + 1]
        acc = jnp.where(acc == 0.0, 0.0, acc)
        bits = lax.bitcast_convert_type(acc, I32)
        key = jnp.where(bits >= 0, bits, bits ^ 0x7FFFFFFF)
        keys_ref[jb] = jnp.where(jb * kb + col0 <= q_pos, key, INT_MIN)
        return 0

    lax.fori_loop(0, n_live, score_block, 0)

    def count(pred):
        def body(jb, part):
            kk = keys_ref[jb]
            hit = jnp.where(pred(kk, jb * kb + col0), 1.0, 0.0)
            for t in range(kb // LANES):
                part = part + hit[:, t * LANES:(t + 1) * LANES]
            return part
        part = lax.fori_loop(0, n_live, body, jnp.zeros((qb, LANES), F32))
        return jnp.sum(part, axis=1, keepdims=True)

    k_f = float(topk)
    c_nonneg = count(lambda kk, cc: kk >= 0)
    thr = jnp.where(c_nonneg >= k_f, jnp.int32(0), jnp.int32(INT_MIN))
    c_thr = jnp.where(c_nonneg >= k_f, c_nonneg, float(n_kblocks * kb))
    short = i * qb + lax.broadcasted_iota(I32, (qb, 1), 0) + 1 < topk

    def unsettled(c_thr):
        return jnp.max(jnp.where((c_thr == k_f) | short, 0.0, 1.0)) > 0.0

    def thr_cond(carry):
        return jnp.logical_and(carry[0] < 31, unsettled(carry[2]))

    def thr_bit(carry):
        b, thr, c_thr = carry
        cand = thr + jnp.left_shift(jnp.int32(1), 30 - b)
        c = count(lambda kk, cc: kk >= cand)
        ok = c >= k_f
        return b + 1, jnp.where(ok, cand, thr), jnp.where(ok, c, c_thr)

    _, thr, c_thr = lax.while_loop(thr_cond, thr_bit, (jnp.int32(0), thr, c_thr))

    need = k_f - count(lambda kk, cc: kk > thr)
    n_eq = count(lambda kk, cc: kk == thr)
    tied = (n_eq > need) & (thr != INT_MIN)

    def cut_bit(b, cut):
        cand = cut + jnp.left_shift(jnp.int32(1), idx_bits - 1 - b)
        c = count(lambda kk, cc: (kk == thr) & (cc < cand))
        return jnp.where(c < need, cand, cut)

    def cut_search():
        return lax.fori_loop(0, idx_bits, cut_bit, jnp.zeros((qb, 1), I32))

    def no_cut():
        return jnp.full((qb, 1), n_kblocks * kb, I32)

    cut = lax.cond(jnp.max(jnp.where(tied, 1.0, 0.0)) > 0.0, cut_search, no_cut)

    def write_live(jb, _):
        kk = keys_ref[jb]
        cc = jb * kb + col0
        sel = ((kk > thr) | ((kk == thr) & (cc <= cut))) & (kk != INT_MIN)
        o_ref[jb] = jnp.where(sel, 0.0, MASK_NEG).astype(o_ref.dtype)
        return 0

    lax.fori_loop(0, n_live, write_live, 0)

    def write_dead(jb, _):
        o_ref[jb] = jnp.full((qb, kb), MASK_NEG, o_ref.dtype)
        return 0

    lax.fori_loop(n_live, n_kblocks, write_dead, 0)


def _dsa_select(qi, ki, w, topk):
    s_dim = qi.shape[0]
    qb, kb = DSA_QB, DSA_KB
    n_kblocks = s_dim // kb
    idx_bits = max(1, (s_dim - 1).bit_length())
    return pl.pallas_call(
        functools.partial(_select_kernel, topk=topk, n_kblocks=n_kblocks, idx_bits=idx_bits),
        grid=(s_dim // qb,),
        in_specs=[
            pl.BlockSpec((qb, IDX_HEADS * LANES), lambda i: (i, 0)),
            pl.BlockSpec((s_dim, LANES), lambda i: (0, 0)),
            pl.BlockSpec((qb, LANES), lambda i: (i, 0)),
        ],
        out_specs=pl.BlockSpec((None, n_kblocks, qb, kb), lambda i: (i, 0, 0, 0)),
        out_shape=jax.ShapeDtypeStruct((s_dim // qb, n_kblocks, qb, kb), BF16),
        scratch_shapes=[pltpu.VMEM((n_kblocks, qb, kb), I32)],
        compiler_params=_cparams("parallel"),
        name="dsa_select",
    )(qi, ki, w)


def _dsa_attn_kernel(q_ref, k_ref, v_ref, b_ref, o_ref):
    i = pl.program_id(1)
    qb, kb, grp = DSA_AQB, DSA_KB, DSA_GROUP
    n_sel = qb // DSA_QB
    n_live = ((i + 1) * qb + kb - 1) // kb
    q = jnp.concatenate([q_ref[:, r * HEAD_DIM:(r + 1) * HEAD_DIM] for r in range(grp)], axis=0)

    def body(jb, carry):
        m, l, acc = carry
        start = pl.multiple_of(jb * kb, kb)
        kj = k_ref[pl.ds(start, kb), :]
        vj = v_ref[pl.ds(start, kb), :]
        z = _dot_nt(q, kj).astype(BF16)
        bias = b_ref[:, jb].reshape(1, qb, kb)
        z = (z.reshape(grp, qb, kb) + bias).reshape(grp * qb, kb)
        m_new = jnp.maximum(m, jnp.max(z, axis=1, keepdims=True).astype(F32))
        p = jnp.exp2(z - m_new.astype(BF16))
        alpha = jnp.exp2(m - m_new)
        l = alpha * l + jnp.sum(p.astype(F32), axis=1, keepdims=True)
        acc = alpha * acc + _dot(p, vj)
        return m_new, l, acc

    rows = grp * qb
    init = (jnp.full((rows, 1), MASK_NEG, F32), jnp.zeros((rows, 1), F32), jnp.zeros((rows, HEAD_DIM), F32))
    _, l, acc = lax.fori_loop(0, n_live, body, init)
    out = acc / l
    for r in range(grp):
        o_ref[:, r * HEAD_DIM:(r + 1) * HEAD_DIM] = out[r * qb:(r + 1) * qb, :].astype(o_ref.dtype)


def _dsa_attention(q, k, v, bias):
    s_dim = q.shape[0]
    qb, kb = DSA_AQB, DSA_KB
    gw = DSA_GROUP * HEAD_DIM
    kv = pl.BlockSpec((s_dim, HEAD_DIM), lambda g, i: (0, g))
    return pl.pallas_call(
        _dsa_attn_kernel,
        grid=(DSA_KV_HEADS, s_dim // qb),
        in_specs=[
            pl.BlockSpec((qb, gw), lambda g, i: (i, g)),
            kv, kv,
            pl.BlockSpec((qb // DSA_QB, s_dim // kb, DSA_QB, kb), lambda g, i: (i, 0, 0, 0)),
        ],
        out_specs=pl.BlockSpec((qb, gw), lambda g, i: (i, g)),
        out_shape=jax.ShapeDtypeStruct((s_dim, DSA_Q_W), BF16),
        compiler_params=_cparams("parallel", "arbitrary"),
        name="dsa_attention",
    )(q, k, v, bias)


ROUTE_E1, ROUTE_E2, ROUTE_G1, ROUTE_G2, ROUTE_R1, ROUTE_R2 = range(6)


def _router_kernel(x_ref, w_ref, o_ref, cnt_ref, seen):
    @pl.when(pl.program_id(0) == 0)
    def _():
        seen[...] = jnp.zeros_like(seen)

    x = x_ref[...]
    w = w_ref[...]
    xh = x.astype(BF16)
    xl = (x - xh.astype(F32)).astype(BF16)
    wh = w.astype(BF16)
    wl = (w - wh.astype(F32)).astype(BF16)
    logits = _dot(xh, wh) + (_dot(xh, wl) + _dot(xl, wh))
    lane = lax.broadcasted_iota(I32, logits.shape, 1).astype(F32)
    logits = jnp.where(lane < N_EXPERTS, logits, -jnp.inf)
    v1 = jnp.max(logits, axis=1, keepdims=True)
    i1 = jnp.min(jnp.where(logits == v1, lane, float(LANES)), axis=1, keepdims=True)
    rest = jnp.where(lane == i1, -jnp.inf, logits)
    v2 = jnp.max(rest, axis=1, keepdims=True)
    i2 = jnp.min(jnp.where(rest == v2, lane, float(LANES)), axis=1, keepdims=True)
    e2 = jnp.exp(v2 - v1)
    g1 = 1.0 / (1.0 + e2)
    g2 = e2 / (1.0 + e2)

    tb = x.shape[0]
    member = jnp.where((lane == i1) | (lane == i2), 1.0, 0.0)
    earlier = jnp.where(lax.broadcasted_iota(I32, (tb, tb), 1) < lax.broadcasted_iota(I32, (tb, tb), 0), 1.0, 0.0)
    prefix = _dot(earlier.astype(BF16), member.astype(BF16)) + seen[0:1, :]
    r1 = jnp.sum(jnp.where(lane == i1, prefix, 0.0), axis=1, keepdims=True)
    r2 = jnp.sum(jnp.where(lane == i2, prefix, 0.0), axis=1, keepdims=True)
    seen[...] = seen[...] + jnp.sum(member, axis=0, keepdims=True)
    cnt_ref[...] = seen[...]

    rec = jnp.zeros_like(logits)
    for slot, val in ((ROUTE_E1, i1), (ROUTE_E2, i2), (ROUTE_G1, g1), (ROUTE_G2, g2), (ROUTE_R1, r1), (ROUTE_R2, r2)):
        rec = jnp.where(lane == float(slot), val, rec)
    o_ref[...] = rec


def _router(x, w_pad, *, tm=256):
    s_dim, d_dim = x.shape
    tm = min(tm, s_dim)
    return pl.pallas_call(
        _router_kernel,
        grid=(s_dim // tm,),
        in_specs=[pl.BlockSpec((tm, d_dim), lambda i: (i, 0)), pl.BlockSpec((d_dim, LANES), lambda i: (0, 0))],
        out_specs=[pl.BlockSpec((tm, LANES), lambda i: (i, 0)), pl.BlockSpec((8, LANES), lambda i: (0, 0))],
        out_shape=[jax.ShapeDtypeStruct((s_dim, LANES), F32), jax.ShapeDtypeStruct((8, LANES), F32)],
        scratch_shapes=[pltpu.VMEM((8, LANES), F32)],
        compiler_params=_cparams("arbitrary"),
        name="moe_router",
    )(x, w_pad)


MOE_TM = 512


def _moe_plan(route, counts, s_dim):
    tm = min(MOE_TM, s_dim)
    n_rows = 2 * s_dim + N_EXPERTS * tm
    e1 = route[:, ROUTE_E1].astype(I32)
    e2 = route[:, ROUTE_E2].astype(I32)
    cnt = counts[0, :N_EXPERTS].astype(I32)
    padded = (cnt + tm - 1) // tm * tm
    ends = jnp.cumsum(padded)
    starts = ends - padded
    dest1 = starts[e1] + route[:, ROUTE_R1].astype(I32)
    dest2 = starts[e2] + route[:, ROUTE_R2].astype(I32)
    tok = jnp.arange(s_dim, dtype=I32)
    src_tok = jnp.zeros((n_rows,), I32).at[dest1].set(tok).at[dest2].set(tok)
    tile_start = jnp.arange(n_rows // tm, dtype=I32) * tm
    tile_expert = jnp.minimum(jnp.searchsorted(ends, tile_start, side="right"), N_EXPERTS - 1).astype(I32)
    n_used = (ends[-1:] // tm).astype(I32)
    return dict(tm=tm, n_rows=n_rows, dest=jnp.concatenate([dest1, dest2]), src_tok=src_tok,
                tile_expert=tile_expert, n_used=n_used)


def _row_copy(src_ref, src_row, dst_ref, dst_row, sem):
    return pltpu.make_async_copy(src_ref.at[pl.ds(src_row, 1), :], dst_ref.at[pl.ds(dst_row, 1), :], sem)


def _dispatch_kernel(src_tok_ref, x_ref, o_ref, buf, sem):
    i = pl.program_id(0)
    n = pl.num_programs(0)
    tm = buf.shape[1]

    def fetch(tile, slot):
        def issue(r, _):
            _row_copy(x_ref, src_tok_ref[tile * tm + r], buf.at[slot], r, sem.at[slot]).start()
            return 0
        lax.fori_loop(0, tm, issue, 0)

    @pl.when(i == 0)
    def _():
        fetch(0, 0)

    @pl.when(i + 1 < n)
    def _():
        fetch(i + 1, (i + 1) % 2)

    slot = i % 2
    pltpu.make_async_copy(x_ref.at[pl.ds(0, tm), :], buf.at[slot], sem.at[slot]).wait()
    o_ref[...] = buf[slot].astype(o_ref.dtype)


def _moe_dispatch(x, plan, *, tm=256):
    s_dim, d_dim = x.shape
    tm = min(tm, s_dim)
    n_rows = plan["n_rows"]
    return pl.pallas_call(
        _dispatch_kernel,
        grid_spec=pltpu.PrefetchScalarGridSpec(
            num_scalar_prefetch=1,
            grid=(n_rows // tm,),
            in_specs=[pl.BlockSpec(memory_space=pl.ANY)],
            out_specs=pl.BlockSpec((tm, d_dim), lambda i, st: (i, 0)),
            scratch_shapes=[pltpu.VMEM((2, tm, d_dim), F32), pltpu.SemaphoreType.DMA((2,))],
        ),
        out_shape=jax.ShapeDtypeStruct((n_rows, d_dim), BF16),
        compiler_params=_cparams("arbitrary"),
        name="moe_dispatch",
    )(plan["src_tok"], x)


def _moe_up_kernel(te_ref, nu_ref, a_ref, wg_ref, wu_ref, o_ref):
    m = pl.program_id(1)

    @pl.when(m < nu_ref[0])
    def _():
        a = a_ref[...]
        g = _dot(a, wg_ref[...].astype(BF16))
        u = _dot(a, wu_ref[...].astype(BF16))
        o_ref[...] = (g * _sigmoid(g) * u).astype(o_ref.dtype)

    @pl.when(m >= nu_ref[0])
    def _():
        o_ref[...] = jnp.zeros_like(o_ref)


def _moe_up(xs, wg, wu, layer, plan, *, tn=256):
    n_rows, d_dim = xs.shape
    tm = plan["tm"]
    w_spec = pl.BlockSpec((None, None, d_dim, tn), lambda n, m, te, nu: (layer, te[m], 0, n))
    return pl.pallas_call(
        _moe_up_kernel,
        grid_spec=pltpu.PrefetchScalarGridSpec(
            num_scalar_prefetch=2,
            grid=(D_FF_EXPERT // tn, n_rows // tm),
            in_specs=[pl.BlockSpec((tm, d_dim), lambda n, m, te, nu: (m, 0)), w_spec, w_spec],
            out_specs=pl.BlockSpec((tm, tn), lambda n, m, te, nu: (m, n)),
        ),
        out_shape=jax.ShapeDtypeStruct((n_rows, D_FF_EXPERT), BF16),
        compiler_params=_cparams("parallel", "arbitrary"),
        name="moe_up",
    )(plan["tile_expert"], plan["n_used"], xs, wg, wu)


def _moe_down_kernel(te_ref, nu_ref, a_ref, w_ref, o_ref):
    m = pl.program_id(1)

    @pl.when(m < nu_ref[0])
    def _():
        o_ref[...] = _dot(a_ref[...], w_ref[...].astype(BF16))

    @pl.when(m >= nu_ref[0])
    def _():
        o_ref[...] = jnp.zeros_like(o_ref)


def _moe_down(hid, wd, layer, plan, *, tn=2048):
    n_rows, f_dim = hid.shape
    tm = plan["tm"]
    d_dim = wd.shape[-1]
    return pl.pallas_call(
        _moe_down_kernel,
        grid_spec=pltpu.PrefetchScalarGridSpec(
            num_scalar_prefetch=2,
            grid=(d_dim // tn, n_rows // tm),
            in_specs=[pl.BlockSpec((tm, f_dim), lambda n, m, te, nu: (m, 0)),
                      pl.BlockSpec((None, None, f_dim, tn), lambda n, m, te, nu: (layer, te[m], 0, n))],
            out_specs=pl.BlockSpec((tm, tn), lambda n, m, te, nu: (m, n)),
        ),
        out_shape=jax.ShapeDtypeStruct((n_rows, d_dim), F32),
        compiler_params=_cparams("parallel", "arbitrary"),
        name="moe_down",
    )(plan["tile_expert"], plan["n_used"], hid, wd)


def _combine_ln_kernel(dest_ref, x_ref, r_ref, ys_ref, g_ref, b_ref, o_ref, ob_ref, buf, sem, *, s_dim):
    i = pl.program_id(0)
    n = pl.num_programs(0)
    tb = x_ref.shape[0]

    def fetch(blk, slot):
        def issue(r, _):
            t = blk * tb + r
            _row_copy(ys_ref, dest_ref[t], buf.at[slot, 0], r, sem.at[slot]).start()
            _row_copy(ys_ref, dest_ref[s_dim + t], buf.at[slot, 1], r, sem.at[slot]).start()
            return 0
        lax.fori_loop(0, tb, issue, 0)

    @pl.when(i == 0)
    def _():
        fetch(0, 0)

    @pl.when(i + 1 < n)
    def _():
        fetch(i + 1, (i + 1) % 2)

    slot = i % 2
    for k in range(2):
        pltpu.make_async_copy(ys_ref.at[pl.ds(0, tb), :], buf.at[slot, k], sem.at[slot]).wait()
    rec = r_ref[...]
    f = rec[:, ROUTE_G1:ROUTE_G1 + 1] * buf[slot, 0] + rec[:, ROUTE_G2:ROUTE_G2 + 1] * buf[slot, 1]
    y = DEEPNORM_ALPHA * x_ref[...] + f
    mu = jnp.mean(y, axis=-1, keepdims=True)
    d = y - mu
    var = jnp.mean(d * d, axis=-1, keepdims=True)
    out = d * lax.rsqrt(var + LN_EPS) * g_ref[...] + b_ref[...]
    o_ref[...] = out
    ob_ref[...] = out.astype(BF16)


def _moe_combine_ln(x, route, ys, plan, g3, b3, idx, *, tb=256):
    s_dim, d_dim = x.shape
    tb = min(tb, s_dim)
    row = pl.BlockSpec((tb, d_dim), lambda i, de: (i, 0))
    par = pl.BlockSpec((None, 1, d_dim), lambda i, de: (idx, 0, 0))
    return pl.pallas_call(
        functools.partial(_combine_ln_kernel, s_dim=s_dim),
        grid_spec=pltpu.PrefetchScalarGridSpec(
            num_scalar_prefetch=1,
            grid=(s_dim // tb,),
            in_specs=[row, pl.BlockSpec((tb, LANES), lambda i, de: (i, 0)), pl.BlockSpec(memory_space=pl.ANY),
                      par, par],
            out_specs=[row, row],
            scratch_shapes=[pltpu.VMEM((2, 2, tb, d_dim), F32), pltpu.SemaphoreType.DMA((2,))],
        ),
        out_shape=[jax.ShapeDtypeStruct((s_dim, d_dim), F32), jax.ShapeDtypeStruct((s_dim, d_dim), BF16)],
        compiler_params=_cparams("arbitrary"),
        name="moe_combine_ln",
    )(plan["dest"], x, route, ys, g3, b3)


def _even_layer(x, xb, j, i, even_w_in, even_w_out, ffn_w_gate, ffn_w_up, ffn_w_down, ln_g3, ln_b3, tabs):
    d = D_MODEL
    p = _matmul_rows([xb], even_w_in, lambda n: (j, 0, n), EVEN_IN, name="even_in_proj")
    a = _stick_breaking(p)
    r = _retention(p, tabs["log_gamma"], tabs["ret_cos"], tabs["ret_sin"])
    h = _matmul_rows([a, r], even_w_out, lambda n: (j, 0, n), d, name="even_out_proj")
    x, xb = _deepnorm_ln(x, h, ln_g3, ln_b3, 2 * i)
    hid = _swiglu_rows(xb, ffn_w_gate, ffn_w_up, lambda n: (j, 0, n), D_FF, name="ffn_up")
    f = _matmul_rows([hid], ffn_w_down, lambda n: (j, 0, n), d, name="ffn_down")
    return _deepnorm_ln(x, f, ln_g3, ln_b3, 2 * i + 1)


def _odd_layer(x, xb, j, i, odd_w_in, odd_w_out, moe_w_router, moe_w_gate, moe_w_up, moe_w_down,
               ln_g3, ln_b3, tabs):
    d = D_MODEL
    s_dim = x.shape[0]
    w_nk = jnp.swapaxes(odd_w_in, 1, 2)
    p = _matmul_rows([xb], w_nk, lambda n: (j, n, 0), ODD_MAIN, name="odd_in_proj", w_is_nk=True)
    w_tail = jnp.pad(w_nk[j, ODD_MAIN:, :], ((0, LANES - ODD_TAIL), (0, 0)))
    tail = _matmul_rows([xb], w_tail, lambda n: (n, 0), LANES, name="odd_in_tail", w_is_nk=True)

    cos_h, sin_h = tabs["head_cos"], tabs["head_sin"]
    q = _rope_cast(p, 0, DSA_Q_W, cos_h, sin_h, half=HEAD_DIM // ROT_FRACTION // 2,
                   scale=HEAD_DIM ** -0.5 * LOG2_E)
    k = _rope_cast(p, DSA_Q_W, DSA_KV_W, cos_h, sin_h, half=HEAD_DIM // ROT_FRACTION // 2, scale=1.0)
    v = _cast(p, DSA_Q_W + DSA_KV_W, DSA_KV_W)
    qi = _idx_q_prep(p, DSA_Q_W + 2 * DSA_KV_W, tabs["idx_cos"], tabs["idx_sin"])
    ki, wi = _idx_tail_prep(tail, tabs["tail_cos"], tabs["tail_sin"])

    bias = _dsa_select(qi, ki, wi, min(IDX_TOPK_MAX, s_dim // 4))
    y = _dsa_attention(q, k, v, bias)
    h = _matmul_rows([y], odd_w_out, lambda n: (j, 0, n), d, name="odd_out_proj")
    x, xb = _deepnorm_ln(x, h, ln_g3, ln_b3, 2 * i)

    w_router = jnp.pad(moe_w_router[j], ((0, 0), (0, LANES - N_EXPERTS)))
    route, counts = _router(x, w_router)
    plan = _moe_plan(route, counts, s_dim)
    xs = _moe_dispatch(x, plan)
    hid = _moe_up(xs, moe_w_gate, moe_w_up, j, plan)
    ys = _moe_down(hid, moe_w_down, j, plan)
    return _moe_combine_ln(x, route, ys, plan, ln_g3, ln_b3, 2 * i + 1)


def kernel(x, even_w_in, even_w_out, odd_w_in, odd_w_out, ffn_w_gate, ffn_w_up, ffn_w_down, moe_w_router,
           moe_w_gate, moe_w_up, moe_w_down, ln_g, ln_b):
    batch, s_dim, d = x.shape
    ln_g3 = ln_g.reshape(2 * DEPTH, 1, d)
    ln_b3 = ln_b.reshape(2 * DEPTH, 1, d)
    ret_cos, ret_sin = _rope_tables(s_dim, LANES, RET_QK_DIM, RET_THETA)
    head_cos, head_sin = _rope_tables(s_dim, LANES, HEAD_DIM // ROT_FRACTION, ROPE_THETA)
    idx_cos, idx_sin = _rope_tables(s_dim, IDX_DIM, IDX_DIM // ROT_FRACTION, ROPE_THETA)
    tail_cos, tail_sin = _rope_tables(s_dim, IDX_DIM, IDX_DIM // ROT_FRACTION, ROPE_THETA, active=IDX_DIM)
    tabs = dict(
        log_gamma=jnp.log1p(-jnp.exp2(-5.0 - jnp.arange(RET_HEADS, dtype=F32))),
        ret_cos=ret_cos, ret_sin=ret_sin, head_cos=head_cos, head_sin=head_sin,
        idx_cos=idx_cos, idx_sin=idx_sin, tail_cos=tail_cos, tail_sin=tail_sin,
    )
    outs = []
    for b in range(batch):
        xs = x[b] if batch > 1 else x.reshape(s_dim, d)
        xb = xs.astype(BF16)
        for i in range(DEPTH):
            j = i // 2
            if i % 2 == 0:
                xs, xb = _even_layer(xs, xb, j, i, even_w_in, even_w_out, ffn_w_gate, ffn_w_up, ffn_w_down,
                                     ln_g3, ln_b3, tabs)
            else:
                xs, xb = _odd_layer(xs, xb, j, i, odd_w_in, odd_w_out, moe_w_router, moe_w_gate, moe_w_up,
                                    moe_w_down, ln_g3, ln_b3, tabs)
        outs.append(xs)
    return jnp.stack(outs, axis=0) if batch > 1 else outs[0].reshape(1, s_dim, d)
```

```python
import functools
import math

import jax
import jax.numpy as jnp
from jax import lax
from jax.experimental import pallas as pl
from jax.experimental.pallas import tpu as pltpu

F32 = jnp.float32
BF16 = jnp.bfloat16
I32 = jnp.int32

D_MODEL = 4096
DEPTH = 4
HEAD_DIM = 128
SB_HEADS = 16
SB_W = SB_HEADS * HEAD_DIM
RET_HEADS = 8
RET_QK_DIM = 128
RET_V_DIM = 256
RET_QK_W = RET_HEADS * RET_QK_DIM
RET_V_W = RET_HEADS * RET_V_DIM
RET_THETA = 10000.0
DSA_Q_HEADS = 32
DSA_KV_HEADS = 8
DSA_GROUP = DSA_Q_HEADS // DSA_KV_HEADS
DSA_Q_W = DSA_Q_HEADS * HEAD_DIM
DSA_KV_W = DSA_KV_HEADS * HEAD_DIM
IDX_HEADS = 16
IDX_DIM = 64
IDX_Q_W = IDX_HEADS * IDX_DIM
IDX_TOPK_MAX = 256
ROPE_THETA = 500000.0
ROT_FRACTION = 4
D_FF = 11008
N_EXPERTS = 8
D_FF_EXPERT = 1792
LN_EPS = 1e-5
DEEPNORM_ALPHA = (2.0 * DEPTH) ** 0.25

EVEN_IN = 3 * SB_W + 2 * RET_QK_W + 2 * RET_V_W
ODD_MAIN = DSA_Q_W + 2 * DSA_KV_W + IDX_Q_W
ODD_TAIL = IDX_DIM + IDX_HEADS

LANES = 128
V7X_VMEM_BYTES = 64 * 1024 * 1024
VMEM_LIMIT_BYTES = V7X_VMEM_BYTES * 7 // 8

INT_MIN = -2147483648
MASK_NEG = -(2.0 ** 100)
LOG2_E = math.log2(math.e)


def _cparams(*sem):
    return pltpu.CompilerParams(dimension_semantics=sem, vmem_limit_bytes=VMEM_LIMIT_BYTES)


def _sigmoid(x):
    return 1.0 / (1.0 + jnp.exp(-x))


def _dot(a, b):
    return jnp.dot(a, b, preferred_element_type=F32)


def _dot_nt(a, b):
    return lax.dot_general(a, b, (((1,), (1,)), ((), ())), preferred_element_type=F32)


def _mm_rows_kernel(*refs, w_is_nk):
    *a_refs, w_ref, o_ref = refs
    k0, acc = 0, None
    for a_ref in a_refs:
        kw = a_ref.shape[1]
        if w_is_nk:
            part = _dot_nt(a_ref[...], w_ref[:, k0:k0 + kw].astype(BF16))
        else:
            part = _dot(a_ref[...], w_ref[k0:k0 + kw, :].astype(BF16))
        acc = part if acc is None else acc + part
        k0 += kw
    o_ref[...] = acc


ROW_PANEL_MAX_ROWS = 2048
ROW_PANEL_VMEM_SHARE = 0.4
ROW_STEP_VMEM_SHARE = 0.85


def _row_panel_tiles(m_dim, k_dim, n_cols, n_weights, out_bytes):
    tm = min(ROW_PANEL_MAX_ROWS, m_dim)
    while tm * k_dim * 2 > ROW_PANEL_VMEM_SHARE * VMEM_LIMIT_BYTES and tm % 2 == 0 and m_dim % (tm // 2) == 0:
        tm //= 2
    for tn in (4 * LANES, 2 * LANES, LANES):
        step = tm * k_dim * 2 + 2 * (n_weights * k_dim * tn * 4 + tm * tn * out_bytes)
        if n_cols % tn == 0 and step <= ROW_STEP_VMEM_SHARE * VMEM_LIMIT_BYTES:
            return tm, tn
    raise ValueError(f"no row-panel tiling for K={k_dim}, N={n_cols}")


def _matmul_rows(panels, w, w_index, n_cols, *, name, w_is_nk=False):
    m_dim = panels[0].shape[0]
    k_dim = sum(a.shape[1] for a in panels)
    tm, tn = _row_panel_tiles(m_dim, k_dim, n_cols, 1, 4)
    n_lead = w.ndim - 2
    w_block = (tn, k_dim) if w_is_nk else (k_dim, tn)
    return pl.pallas_call(
        functools.partial(_mm_rows_kernel, w_is_nk=w_is_nk),
        grid=(m_dim // tm, n_cols // tn),
        in_specs=[pl.BlockSpec((tm, a.shape[1]), lambda m, n: (m, 0), pipeline_mode=pl.Buffered(1))
                  for a in panels]
        + [pl.BlockSpec((None,) * n_lead + w_block, lambda m, n: w_index(n))],
        out_specs=pl.BlockSpec((tm, tn), lambda m, n: (m, n)),
        out_shape=jax.ShapeDtypeStruct((m_dim, n_cols), F32),
        compiler_params=_cparams("parallel", "arbitrary"),
        name=name,
    )(*panels, w)


def _swiglu_rows_kernel(a_ref, wg_ref, wu_ref, o_ref):
    a = a_ref[...]
    g = _dot(a, wg_ref[...].astype(BF16))
    u = _dot(a, wu_ref[...].astype(BF16))
    o_ref[...] = (g * _sigmoid(g) * u).astype(o_ref.dtype)


def _swiglu_rows(a, wg, wu, w_index, n_cols, *, name):
    m_dim, k_dim = a.shape
    tm, tn = _row_panel_tiles(m_dim, k_dim, n_cols, 2, 2)
    n_lead = wg.ndim - 2
    w_spec = pl.BlockSpec((None,) * n_lead + (k_dim, tn), lambda m, n: w_index(n))
    return pl.pallas_call(
        _swiglu_rows_kernel,
        grid=(m_dim // tm, n_cols // tn),
        in_specs=[pl.BlockSpec((tm, k_dim), lambda m, n: (m, 0), pipeline_mode=pl.Buffered(1)), w_spec, w_spec],
        out_specs=pl.BlockSpec((tm, tn), lambda m, n: (m, n)),
        out_shape=jax.ShapeDtypeStruct((m_dim, n_cols), BF16),
        compiler_params=_cparams("parallel", "arbitrary"),
        name=name,
    )(a, wg, wu)


def _ln_kernel(x_ref, h_ref, g_ref, b_ref, o_ref, ob_ref):
    y = DEEPNORM_ALPHA * x_ref[...] + h_ref[...]
    mu = jnp.mean(y, axis=-1, keepdims=True)
    d = y - mu
    var = jnp.mean(d * d, axis=-1, keepdims=True)
    out = d * lax.rsqrt(var + LN_EPS) * g_ref[...] + b_ref[...]
    o_ref[...] = out
    ob_ref[...] = out.astype(BF16)


def _deepnorm_ln(x, h, g3, b3, idx, *, tm=256):
    s_dim, d_dim = x.shape
    tm = min(tm, s_dim)
    row = pl.BlockSpec((tm, d_dim), lambda i: (i, 0))
    par = pl.BlockSpec((None, 1, d_dim), lambda i: (idx, 0, 0))
    return pl.pallas_call(
        _ln_kernel,
        grid=(s_dim // tm,),
        in_specs=[row, row, par, par],
        out_specs=[row, row],
        out_shape=[jax.ShapeDtypeStruct((s_dim, d_dim), F32), jax.ShapeDtypeStruct((s_dim, d_dim), BF16)],
        compiler_params=_cparams("parallel"),
        name="deepnorm_ln",
    )(x, h, g3, b3)


def _rope_tables(s_dim, period, rot_dim, theta, active=LANES):
    half = rot_dim // 2
    pos = jnp.arange(s_dim, dtype=F32)
    freqs = jnp.exp(-math.log(theta) * jnp.arange(half, dtype=F32) * (2.0 / rot_dim))
    ang = pos[:, None] * freqs[None, :]
    cos, sin = jnp.cos(ang), jnp.sin(ang)
    ones = jnp.ones((s_dim, period - rot_dim), F32)
    zeros = jnp.zeros((s_dim, period - rot_dim), F32)
    cos_p = jnp.concatenate([cos, cos, ones], axis=1)
    sin_p = jnp.concatenate([-sin, sin, zeros], axis=1)
    reps = LANES // period
    cos_t, sin_t = jnp.tile(cos_p, (1, reps)), jnp.tile(sin_p, (1, reps))
    if active < LANES:
        lane = jnp.arange(LANES)[None, :]
        cos_t = jnp.where(lane < active, cos_t, 1.0)
        sin_t = jnp.where(lane < active, sin_t, 0.0)
    return cos_t, sin_t


def _rope_tile(x, cos, sin, half, period=LANES):
    if 2 * half == LANES:
        return x * cos + pltpu.roll(x, half, 1) * sin
    lane = lax.broadcasted_iota(I32, x.shape, 1)
    first = (lane & (period - 1)) < half
    partner = jnp.where(first, pltpu.roll(x, LANES - half, 1), pltpu.roll(x, half, 1))
    return x * cos + partner * sin


SB_BLOCK = 256
SB_GROUP = 2
SB_DEAD = 160.0


def _sb_kernel(q_ref, k_ref, v_ref, o_ref, kb_ref, vb_ref):
    i = pl.program_id(1)
    blk = SB_BLOCK
    span = SB_GROUP * blk

    @pl.when(i == 0)
    def _():
        kb_ref[...] = k_ref[...].astype(BF16)
        vb_ref[...] = v_ref[...].astype(BF16)

    q = (q_ref[...] * (HEAD_DIM ** -0.5 * LOG2_E)).astype(BF16)
    row = lax.broadcasted_iota(I32, (blk, blk), 0)
    col = lax.broadcasted_iota(I32, (blk, blk), 1)
    later = jnp.where(row > col, 1.0, 0.0).astype(BF16)

    def group(start, acc, run, key_end):
        z = _dot_nt(q, kb_ref[pl.ds(start, span), :])
        soft = jnp.log(1.0 + jnp.exp2(-jnp.abs(z))) * LOG2_E
        sp = jnp.maximum(z, 0.0) + soft
        log_beta = z - sp
        if key_end is not None:
            key_pos = start + lax.broadcasted_iota(I32, (blk, span), 1)
            q_pos = i * blk + lax.broadcasted_iota(I32, (blk, span), 0)
            causal = (key_pos < q_pos) & (key_pos < key_end)
            sp = jnp.where(causal, sp, 0.0)
        parts = [None] * SB_GROUP
        for c in reversed(range(SB_GROUP)):
            sp_c = sp[:, c * blk:(c + 1) * blk]
            parts[c] = _dot(sp_c.astype(BF16), later) + run
            run = run + jnp.sum(sp_c, axis=1, keepdims=True)
        w = jnp.exp2(log_beta - jnp.concatenate(parts, axis=1))
        if key_end is not None:
            w = jnp.where(causal, w, 0.0)
        acc = acc + _dot(w.astype(BF16), vb_ref[pl.ds(start, span), :])
        return acc, run

    below = jnp.maximum(i - (SB_GROUP - 1), 0)
    n_full = below // SB_GROUP
    n_left = below - n_full * SB_GROUP
    acc0 = jnp.zeros((blk, HEAD_DIM), F32)
    run0 = jnp.zeros((blk, 1), F32)
    acc, run = group(pl.multiple_of(below * blk, blk), acc0, run0, (i + 1) * blk)

    def live(run):
        return jnp.min(run) < SB_DEAD

    def cond(carry):
        return jnp.logical_and(carry[0] < n_full, live(carry[2]))

    def body(carry):
        t, acc, run = carry
        start = pl.multiple_of((below - SB_GROUP * (t + 1)) * blk, blk)
        acc, run = group(start, acc, run, None)
        return t + 1, acc, run

    t, acc, run = lax.while_loop(cond, body, (jnp.int32(0), acc, run))
    tail = jnp.logical_and(jnp.logical_and(t == n_full, n_left > 0), live(run))
    acc, run = lax.cond(tail, lambda: group(0, acc, run, n_left * blk), lambda: (acc, run))
    o_ref[...] = acc.astype(o_ref.dtype)


def _stick_breaking(p):
    s_dim = p.shape[0]
    blk = SB_BLOCK
    kv = lambda off: pl.BlockSpec((s_dim, HEAD_DIM), lambda h, i: (0, off + h))
    return pl.pallas_call(
        _sb_kernel,
        grid=(SB_HEADS, s_dim // blk),
        in_specs=[pl.BlockSpec((blk, HEAD_DIM), lambda h, i: (i, h)), kv(SB_HEADS), kv(2 * SB_HEADS)],
        out_specs=pl.BlockSpec((blk, HEAD_DIM), lambda h, i: (i, h)),
        out_shape=jax.ShapeDtypeStruct((s_dim, SB_W), BF16),
        scratch_shapes=[pltpu.VMEM((s_dim, HEAD_DIM), BF16), pltpu.VMEM((s_dim, HEAD_DIM), BF16)],
        compiler_params=_cparams("parallel", "arbitrary"),
        name="stick_breaking",
    )(p, p, p)


RET_CHUNK = 512


def _ret_kernel(lg_ref, q_ref, k_ref, v_ref, g_ref, cos_ref, sin_ref, o_ref, state):
    h = pl.program_id(0)
    c = pl.program_id(1)
    n = RET_CHUNK

    @pl.when(c == 0)
    def _():
        state[...] = jnp.zeros_like(state)

    lg = lg_ref[h]
    cos, sin = cos_ref[...], sin_ref[...]
    q = _rope_tile(q_ref[...], cos, sin, RET_QK_DIM // 2)
    k = _rope_tile(k_ref[...], cos, sin, RET_QK_DIM // 2) * (RET_QK_DIM ** -0.5)
    v = v_ref[...].astype(BF16)

    ii = lax.broadcasted_iota(I32, (n, n), 0)
    jj = lax.broadcasted_iota(I32, (n, n), 1)
    rel = (ii - jj).astype(F32)
    decay = jnp.where(rel >= 0.0, jnp.exp(lg * jnp.maximum(rel, 0.0)), 0.0)
    inner = _dot_nt(q.astype(BF16), k.astype(BF16)) * decay
    out = _dot(inner.astype(BF16), v)

    pos = lax.broadcasted_iota(I32, (n, 1), 0).astype(F32)
    q_decay = jnp.exp(lg * (pos + 1.0))
    k_decay = jnp.exp(lg * (n - 1.0 - pos))
    prev = state[...]
    out = out + _dot((q * q_decay).astype(BF16), prev.astype(BF16))
    kd_t = jnp.transpose(k * k_decay).astype(BF16)
    state[...] = jnp.exp(lg * jnp.full((1, 1), n, F32)) * prev + _dot(kd_t, v)

    mu = jnp.mean(out, axis=-1, keepdims=True)
    d = out - mu
    var = jnp.mean(d * d, axis=-1, keepdims=True)
    g = g_ref[...]
    o_ref[...] = (g * _sigmoid(g) * (d * lax.rsqrt(var + LN_EPS))).astype(o_ref.dtype)


def _retention(p, log_gamma, cos, sin):
    s_dim = p.shape[0]
    n = RET_CHUNK
    qk = lambda off: pl.BlockSpec((n, RET_QK_DIM), lambda h, c, lg: (c, off + h))
    vg = lambda off: pl.BlockSpec((n, RET_V_DIM), lambda h, c, lg: (c, off + h))
    tab = pl.BlockSpec((n, LANES), lambda h, c, lg: (c, 0))
    q_off = 3 * SB_W // RET_QK_DIM
    v_off = (3 * SB_W + 2 * RET_QK_W) // RET_V_DIM
    grid_spec = pltpu.PrefetchScalarGridSpec(
        num_scalar_prefetch=1,
        grid=(RET_HEADS, s_dim // n),
        in_specs=[qk(q_off), qk(q_off + RET_HEADS), vg(v_off), vg(v_off + RET_HEADS), tab, tab],
        out_specs=pl.BlockSpec((n, RET_V_DIM), lambda h, c, lg: (c, h)),
        scratch_shapes=[pltpu.VMEM((RET_QK_DIM, RET_V_DIM), F32)],
    )
    return pl.pallas_call(
        _ret_kernel,
        grid_spec=grid_spec,
        out_shape=jax.ShapeDtypeStruct((s_dim, RET_V_W), BF16),
        compiler_params=_cparams("parallel", "arbitrary"),
        name="retention",
    )(log_gamma, p, p, p, p, cos, sin)


def _rope_cast_kernel(x_ref, cos_ref, sin_ref, o_ref, *, half, scale):
    cos, sin = cos_ref[...], sin_ref[...]
    for t in range(x_ref.shape[1] // LANES):
        sl = slice(t * LANES, (t + 1) * LANES)
        o_ref[:, sl] = (_rope_tile(x_ref[:, sl], cos, sin, half) * scale).astype(o_ref.dtype)


def _rope_cast(p, col0, n_cols, cos, sin, *, half, scale, tm=512, tn=1024):
    s_dim = p.shape[0]
    tm = min(tm, s_dim)
    tn = min(tn, n_cols)
    tab = pl.BlockSpec((tm, LANES), lambda i, j: (i, 0))
    return pl.pallas_call(
        functools.partial(_rope_cast_kernel, half=half, scale=scale),
        grid=(s_dim // tm, n_cols // tn),
        in_specs=[pl.BlockSpec((tm, tn), lambda i, j: (i, col0 // tn + j)), tab, tab],
        out_specs=pl.BlockSpec((tm, tn), lambda i, j: (i, j)),
        out_shape=jax.ShapeDtypeStruct((s_dim, n_cols), BF16),
        compiler_params=_cparams("parallel", "parallel"),
        name="rope_cast",
    )(p, cos, sin)


def _cast_kernel(x_ref, o_ref):
    o_ref[...] = x_ref[...].astype(o_ref.dtype)


def _cast(p, col0, n_cols, *, tm=512, tn=1024):
    s_dim = p.shape[0]
    tm = min(tm, s_dim)
    return pl.pallas_call(
        _cast_kernel,
        grid=(s_dim // tm, n_cols // tn),
        in_specs=[pl.BlockSpec((tm, tn), lambda i, j: (i, col0 // tn + j))],
        out_specs=pl.BlockSpec((tm, tn), lambda i, j: (i, j)),
        out_shape=jax.ShapeDtypeStruct((s_dim, n_cols), BF16),
        compiler_params=_cparams("parallel", "parallel"),
        name="cast_bf16",
    )(p)


def _idx_q_kernel(x_ref, cos_ref, sin_ref, o_ref):
    cos, sin = cos_ref[...], sin_ref[...]
    lane = lax.broadcasted_iota(I32, cos.shape, 1)
    low = lane < IDX_DIM
    for t in range(x_ref.shape[1] // LANES):
        y = _rope_tile(x_ref[:, t * LANES:(t + 1) * LANES], cos, sin, IDX_DIM // ROT_FRACTION // 2, IDX_DIM)
        o_ref[:, (2 * t) * LANES:(2 * t + 1) * LANES] = jnp.where(low, y, 0.0).astype(o_ref.dtype)
        o_ref[:, (2 * t + 1) * LANES:(2 * t + 2) * LANES] = jnp.where(
            low, pltpu.roll(y, IDX_DIM, 1), 0.0).astype(o_ref.dtype)


def _idx_q_prep(p, col0, cos, sin, *, tm=512):
    s_dim = p.shape[0]
    tm = min(tm, s_dim)
    tab = pl.BlockSpec((tm, LANES), lambda i: (i, 0))
    return pl.pallas_call(
        _idx_q_kernel,
        grid=(s_dim // tm,),
        in_specs=[pl.BlockSpec((tm, IDX_Q_W), lambda i: (i, col0 // IDX_Q_W)), tab, tab],
        out_specs=pl.BlockSpec((tm, IDX_HEADS * LANES), lambda i: (i, 0)),
        out_shape=jax.ShapeDtypeStruct((s_dim, IDX_HEADS * LANES), BF16),
        compiler_params=_cparams("parallel"),
        name="idx_q_prep",
    )(p, cos, sin)


def _idx_tail_kernel(x_ref, cos_ref, sin_ref, k_ref, w_ref):
    x = x_ref[...]
    lane = lax.broadcasted_iota(I32, x.shape, 1)
    y = _rope_tile(x, cos_ref[...], sin_ref[...], IDX_DIM // ROT_FRACTION // 2, IDX_DIM) * (IDX_DIM ** -0.5)
    k_ref[...] = jnp.where(lane < IDX_DIM, y, 0.0).astype(k_ref.dtype)
    w_ref[...] = x * (IDX_HEADS ** -0.5)


def _idx_tail_prep(tail, cos, sin, *, tm=512):
    s_dim = tail.shape[0]
    tm = min(tm, s_dim)
    blk = pl.BlockSpec((tm, LANES), lambda i: (i, 0))
    return pl.pallas_call(
        _idx_tail_kernel,
        grid=(s_dim // tm,),
        in_specs=[blk, blk, blk],
        out_specs=[blk, blk],
        out_shape=[jax.ShapeDtypeStruct((s_dim, LANES), BF16), jax.ShapeDtypeStruct((s_dim, LANES), F32)],
        compiler_params=_cparams("parallel"),
        name="idx_tail_prep",
    )(tail, cos, sin)


DSA_QB = 128
DSA_KB = 512
DSA_AQB = 512


def _select_kernel(qi_ref, ki_ref, w_ref, o_ref, keys_ref, *, topk, n_kblocks, idx_bits):
    i = pl.program_id(0)
    qb, kb = DSA_QB, DSA_KB
    n_live = ((i + 1) * qb + kb - 1) // kb
    q_pos = i * qb + lax.broadcasted_iota(I32, (qb, kb), 0)
    col0 = lax.broadcasted_iota(I32, (qb, kb), 1)
    w = w_ref[...]

    def score_block(jb, _):
        start = pl.multiple_of(jb * kb, kb)
        kj = ki_ref[pl.ds(start, kb), :]
        acc = jnp.zeros((qb, kb), F32)
        for h in range(IDX_HEADS):
            z = _dot_nt(qi_ref[:, h * LANES:(h + 1) * LANES], kj)
            acc = acc + jnp.maximum(z, 0.0) * w[:, IDX_DIM + h:IDX_DIM + h + 1]
        acc = jnp.where(acc == 0.0, 0.0, acc)
        bits = lax.bitcast_convert_type(acc, I32)
        key = jnp.where(bits >= 0, bits, bits ^ 0x7FFFFFFF)
        keys_ref[jb] = jnp.where(jb * kb + col0 <= q_pos, key, INT_MIN)
        return 0

    lax.fori_loop(0, n_live, score_block, 0)

    def count(pred):
        def body(jb, part):
            kk = keys_ref[jb]
            hit = jnp.where(pred(kk, jb * kb + col0), 1.0, 0.0)
            for t in range(kb // LANES):
                part = part + hit[:, t * LANES:(t + 1) * LANES]
            return part
        part = lax.fori_loop(0, n_live, body, jnp.zeros((qb, LANES), F32))
        return jnp.sum(part, axis=1, keepdims=True)

    k_f = float(topk)
    c_nonneg = count(lambda kk, cc: kk >= 0)
    thr = jnp.where(c_nonneg >= k_f, jnp.int32(0), jnp.int32(INT_MIN))
    c_thr = jnp.where(c_nonneg >= k_f, c_nonneg, float(n_kblocks * kb))
    short = i * qb + lax.broadcasted_iota(I32, (qb, 1), 0) + 1 < topk

    def unsettled(c_thr):
        return jnp.max(jnp.where((c_thr == k_f) | short, 0.0, 1.0)) > 0.0

    def thr_cond(carry):
        return jnp.logical_and(carry[0] < 31, unsettled(carry[2]))

    def thr_bit(carry):
        b, thr, c_thr = carry
        cand = thr + jnp.left_shift(jnp.int32(1), 30 - b)
        c = count(lambda kk, cc: kk >= cand)
        ok = c >= k_f
        return b + 1, jnp.where(ok, cand, thr), jnp.where(ok, c, c_thr)

    _, thr, c_thr = lax.while_loop(thr_cond, thr_bit, (jnp.int32(0), thr, c_thr))

    def no_cut():
        return jnp.full((qb, 1), n_kblocks * kb, I32)

    def tie_cut():
        need = k_f - count(lambda kk, cc: kk > thr)
        n_eq = count(lambda kk, cc: kk == thr)
        tied = (n_eq > need) & (thr != INT_MIN)

        def cut_bit(b, cut):
            cand = cut + jnp.left_shift(jnp.int32(1), idx_bits - 1 - b)
            c = count(lambda kk, cc: (kk == thr) & (cc < cand))
            return jnp.where(c < need, cand, cut)

        def cut_search():
            return lax.fori_loop(0, idx_bits, cut_bit, jnp.zeros((qb, 1), I32))

        return lax.cond(jnp.max(jnp.where(tied, 1.0, 0.0)) > 0.0, cut_search, no_cut)

    cut = lax.cond(unsettled(c_thr), tie_cut, no_cut)

    def write_live(jb, _):
        kk = keys_ref[jb]
        cc = jb * kb + col0
        sel = ((kk > thr) | ((kk == thr) & (cc <= cut))) & (kk != INT_MIN)
        o_ref[jb] = jnp.where(sel, 0.0, MASK_NEG).astype(o_ref.dtype)
        return 0

    lax.fori_loop(0, n_live, write_live, 0)

    def write_dead(jb, _):
        o_ref[jb] = jnp.full((qb, kb), MASK_NEG, o_ref.dtype)
        return 0

    lax.fori_loop(n_live, n_kblocks, write_dead, 0)


def _dsa_select(qi, ki, w, topk):
    s_dim = qi.shape[0]
    qb, kb = DSA_QB, DSA_KB
    n_kblocks = s_dim // kb
    idx_bits = max(1, (s_dim - 1).bit_length())
    return pl.pallas_call(
        functools.partial(_select_kernel, topk=topk, n_kblocks=n_kblocks, idx_bits=idx_bits),
        grid=(s_dim // qb,),
        in_specs=[
            pl.BlockSpec((qb, IDX_HEADS * LANES), lambda i: (i, 0)),
            pl.BlockSpec((s_dim, LANES), lambda i: (0, 0)),
            pl.BlockSpec((qb, LANES), lambda i: (i, 0)),
        ],
        out_specs=pl.BlockSpec((None, n_kblocks, qb, kb), lambda i: (i, 0, 0, 0)),
        out_shape=jax.ShapeDtypeStruct((s_dim // qb, n_kblocks, qb, kb), BF16),
        scratch_shapes=[pltpu.VMEM((n_kblocks, qb, kb), I32)],
        compiler_params=_cparams("parallel"),
        name="dsa_select",
    )(qi, ki, w)


def _dsa_attn_kernel(q_ref, k_ref, v_ref, b_ref, o_ref):
    i = pl.program_id(1)
    qb, kb, grp = DSA_AQB, DSA_KB, DSA_GROUP
    n_sel = qb // DSA_QB
    n_live = ((i + 1) * qb + kb - 1) // kb
    q = jnp.concatenate([q_ref[:, r * HEAD_DIM:(r + 1) * HEAD_DIM] for r in range(grp)], axis=0)

    def body(jb, carry):
        m, l, acc = carry
        start = pl.multiple_of(jb * kb, kb)
        kj = k_ref[pl.ds(start, kb), :]
        vj = v_ref[pl.ds(start, kb), :]
        z = _dot_nt(q, kj).astype(BF16)
        bias = b_ref[:, jb].reshape(1, qb, kb)
        z = (z.reshape(grp, qb, kb) + bias).reshape(grp * qb, kb)
        m_new = jnp.maximum(m, jnp.max(z, axis=1, keepdims=True).astype(F32))
        p = jnp.exp2(z - m_new.astype(BF16))
        alpha = jnp.exp2(m - m_new)
        l = alpha * l + jnp.sum(p.astype(F32), axis=1, keepdims=True)
        acc = alpha * acc + _dot(p, vj)
        return m_new, l, acc

    rows = grp * qb
    init = (jnp.full((rows, 1), MASK_NEG, F32), jnp.zeros((rows, 1), F32), jnp.zeros((rows, HEAD_DIM), F32))
    _, l, acc = lax.fori_loop(0, n_live, body, init)
    out = acc / l
    for r in range(grp):
        o_ref[:, r * HEAD_DIM:(r + 1) * HEAD_DIM] = out[r * qb:(r + 1) * qb, :].astype(o_ref.dtype)


def _dsa_attention(q, k, v, bias):
    s_dim = q.shape[0]
    qb, kb = DSA_AQB, DSA_KB
    gw = DSA_GROUP * HEAD_DIM
    kv = pl.BlockSpec((s_dim, HEAD_DIM), lambda g, i: (0, g))
    return pl.pallas_call(
        _dsa_attn_kernel,
        grid=(DSA_KV_HEADS, s_dim // qb),
        in_specs=[
            pl.BlockSpec((qb, gw), lambda g, i: (i, g)),
            kv, kv,
            pl.BlockSpec((qb // DSA_QB, s_dim // kb, DSA_QB, kb), lambda g, i: (i, 0, 0, 0)),
        ],
        out_specs=pl.BlockSpec((qb, gw), lambda g, i: (i, g)),
        out_shape=jax.ShapeDtypeStruct((s_dim, DSA_Q_W), BF16),
        compiler_params=_cparams("parallel", "arbitrary"),
        name="dsa_attention",
    )(q, k, v, bias)


ROUTE_E1, ROUTE_E2, ROUTE_G1, ROUTE_G2, ROUTE_R1, ROUTE_R2 = range(6)


def _router_kernel(x_ref, w_ref, o_ref, cnt_ref, seen):
    @pl.when(pl.program_id(0) == 0)
    def _():
        seen[...] = jnp.zeros_like(seen)

    x = x_ref[...]
    w = w_ref[...]
    xh = x.astype(BF16)
    xl = (x - xh.astype(F32)).astype(BF16)
    wh = w.astype(BF16)
    wl = (w - wh.astype(F32)).astype(BF16)
    logits = _dot(xh, wh) + (_dot(xh, wl) + _dot(xl, wh))
    lane = lax.broadcasted_iota(I32, logits.shape, 1).astype(F32)
    logits = jnp.where(lane < N_EXPERTS, logits, -jnp.inf)
    v1 = jnp.max(logits, axis=1, keepdims=True)
    i1 = jnp.min(jnp.where(logits == v1, lane, float(LANES)), axis=1, keepdims=True)
    rest = jnp.where(lane == i1, -jnp.inf, logits)
    v2 = jnp.max(rest, axis=1, keepdims=True)
    i2 = jnp.min(jnp.where(rest == v2, lane, float(LANES)), axis=1, keepdims=True)
    e2 = jnp.exp(v2 - v1)
    g1 = 1.0 / (1.0 + e2)
    g2 = e2 / (1.0 + e2)

    tb = x.shape[0]
    member = jnp.where((lane == i1) | (lane == i2), 1.0, 0.0)
    earlier = jnp.where(lax.broadcasted_iota(I32, (tb, tb), 1) < lax.broadcasted_iota(I32, (tb, tb), 0), 1.0, 0.0)
    prefix = _dot(earlier.astype(BF16), member.astype(BF16)) + seen[0:1, :]
    r1 = jnp.sum(jnp.where(lane == i1, prefix, 0.0), axis=1, keepdims=True)
    r2 = jnp.sum(jnp.where(lane == i2, prefix, 0.0), axis=1, keepdims=True)
    seen[...] = seen[...] + jnp.sum(member, axis=0, keepdims=True)
    cnt_ref[...] = seen[...]

    rec = jnp.zeros_like(logits)
    for slot, val in ((ROUTE_E1, i1), (ROUTE_E2, i2), (ROUTE_G1, g1), (ROUTE_G2, g2), (ROUTE_R1, r1), (ROUTE_R2, r2)):
        rec = jnp.where(lane == float(slot), val, rec)
    o_ref[...] = rec


def _router(x, w_pad, *, tm=256):
    s_dim, d_dim = x.shape
    tm = min(tm, s_dim)
    return pl.pallas_call(
        _router_kernel,
        grid=(s_dim // tm,),
        in_specs=[pl.BlockSpec((tm, d_dim), lambda i: (i, 0)), pl.BlockSpec((d_dim, LANES), lambda i: (0, 0))],
        out_specs=[pl.BlockSpec((tm, LANES), lambda i: (i, 0)), pl.BlockSpec((8, LANES), lambda i: (0, 0))],
        out_shape=[jax.ShapeDtypeStruct((s_dim, LANES), F32), jax.ShapeDtypeStruct((8, LANES), F32)],
        scratch_shapes=[pltpu.VMEM((8, LANES), F32)],
        compiler_params=_cparams("arbitrary"),
        name="moe_router",
    )(x, w_pad)


MOE_TM = 512


def _moe_plan(route, counts, s_dim):
    tm = min(MOE_TM, s_dim)
    n_rows = 2 * s_dim + N_EXPERTS * tm
    e1 = route[:, ROUTE_E1].astype(I32)
    e2 = route[:, ROUTE_E2].astype(I32)
    cnt = counts[0, :N_EXPERTS].astype(I32)
    padded = (cnt + tm - 1) // tm * tm
    ends = jnp.cumsum(padded)
    starts = ends - padded
    dest1 = starts[e1] + route[:, ROUTE_R1].astype(I32)
    dest2 = starts[e2] + route[:, ROUTE_R2].astype(I32)
    tok = jnp.arange(s_dim, dtype=I32)
    src_tok = jnp.zeros((n_rows,), I32).at[dest1].set(tok).at[dest2].set(tok)
    tile_start = jnp.arange(n_rows // tm, dtype=I32) * tm
    tile_expert = jnp.minimum(jnp.searchsorted(ends, tile_start, side="right"), N_EXPERTS - 1).astype(I32)
    n_used = (ends[-1:] // tm).astype(I32)
    return dict(tm=tm, n_rows=n_rows, dest=jnp.concatenate([dest1, dest2]), src_tok=src_tok,
                tile_expert=tile_expert, n_used=n_used)


def _row_copy(src_ref, src_row, dst_ref, dst_row, sem):
    return pltpu.make_async_copy(src_ref.at[pl.ds(src_row, 1), :], dst_ref.at[pl.ds(dst_row, 1), :], sem)


def _dispatch_kernel(src_tok_ref, x_ref, o_ref, buf, sem):
    i = pl.program_id(0)
    n = pl.num_programs(0)
    tm = buf.shape[1]

    def fetch(tile, slot):
        def issue(r, _):
            _row_copy(x_ref, src_tok_ref[tile * tm + r], buf.at[slot], r, sem.at[slot]).start()
            return 0
        lax.fori_loop(0, tm, issue, 0)

    @pl.when(i == 0)
    def _():
        fetch(0, 0)

    @pl.when(i + 1 < n)
    def _():
        fetch(i + 1, (i + 1) % 2)

    slot = i % 2
    pltpu.make_async_copy(x_ref.at[pl.ds(0, tm), :], buf.at[slot], sem.at[slot]).wait()
    o_ref[...] = buf[slot].astype(o_ref.dtype)


def _moe_dispatch(x, plan, *, tm=256):
    s_dim, d_dim = x.shape
    tm = min(tm, s_dim)
    n_rows = plan["n_rows"]
    return pl.pallas_call(
        _dispatch_kernel,
        grid_spec=pltpu.PrefetchScalarGridSpec(
            num_scalar_prefetch=1,
            grid=(n_rows // tm,),
            in_specs=[pl.BlockSpec(memory_space=pl.ANY)],
            out_specs=pl.BlockSpec((tm, d_dim), lambda i, st: (i, 0)),
            scratch_shapes=[pltpu.VMEM((2, tm, d_dim), F32), pltpu.SemaphoreType.DMA((2,))],
        ),
        out_shape=jax.ShapeDtypeStruct((n_rows, d_dim), BF16),
        compiler_params=_cparams("arbitrary"),
        name="moe_dispatch",
    )(plan["src_tok"], x)


def _moe_up_kernel(te_ref, nu_ref, a_ref, wg_ref, wu_ref, o_ref):
    m = pl.program_id(1)

    @pl.when(m < nu_ref[0])
    def _():
        a = a_ref[...]
        g = _dot(a, wg_ref[...].astype(BF16))
        u = _dot(a, wu_ref[...].astype(BF16))
        o_ref[...] = (g * _sigmoid(g) * u).astype(o_ref.dtype)

    @pl.when(m >= nu_ref[0])
    def _():
        o_ref[...] = jnp.zeros_like(o_ref)


def _moe_up(xs, wg, wu, layer, plan, *, tn=256):
    n_rows, d_dim = xs.shape
    tm = plan["tm"]
    w_spec = pl.BlockSpec((None, None, d_dim, tn), lambda n, m, te, nu: (layer, te[m], 0, n))
    return pl.pallas_call(
        _moe_up_kernel,
        grid_spec=pltpu.PrefetchScalarGridSpec(
            num_scalar_prefetch=2,
            grid=(D_FF_EXPERT // tn, n_rows // tm),
            in_specs=[pl.BlockSpec((tm, d_dim), lambda n, m, te, nu: (m, 0)), w_spec, w_spec],
            out_specs=pl.BlockSpec((tm, tn), lambda n, m, te, nu: (m, n)),
        ),
        out_shape=jax.ShapeDtypeStruct((n_rows, D_FF_EXPERT), BF16),
        compiler_params=_cparams("parallel", "arbitrary"),
        name="moe_up",
    )(plan["tile_expert"], plan["n_used"], xs, wg, wu)


def _moe_down_kernel(te_ref, nu_ref, a_ref, w_ref, o_ref):
    m = pl.program_id(1)

    @pl.when(m < nu_ref[0])
    def _():
        o_ref[...] = _dot(a_ref[...], w_ref[...].astype(BF16))

    @pl.when(m >= nu_ref[0])
    def _():
        o_ref[...] = jnp.zeros_like(o_ref)


def _moe_down(hid, wd, layer, plan, *, tn=2048):
    n_rows, f_dim = hid.shape
    tm = plan["tm"]
    d_dim = wd.shape[-1]
    return pl.pallas_call(
        _moe_down_kernel,
        grid_spec=pltpu.PrefetchScalarGridSpec(
            num_scalar_prefetch=2,
            grid=(d_dim // tn, n_rows // tm),
            in_specs=[pl.BlockSpec((tm, f_dim), lambda n, m, te, nu: (m, 0)),
                      pl.BlockSpec((None, None, f_dim, tn), lambda n, m, te, nu: (layer, te[m], 0, n))],
            out_specs=pl.BlockSpec((tm, tn), lambda n, m, te, nu: (m, n)),
        ),
        out_shape=jax.ShapeDtypeStruct((n_rows, d_dim), F32),
        compiler_params=_cparams("parallel", "arbitrary"),
        name="moe_down",
    )(plan["tile_expert"], plan["n_used"], hid, wd)


def _combine_ln_kernel(dest_ref, x_ref, r_ref, ys_ref, g_ref, b_ref, o_ref, ob_ref, buf, sem, *, s_dim):
    i = pl.program_id(0)
    n = pl.num_programs(0)
    tb = x_ref.shape[0]

    def fetch(blk, slot):
        def issue(r, _):
            t = blk * tb + r
            _row_copy(ys_ref, dest_ref[t], buf.at[slot, 0], r, sem.at[slot]).start()
            _row_copy(ys_ref, dest_ref[s_dim + t], buf.at[slot, 1], r, sem.at[slot]).start()
            return 0
        lax.fori_loop(0, tb, issue, 0)

    @pl.when(i == 0)
    def _():
        fetch(0, 0)

    @pl.when(i + 1 < n)
    def _():
        fetch(i + 1, (i + 1) % 2)

    slot = i % 2
    for k in range(2):
        pltpu.make_async_copy(ys_ref.at[pl.ds(0, tb), :], buf.at[slot, k], sem.at[slot]).wait()
    rec = r_ref[...]
    f = rec[:, ROUTE_G1:ROUTE_G1 + 1] * buf[slot, 0] + rec[:, ROUTE_G2:ROUTE_G2 + 1] * buf[slot, 1]
    y = DEEPNORM_ALPHA * x_ref[...] + f
    mu = jnp.mean(y, axis=-1, keepdims=True)
    d = y - mu
    var = jnp.mean(d * d, axis=-1, keepdims=True)
    out = d * lax.rsqrt(var + LN_EPS) * g_ref[...] + b_ref[...]
    o_ref[...] = out
    ob_ref[...] = out.astype(BF16)


def _moe_combine_ln(x, route, ys, plan, g3, b3, idx, *, tb=256):
    s_dim, d_dim = x.shape
    tb = min(tb, s_dim)
    row = pl.BlockSpec((tb, d_dim), lambda i, de: (i, 0))
    par = pl.BlockSpec((None, 1, d_dim), lambda i, de: (idx, 0, 0))
    return pl.pallas_call(
        functools.partial(_combine_ln_kernel, s_dim=s_dim),
        grid_spec=pltpu.PrefetchScalarGridSpec(
            num_scalar_prefetch=1,
            grid=(s_dim // tb,),
            in_specs=[row, pl.BlockSpec((tb, LANES), lambda i, de: (i, 0)), pl.BlockSpec(memory_space=pl.ANY),
                      par, par],
            out_specs=[row, row],
            scratch_shapes=[pltpu.VMEM((2, 2, tb, d_dim), F32), pltpu.SemaphoreType.DMA((2,))],
        ),
        out_shape=[jax.ShapeDtypeStruct((s_dim, d_dim), F32), jax.ShapeDtypeStruct((s_dim, d_dim), BF16)],
        compiler_params=_cparams("arbitrary"),
        name="moe_combine_ln",
    )(plan["dest"], x, route, ys, g3, b3)


def _even_layer(x, xb, j, i, even_w_in, even_w_out, ffn_w_gate, ffn_w_up, ffn_w_down, ln_g3, ln_b3, tabs):
    d = D_MODEL
    p = _matmul_rows([xb], even_w_in, lambda n: (j, 0, n), EVEN_IN, name="even_in_proj")
    a = _stick_breaking(p)
    r = _retention(p, tabs["log_gamma"], tabs["ret_cos"], tabs["ret_sin"])
    h = _matmul_rows([a, r], even_w_out, lambda n: (j, 0, n), d, name="even_out_proj")
    x, xb = _deepnorm_ln(x, h, ln_g3, ln_b3, 2 * i)
    hid = _swiglu_rows(xb, ffn_w_gate, ffn_w_up, lambda n: (j, 0, n), D_FF, name="ffn_up")
    f = _matmul_rows([hid], ffn_w_down, lambda n: (j, 0, n), d, name="ffn_down")
    return _deepnorm_ln(x, f, ln_g3, ln_b3, 2 * i + 1)


def _odd_layer(x, xb, j, i, odd_w_in, odd_w_out, moe_w_router, moe_w_gate, moe_w_up, moe_w_down,
               ln_g3, ln_b3, tabs):
    d = D_MODEL
    s_dim = x.shape[0]
    w_nk = jnp.swapaxes(odd_w_in, 1, 2)
    p = _matmul_rows([xb], w_nk, lambda n: (j, n, 0), ODD_MAIN, name="odd_in_proj", w_is_nk=True)
    w_tail = jnp.pad(w_nk[j, ODD_MAIN:, :], ((0, LANES - ODD_TAIL), (0, 0)))
    tail = _matmul_rows([xb], w_tail, lambda n: (n, 0), LANES, name="odd_in_tail", w_is_nk=True)

    cos_h, sin_h = tabs["head_cos"], tabs["head_sin"]
    q = _rope_cast(p, 0, DSA_Q_W, cos_h, sin_h, half=HEAD_DIM // ROT_FRACTION // 2,
                   scale=HEAD_DIM ** -0.5 * LOG2_E)
    k = _rope_cast(p, DSA_Q_W, DSA_KV_W, cos_h, sin_h, half=HEAD_DIM // ROT_FRACTION // 2, scale=1.0)
    v = _cast(p, DSA_Q_W + DSA_KV_W, DSA_KV_W)
    qi = _idx_q_prep(p, DSA_Q_W + 2 * DSA_KV_W, tabs["idx_cos"], tabs["idx_sin"])
    ki, wi = _idx_tail_prep(tail, tabs["tail_cos"], tabs["tail_sin"])

    bias = _dsa_select(qi, ki, wi, min(IDX_TOPK_MAX, s_dim // 4))
    y = _dsa_attention(q, k, v, bias)
    h = _matmul_rows([y], odd_w_out, lambda n: (j, 0, n), d, name="odd_out_proj")
    x, xb = _deepnorm_ln(x, h, ln_g3, ln_b3, 2 * i)

    w_router = jnp.pad(moe_w_router[j], ((0, 0), (0, LANES - N_EXPERTS)))
    route, counts = _router(x, w_router)
    plan = _moe_plan(route, counts, s_dim)
    xs = _moe_dispatch(x, plan)
    hid = _moe_up(xs, moe_w_gate, moe_w_up, j, plan)
    ys = _moe_down(hid, moe_w_down, j, plan)
    return _moe_combine_ln(x, route, ys, plan, ln_g3, ln_b3, 2 * i + 1)


def kernel(x, even_w_in, even_w_out, odd_w_in, odd_w_out, ffn_w_gate, ffn_w_up, ffn_w_down, moe_w_router,
           moe_w_gate, moe_w_up, moe_w_down, ln_g, ln_b):
    batch, s_dim, d = x.shape
    ln_g3 = ln_g.reshape(2 * DEPTH, 1, d)
    ln_b3 = ln_b.reshape(2 * DEPTH, 1, d)
    ret_cos, ret_sin = _rope_tables(s_dim, LANES, RET_QK_DIM, RET_THETA)
    head_cos, head_sin = _rope_tables(s_dim, LANES, HEAD_DIM // ROT_FRACTION, ROPE_THETA)
    idx_cos, idx_sin = _rope_tables(s_dim, IDX_DIM, IDX_DIM // ROT_FRACTION, ROPE_THETA)
    tail_cos, tail_sin = _rope_tables(s_dim, IDX_DIM, IDX_DIM // ROT_FRACTION, ROPE_THETA, active=IDX_DIM)
    tabs = dict(
        log_gamma=jnp.log1p(-jnp.exp2(-5.0 - jnp.arange(RET_HEADS, dtype=F32))),
        ret_cos=ret_cos, ret_sin=ret_sin, head_cos=head_cos, head_sin=head_sin,
        idx_cos=idx_cos, idx_sin=idx_sin, tail_cos=tail_cos, tail_sin=tail_sin,
    )
    outs = []
    for b in range(batch):
        xs = x[b] if batch > 1 else x.reshape(s_dim, d)
        xb = xs.astype(BF16)
        for i in range(DEPTH):
            j = i // 2
            if i % 2 == 0:
                xs, xb = _even_layer(xs, xb, j, i, even_w_in, even_w_out, ffn_w_gate, ffn_w_up, ffn_w_down,
                                     ln_g3, ln_b3, tabs)
            else:
                xs, xb = _odd_layer(xs, xb, j, i, odd_w_in, odd_w_out, moe_w_router, moe_w_gate, moe_w_up,
                                    moe_w_down, ln_g3, ln_b3, tabs)
        outs.append(xs)
    return jnp.stack(outs, axis=0) if batch > 1 else outs[0].reshape(1, s_dim, d)
```

```python
import functools
import math

import jax
import jax.numpy as jnp
from jax import lax
from jax.experimental import pallas as pl
from jax.experimental.pallas import tpu as pltpu

F32 = jnp.float32
BF16 = jnp.bfloat16
I32 = jnp.int32

D_MODEL = 4096
DEPTH = 4
HEAD_DIM = 128
SB_HEADS = 16
SB_W = SB_HEADS * HEAD_DIM
RET_HEADS = 8
RET_QK_DIM = 128
RET_V_DIM = 256
RET_QK_W = RET_HEADS * RET_QK_DIM
RET_V_W = RET_HEADS * RET_V_DIM
RET_THETA = 10000.0
DSA_Q_HEADS = 32
DSA_KV_HEADS = 8
DSA_GROUP = DSA_Q_HEADS // DSA_KV_HEADS
DSA_Q_W = DSA_Q_HEADS * HEAD_DIM
DSA_KV_W = DSA_KV_HEADS * HEAD_DIM
IDX_HEADS = 16
IDX_DIM = 64
IDX_Q_W = IDX_HEADS * IDX_DIM
IDX_TOPK_MAX = 256
ROPE_THETA = 500000.0
ROT_FRACTION = 4
D_FF = 11008
N_EXPERTS = 8
D_FF_EXPERT = 1792
LN_EPS = 1e-5
DEEPNORM_ALPHA = (2.0 * DEPTH) ** 0.25

EVEN_IN = 3 * SB_W + 2 * RET_QK_W + 2 * RET_V_W
ODD_MAIN = DSA_Q_W + 2 * DSA_KV_W + IDX_Q_W
ODD_TAIL = IDX_DIM + IDX_HEADS

LANES = 128
V7X_VMEM_BYTES = 64 * 1024 * 1024
VMEM_LIMIT_BYTES = V7X_VMEM_BYTES * 7 // 8

INT_MIN = -2147483648
MASK_NEG = -(2.0 ** 100)
LOG2_E = math.log2(math.e)


def _cparams(*sem):
    return pltpu.CompilerParams(dimension_semantics=sem, vmem_limit_bytes=VMEM_LIMIT_BYTES)


def _sigmoid(x):
    return 1.0 / (1.0 + jnp.exp(-x))


def _dot(a, b):
    return jnp.dot(a, b, preferred_element_type=F32)


def _dot_nt(a, b):
    return lax.dot_general(a, b, (((1,), (1,)), ((), ())), preferred_element_type=F32)


def _mm_rows_kernel(*refs, w_is_nk):
    *a_refs, w_ref, o_ref = refs
    k0, acc = 0, None
    for a_ref in a_refs:
        kw = a_ref.shape[1]
        if w_is_nk:
            part = _dot_nt(a_ref[...], w_ref[:, k0:k0 + kw].astype(BF16))
        else:
            part = _dot(a_ref[...], w_ref[k0:k0 + kw, :].astype(BF16))
        acc = part if acc is None else acc + part
        k0 += kw
    o_ref[...] = acc


ROW_PANEL_MAX_ROWS = 2048
ROW_PANEL_VMEM_SHARE = 0.4
ROW_STEP_VMEM_SHARE = 0.85


def _row_panel_tiles(m_dim, k_dim, n_cols, n_weights, out_bytes):
    tm = min(ROW_PANEL_MAX_ROWS, m_dim)
    while tm * k_dim * 2 > ROW_PANEL_VMEM_SHARE * VMEM_LIMIT_BYTES and tm % 2 == 0 and m_dim % (tm // 2) == 0:
        tm //= 2
    for tn in (4 * LANES, 2 * LANES, LANES):
        step = tm * k_dim * 2 + 2 * (n_weights * k_dim * tn * 4 + tm * tn * out_bytes)
        if n_cols % tn == 0 and step <= ROW_STEP_VMEM_SHARE * VMEM_LIMIT_BYTES:
            return tm, tn
    raise ValueError(f"no row-panel tiling for K={k_dim}, N={n_cols}")


def _matmul_rows(panels, w, w_index, n_cols, *, name, w_is_nk=False):
    m_dim = panels[0].shape[0]
    k_dim = sum(a.shape[1] for a in panels)
    tm, tn = _row_panel_tiles(m_dim, k_dim, n_cols, 1, 4)
    n_lead = w.ndim - 2
    w_block = (tn, k_dim) if w_is_nk else (k_dim, tn)
    return pl.pallas_call(
        functools.partial(_mm_rows_kernel, w_is_nk=w_is_nk),
        grid=(m_dim // tm, n_cols // tn),
        in_specs=[pl.BlockSpec((tm, a.shape[1]), lambda m, n: (m, 0), pipeline_mode=pl.Buffered(1))
                  for a in panels]
        + [pl.BlockSpec((None,) * n_lead + w_block, lambda m, n: w_index(n))],
        out_specs=pl.BlockSpec((tm, tn), lambda m, n: (m, n)),
        out_shape=jax.ShapeDtypeStruct((m_dim, n_cols), F32),
        compiler_params=_cparams("parallel", "arbitrary"),
        name=name,
    )(*panels, w)


def _swiglu_rows_kernel(a_ref, wg_ref, wu_ref, o_ref):
    a = a_ref[...]
    g = _dot(a, wg_ref[...].astype(BF16))
    u = _dot(a, wu_ref[...].astype(BF16))
    o_ref[...] = (g * _sigmoid(g) * u).astype(o_ref.dtype)


def _swiglu_rows(a, wg, wu, w_index, n_cols, *, name):
    m_dim, k_dim = a.shape
    tm, tn = _row_panel_tiles(m_dim, k_dim, n_cols, 2, 2)
    n_lead = wg.ndim - 2
    w_spec = pl.BlockSpec((None,) * n_lead + (k_dim, tn), lambda m, n: w_index(n))
    return pl.pallas_call(
        _swiglu_rows_kernel,
        grid=(m_dim // tm, n_cols // tn),
        in_specs=[pl.BlockSpec((tm, k_dim), lambda m, n: (m, 0), pipeline_mode=pl.Buffered(1)), w_spec, w_spec],
        out_specs=pl.BlockSpec((tm, tn), lambda m, n: (m, n)),
        out_shape=jax.ShapeDtypeStruct((m_dim, n_cols), BF16),
        compiler_params=_cparams("parallel", "arbitrary"),
        name=name,
    )(a, wg, wu)


def _ln_kernel(x_ref, h_ref, g_ref, b_ref, o_ref, ob_ref):
    y = DEEPNORM_ALPHA * x_ref[...] + h_ref[...]
    mu = jnp.mean(y, axis=-1, keepdims=True)
    d = y - mu
    var = jnp.mean(d * d, axis=-1, keepdims=True)
    out = d * lax.rsqrt(var + LN_EPS) * g_ref[...] + b_ref[...]
    o_ref[...] = out
    ob_ref[...] = out.astype(BF16)


def _deepnorm_ln(x, h, g3, b3, idx, *, tm=256):
    s_dim, d_dim = x.shape
    tm = min(tm, s_dim)
    row = pl.BlockSpec((tm, d_dim), lambda i: (i, 0))
    par = pl.BlockSpec((None, 1, d_dim), lambda i: (idx, 0, 0))
    return pl.pallas_call(
        _ln_kernel,
        grid=(s_dim // tm,),
        in_specs=[row, row, par, par],
        out_specs=[row, row],
        out_shape=[jax.ShapeDtypeStruct((s_dim, d_dim), F32), jax.ShapeDtypeStruct((s_dim, d_dim), BF16)],
        compiler_params=_cparams("parallel"),
        name="deepnorm_ln",
    )(x, h, g3, b3)


def _rope_tables(s_dim, period, rot_dim, theta, active=LANES):
    half = rot_dim // 2
    pos = jnp.arange(s_dim, dtype=F32)
    freqs = jnp.exp(-math.log(theta) * jnp.arange(half, dtype=F32) * (2.0 / rot_dim))
    ang = pos[:, None] * freqs[None, :]
    cos, sin = jnp.cos(ang), jnp.sin(ang)
    ones = jnp.ones((s_dim, period - rot_dim), F32)
    zeros = jnp.zeros((s_dim, period - rot_dim), F32)
    cos_p = jnp.concatenate([cos, cos, ones], axis=1)
    sin_p = jnp.concatenate([-sin, sin, zeros], axis=1)
    reps = LANES // period
    cos_t, sin_t = jnp.tile(cos_p, (1, reps)), jnp.tile(sin_p, (1, reps))
    if active < LANES:
        lane = jnp.arange(LANES)[None, :]
        cos_t = jnp.where(lane < active, cos_t, 1.0)
        sin_t = jnp.where(lane < active, sin_t, 0.0)
    return cos_t, sin_t


def _rope_tile(x, cos, sin, half, period=LANES):
    if 2 * half == LANES:
        return x * cos + pltpu.roll(x, half, 1) * sin
    lane = lax.broadcasted_iota(I32, x.shape, 1)
    first = (lane & (period - 1)) < half
    partner = jnp.where(first, pltpu.roll(x, LANES - half, 1), pltpu.roll(x, half, 1))
    return x * cos + partner * sin


SB_BLOCK = 256
SB_GROUP = 2
SB_DEAD = 160.0


def _sb_kernel(q_ref, k_ref, v_ref, o_ref, kb_ref, vb_ref):
    i = pl.program_id(1)
    blk = SB_BLOCK
    span = SB_GROUP * blk

    @pl.when(i == 0)
    def _():
        kb_ref[...] = k_ref[...].astype(BF16)
        vb_ref[...] = v_ref[...].astype(BF16)

    q = (q_ref[...] * (HEAD_DIM ** -0.5 * LOG2_E)).astype(BF16)
    row = lax.broadcasted_iota(I32, (blk, blk), 0)
    col = lax.broadcasted_iota(I32, (blk, blk), 1)
    later = jnp.where(row > col, 1.0, 0.0).astype(BF16)

    def group(start, acc, run, key_end):
        z = _dot_nt(q, kb_ref[pl.ds(start, span), :])
        soft = jnp.log(1.0 + jnp.exp2(-jnp.abs(z))) * LOG2_E
        sp = jnp.maximum(z, 0.0) + soft
        log_beta = z - sp
        if key_end is not None:
            key_pos = start + lax.broadcasted_iota(I32, (blk, span), 1)
            q_pos = i * blk + lax.broadcasted_iota(I32, (blk, span), 0)
            causal = (key_pos < q_pos) & (key_pos < key_end)
            sp = jnp.where(causal, sp, 0.0)
        parts = [None] * SB_GROUP
        for c in reversed(range(SB_GROUP)):
            sp_c = sp[:, c * blk:(c + 1) * blk]
            parts[c] = _dot(sp_c.astype(BF16), later) + run
            run = run + jnp.sum(sp_c, axis=1, keepdims=True)
        w = jnp.exp2(log_beta - jnp.concatenate(parts, axis=1))
        if key_end is not None:
            w = jnp.where(causal, w, 0.0)
        acc = acc + _dot(w.astype(BF16), vb_ref[pl.ds(start, span), :])
        return acc, run

    below = jnp.maximum(i - (SB_GROUP - 1), 0)
    n_full = below // SB_GROUP
    n_left = below - n_full * SB_GROUP
    acc0 = jnp.zeros((blk, HEAD_DIM), F32)
    run0 = jnp.zeros((blk, 1), F32)
    acc, run = group(pl.multiple_of(below * blk, blk), acc0, run0, (i + 1) * blk)

    def live(run):
        return jnp.min(run) < SB_DEAD

    def cond(carry):
        return jnp.logical_and(carry[0] < n_full, live(carry[2]))

    def body(carry):
        t, acc, run = carry
        start = pl.multiple_of((below - SB_GROUP * (t + 1)) * blk, blk)
        acc, run = group(start, acc, run, None)
        return t + 1, acc, run

    t, acc, run = lax.while_loop(cond, body, (jnp.int32(0), acc, run))
    tail = jnp.logical_and(jnp.logical_and(t == n_full, n_left > 0), live(run))
    acc, run = lax.cond(tail, lambda: group(0, acc, run, n_left * blk), lambda: (acc, run))
    o_ref[...] = acc.astype(o_ref.dtype)


def _stick_breaking(p):
    s_dim = p.shape[0]
    blk = SB_BLOCK
    kv = lambda off: pl.BlockSpec((s_dim, HEAD_DIM), lambda h, i: (0, off + h))
    return pl.pallas_call(
        _sb_kernel,
        grid=(SB_HEADS, s_dim // blk),
        in_specs=[pl.BlockSpec((blk, HEAD_DIM), lambda h, i: (i, h)), kv(SB_HEADS), kv(2 * SB_HEADS)],
        out_specs=pl.BlockSpec((blk, HEAD_DIM), lambda h, i: (i, h)),
        out_shape=jax.ShapeDtypeStruct((s_dim, SB_W), BF16),
        scratch_shapes=[pltpu.VMEM((s_dim, HEAD_DIM), BF16), pltpu.VMEM((s_dim, HEAD_DIM), BF16)],
        compiler_params=_cparams("parallel", "arbitrary"),
        name="stick_breaking",
    )(p, p, p)


RET_CHUNK = 512


def _ret_kernel(lg_ref, q_ref, k_ref, v_ref, g_ref, cos_ref, sin_ref, o_ref, state):
    h = pl.program_id(0)
    c = pl.program_id(1)
    n = RET_CHUNK

    @pl.when(c == 0)
    def _():
        state[...] = jnp.zeros_like(state)

    lg = lg_ref[h]
    cos, sin = cos_ref[...], sin_ref[...]
    q = _rope_tile(q_ref[...], cos, sin, RET_QK_DIM // 2)
    k = _rope_tile(k_ref[...], cos, sin, RET_QK_DIM // 2) * (RET_QK_DIM ** -0.5)
    v = v_ref[...].astype(BF16)

    ii = lax.broadcasted_iota(I32, (n, n), 0)
    jj = lax.broadcasted_iota(I32, (n, n), 1)
    rel = (ii - jj).astype(F32)
    decay = jnp.where(rel >= 0.0, jnp.exp(lg * jnp.maximum(rel, 0.0)), 0.0)
    inner = _dot_nt(q.astype(BF16), k.astype(BF16)) * decay
    out = _dot(inner.astype(BF16), v)

    pos = lax.broadcasted_iota(I32, (n, 1), 0).astype(F32)
    q_decay = jnp.exp(lg * (pos + 1.0))
    k_decay = jnp.exp(lg * (n - 1.0 - pos))
    prev = state[...]
    out = out + _dot((q * q_decay).astype(BF16), prev.astype(BF16))
    kd_t = jnp.transpose(k * k_decay).astype(BF16)
    state[...] = jnp.exp(lg * jnp.full((1, 1), n, F32)) * prev + _dot(kd_t, v)

    mu = jnp.mean(out, axis=-1, keepdims=True)
    d = out - mu
    var = jnp.mean(d * d, axis=-1, keepdims=True)
    g = g_ref[...]
    o_ref[...] = (g * _sigmoid(g) * (d * lax.rsqrt(var + LN_EPS))).astype(o_ref.dtype)


def _retention(p, log_gamma, cos, sin):
    s_dim = p.shape[0]
    n = RET_CHUNK
    qk = lambda off: pl.BlockSpec((n, RET_QK_DIM), lambda h, c, lg: (c, off + h))
    vg = lambda off: pl.BlockSpec((n, RET_V_DIM), lambda h, c, lg: (c, off + h))
    tab = pl.BlockSpec((n, LANES), lambda h, c, lg: (c, 0))
    q_off = 3 * SB_W // RET_QK_DIM
    v_off = (3 * SB_W + 2 * RET_QK_W) // RET_V_DIM
    grid_spec = pltpu.PrefetchScalarGridSpec(
        num_scalar_prefetch=1,
        grid=(RET_HEADS, s_dim // n),
        in_specs=[qk(q_off), qk(q_off + RET_HEADS), vg(v_off), vg(v_off + RET_HEADS), tab, tab],
        out_specs=pl.BlockSpec((n, RET_V_DIM), lambda h, c, lg: (c, h)),
        scratch_shapes=[pltpu.VMEM((RET_QK_DIM, RET_V_DIM), F32)],
    )
    return pl.pallas_call(
        _ret_kernel,
        grid_spec=grid_spec,
        out_shape=jax.ShapeDtypeStruct((s_dim, RET_V_W), BF16),
        compiler_params=_cparams("parallel", "arbitrary"),
        name="retention",
    )(log_gamma, p, p, p, p, cos, sin)


def _rope_cast_kernel(x_ref, cos_ref, sin_ref, o_ref, *, half, scale):
    cos, sin = cos_ref[...], sin_ref[...]
    for t in range(x_ref.shape[1] // LANES):
        sl = slice(t * LANES, (t + 1) * LANES)
        o_ref[:, sl] = (_rope_tile(x_ref[:, sl], cos, sin, half) * scale).astype(o_ref.dtype)


def _rope_cast(p, col0, n_cols, cos, sin, *, half, scale, tm=512, tn=1024):
    s_dim = p.shape[0]
    tm = min(tm, s_dim)
    tn = min(tn, n_cols)
    tab = pl.BlockSpec((tm, LANES), lambda i, j: (i, 0))
    return pl.pallas_call(
        functools.partial(_rope_cast_kernel, half=half, scale=scale),
        grid=(s_dim // tm, n_cols // tn),
        in_specs=[pl.BlockSpec((tm, tn), lambda i, j: (i, col0 // tn + j)), tab, tab],
        out_specs=pl.BlockSpec((tm, tn), lambda i, j: (i, j)),
        out_shape=jax.ShapeDtypeStruct((s_dim, n_cols), BF16),
        compiler_params=_cparams("parallel", "parallel"),
        name="rope_cast",
    )(p, cos, sin)


def _cast_kernel(x_ref, o_ref):
    o_ref[...] = x_ref[...].astype(o_ref.dtype)


def _cast(p, col0, n_cols, *, tm=512, tn=1024):
    s_dim = p.shape[0]
    tm = min(tm, s_dim)
    return pl.pallas_call(
        _cast_kernel,
        grid=(s_dim // tm, n_cols // tn),
        in_specs=[pl.BlockSpec((tm, tn), lambda i, j: (i, col0 // tn + j))],
        out_specs=pl.BlockSpec((tm, tn), lambda i, j: (i, j)),
        out_shape=jax.ShapeDtypeStruct((s_dim, n_cols), BF16),
        compiler_params=_cparams("parallel", "parallel"),
        name="cast_bf16",
    )(p)


def _idx_q_kernel(x_ref, cos_ref, sin_ref, o_ref):
    cos, sin = cos_ref[...], sin_ref[...]
    lane = lax.broadcasted_iota(I32, cos.shape, 1)
    low = lane < IDX_DIM
    for t in range(x_ref.shape[1] // LANES):
        y = _rope_tile(x_ref[:, t * LANES:(t + 1) * LANES], cos, sin, IDX_DIM // ROT_FRACTION // 2, IDX_DIM)
        o_ref[:, (2 * t) * LANES:(2 * t + 1) * LANES] = jnp.where(low, y, 0.0).astype(o_ref.dtype)
        o_ref[:, (2 * t + 1) * LANES:(2 * t + 2) * LANES] = jnp.where(
            low, pltpu.roll(y, IDX_DIM, 1), 0.0).astype(o_ref.dtype)


def _idx_q_prep(p, col0, cos, sin, *, tm=512):
    s_dim = p.shape[0]
    tm = min(tm, s_dim)
    tab = pl.BlockSpec((tm, LANES), lambda i: (i, 0))
    return pl.pallas_call(
        _idx_q_kernel,
        grid=(s_dim // tm,),
        in_specs=[pl.BlockSpec((tm, IDX_Q_W), lambda i: (i, col0 // IDX_Q_W)), tab, tab],
        out_specs=pl.BlockSpec((tm, IDX_HEADS * LANES), lambda i: (i, 0)),
        out_shape=jax.ShapeDtypeStruct((s_dim, IDX_HEADS * LANES), BF16),
        compiler_params=_cparams("parallel"),
        name="idx_q_prep",
    )(p, cos, sin)


def _idx_tail_kernel(x_ref, cos_ref, sin_ref, k_ref, w_ref):
    x = x_ref[...]
    lane = lax.broadcasted_iota(I32, x.shape, 1)
    y = _rope_tile(x, cos_ref[...], sin_ref[...], IDX_DIM // ROT_FRACTION // 2, IDX_DIM) * (IDX_DIM ** -0.5)
    k_ref[...] = jnp.where(lane < IDX_DIM, y, 0.0).astype(k_ref.dtype)
    w_ref[...] = x * (IDX_HEADS ** -0.5)


def _idx_tail_prep(tail, cos, sin, *, tm=512):
    s_dim = tail.shape[0]
    tm = min(tm, s_dim)
    blk = pl.BlockSpec((tm, LANES), lambda i: (i, 0))
    return pl.pallas_call(
        _idx_tail_kernel,
        grid=(s_dim // tm,),
        in_specs=[blk, blk, blk],
        out_specs=[blk, blk],
        out_shape=[jax.ShapeDtypeStruct((s_dim, LANES), BF16), jax.ShapeDtypeStruct((s_dim, LANES), F32)],
        compiler_params=_cparams("parallel"),
        name="idx_tail_prep",
    )(tail, cos, sin)


DSA_QB = 128
DSA_KB = 512
DSA_AQB = 512


def _select_kernel(qi_ref, ki_ref, w_ref, o_ref, keys_ref, *, topk, n_kblocks, idx_bits):
    i = pl.program_id(0)
    qb, kb = DSA_QB, DSA_KB
    n_live = ((i + 1) * qb + kb - 1) // kb
    q_pos = i * qb + lax.broadcasted_iota(I32, (qb, kb), 0)
    col0 = lax.broadcasted_iota(I32, (qb, kb), 1)
    w = w_ref[...]

    def score_block(jb, _):
        start = pl.multiple_of(jb * kb, kb)
        kj = ki_ref[pl.ds(start, kb), :]
        acc = jnp.zeros((qb, kb), F32)
        for h in range(IDX_HEADS):
            z = _dot_nt(qi_ref[:, h * LANES:(h + 1) * LANES], kj)
            acc = acc + jnp.maximum(z, 0.0) * w[:, IDX_DIM + h:IDX_DIM + h + 1]
        acc = jnp.where(acc == 0.0, 0.0, acc)
        bits = lax.bitcast_convert_type(acc, I32)
        key = jnp.where(bits >= 0, bits, bits ^ 0x7FFFFFFF)
        keys_ref[jb] = jnp.where(jb * kb + col0 <= q_pos, key, INT_MIN)
        return 0

    lax.fori_loop(0, n_live, score_block, 0)

    def count(pred):
        def body(jb, part):
            kk = keys_ref[jb]
            hit = jnp.where(pred(kk, jb * kb + col0), 1.0, 0.0)
            for t in range(kb // LANES):
                part = part + hit[:, t * LANES:(t + 1) * LANES]
            return part
        part = lax.fori_loop(0, n_live, body, jnp.zeros((qb, LANES), F32))
        return jnp.sum(part, axis=1, keepdims=True)

    k_f = float(topk)
    c_nonneg = count(lambda kk, cc: kk >= 0)
    thr = jnp.where(c_nonneg >= k_f, jnp.int32(0), jnp.int32(INT_MIN))
    c_thr = jnp.where(c_nonneg >= k_f, c_nonneg, float(n_kblocks * kb))
    short = i * qb + lax.broadcasted_iota(I32, (qb, 1), 0) + 1 < topk

    def unsettled(c_thr):
        return jnp.max(jnp.where((c_thr == k_f) | short, 0.0, 1.0)) > 0.0

    def thr_cond(carry):
        return jnp.logical_and(carry[0] < 31, unsettled(carry[2]))

    def thr_bit(carry):
        b, thr, c_thr = carry
        cand = thr + jnp.left_shift(jnp.int32(1), 30 - b)
        c = count(lambda kk, cc: kk >= cand)
        ok = c >= k_f
        return b + 1, jnp.where(ok, cand, thr), jnp.where(ok, c, c_thr)

    _, thr, c_thr = lax.while_loop(thr_cond, thr_bit, (jnp.int32(0), thr, c_thr))

    def no_cut():
        return jnp.full((qb, 1), n_kblocks * kb, I32)

    def tie_cut():
        need = k_f - count(lambda kk, cc: kk > thr)
        n_eq = count(lambda kk, cc: kk == thr)
        tied = (n_eq > need) & (thr != INT_MIN)

        def cut_bit(b, cut):
            cand = cut + jnp.left_shift(jnp.int32(1), idx_bits - 1 - b)
            c = count(lambda kk, cc: (kk == thr) & (cc < cand))
            return jnp.where(c < need, cand, cut)

        def cut_search():
            return lax.fori_loop(0, idx_bits, cut_bit, jnp.zeros((qb, 1), I32))

        return lax.cond(jnp.max(jnp.where(tied, 1.0, 0.0)) > 0.0, cut_search, no_cut)

    cut = lax.cond(unsettled(c_thr), tie_cut, no_cut)

    def write_live(jb, _):
        kk = keys_ref[jb]
        cc = jb * kb + col0
        sel = ((kk > thr) | ((kk == thr) & (cc <= cut))) & (kk != INT_MIN)
        o_ref[jb] = jnp.where(sel, 0.0, MASK_NEG).astype(o_ref.dtype)
        return 0

    lax.fori_loop(0, n_live, write_live, 0)

    def write_dead(jb, _):
        o_ref[jb] = jnp.full((qb, kb), MASK_NEG, o_ref.dtype)
        return 0

    lax.fori_loop(n_live, n_kblocks, write_dead, 0)


def _dsa_select(qi, ki, w, topk):
    s_dim = qi.shape[0]
    qb, kb = DSA_QB, DSA_KB
    n_kblocks = s_dim // kb
    idx_bits = max(1, (s_dim - 1).bit_length())
    return pl.pallas_call(
        functools.partial(_select_kernel, topk=topk, n_kblocks=n_kblocks, idx_bits=idx_bits),
        grid=(s_dim // qb,),
        in_specs=[
            pl.BlockSpec((qb, IDX_HEADS * LANES), lambda i: (i, 0)),
            pl.BlockSpec((s_dim, LANES), lambda i: (0, 0)),
            pl.BlockSpec((qb, LANES), lambda i: (i, 0)),
        ],
        out_specs=pl.BlockSpec((None, n_kblocks, qb, kb), lambda i: (i, 0, 0, 0)),
        out_shape=jax.ShapeDtypeStruct((s_dim // qb, n_kblocks, qb, kb), BF16),
        scratch_shapes=[pltpu.VMEM((n_kblocks, qb, kb), I32)],
        compiler_params=_cparams("parallel"),
        name="dsa_select",
    )(qi, ki, w)


def _dsa_attn_kernel(q_ref, k_ref, v_ref, b_ref, o_ref):
    i = pl.program_id(1)
    qb, kb, grp = DSA_AQB, DSA_KB, DSA_GROUP
    n_sel = qb // DSA_QB
    n_live = ((i + 1) * qb + kb - 1) // kb
    q = jnp.concatenate([q_ref[:, r * HEAD_DIM:(r + 1) * HEAD_DIM] for r in range(grp)], axis=0)

    def body(jb, carry):
        m, l, acc = carry
        start = pl.multiple_of(jb * kb, kb)
        kj = k_ref[pl.ds(start, kb), :]
        vj = v_ref[pl.ds(start, kb), :]
        z = _dot_nt(q, kj).astype(BF16)
        bias = b_ref[:, jb].reshape(1, qb, kb)
        z = (z.reshape(grp, qb, kb) + bias).reshape(grp * qb, kb)
        m_new = jnp.maximum(m, jnp.max(z, axis=1, keepdims=True).astype(F32))
        p = jnp.exp2(z - m_new.astype(BF16))
        alpha = jnp.exp2(m - m_new)
        l = alpha * l + jnp.sum(p.astype(F32), axis=1, keepdims=True)
        acc = alpha * acc + _dot(p, vj)
        return m_new, l, acc

    rows = grp * qb
    init = (jnp.full((rows, 1), MASK_NEG, F32), jnp.zeros((rows, 1), F32), jnp.zeros((rows, HEAD_DIM), F32))
    _, l, acc = lax.fori_loop(0, n_live, body, init)
    out = acc / l
    for r in range(grp):
        o_ref[:, r * HEAD_DIM:(r + 1) * HEAD_DIM] = out[r * qb:(r + 1) * qb, :].astype(o_ref.dtype)


def _dsa_attention(q, k, v, bias):
    s_dim = q.shape[0]
    qb, kb = DSA_AQB, DSA_KB
    gw = DSA_GROUP * HEAD_DIM
    kv = pl.BlockSpec((s_dim, HEAD_DIM), lambda g, i: (0, g))
    return pl.pallas_call(
        _dsa_attn_kernel,
        grid=(DSA_KV_HEADS, s_dim // qb),
        in_specs=[
            pl.BlockSpec((qb, gw), lambda g, i: (i, g)),
            kv, kv,
            pl.BlockSpec((qb // DSA_QB, s_dim // kb, DSA_QB, kb), lambda g, i: (i, 0, 0, 0)),
        ],
        out_specs=pl.BlockSpec((qb, gw), lambda g, i: (i, g)),
        out_shape=jax.ShapeDtypeStruct((s_dim, DSA_Q_W), BF16),
        compiler_params=_cparams("parallel", "arbitrary"),
        name="dsa_attention",
    )(q, k, v, bias)


ROUTE_E1, ROUTE_E2, ROUTE_G1, ROUTE_G2, ROUTE_R1, ROUTE_R2 = range(6)


def _router_kernel(x_ref, w_ref, o_ref, cnt_ref, seen):
    @pl.when(pl.program_id(0) == 0)
    def _():
        seen[...] = jnp.zeros_like(seen)

    x = x_ref[...]
    w = w_ref[...]
    xh = x.astype(BF16)
    xl = (x - xh.astype(F32)).astype(BF16)
    wh = w.astype(BF16)
    wl = (w - wh.astype(F32)).astype(BF16)
    logits = _dot(xh, wh) + (_dot(xh, wl) + _dot(xl, wh))
    lane = lax.broadcasted_iota(I32, logits.shape, 1).astype(F32)
    logits = jnp.where(lane < N_EXPERTS, logits, -jnp.inf)
    v1 = jnp.max(logits, axis=1, keepdims=True)
    i1 = jnp.min(jnp.where(logits == v1, lane, float(LANES)), axis=1, keepdims=True)
    rest = jnp.where(lane == i1, -jnp.inf, logits)
    v2 = jnp.max(rest, axis=1, keepdims=True)
    i2 = jnp.min(jnp.where(rest == v2, lane, float(LANES)), axis=1, keepdims=True)
    e2 = jnp.exp(v2 - v1)
    g1 = 1.0 / (1.0 + e2)
    g2 = e2 / (1.0 + e2)

    tb = x.shape[0]
    member = jnp.where((lane == i1) | (lane == i2), 1.0, 0.0)
    earlier = jnp.where(lax.broadcasted_iota(I32, (tb, tb), 1) < lax.broadcasted_iota(I32, (tb, tb), 0), 1.0, 0.0)
    prefix = _dot(earlier.astype(BF16), member.astype(BF16)) + seen[0:1, :]
    r1 = jnp.sum(jnp.where(lane == i1, prefix, 0.0), axis=1, keepdims=True)
    r2 = jnp.sum(jnp.where(lane == i2, prefix, 0.0), axis=1, keepdims=True)
    seen[...] = seen[...] + jnp.sum(member, axis=0, keepdims=True)
    cnt_ref[...] = seen[...]

    rec = jnp.zeros_like(logits)
    for slot, val in ((ROUTE_E1, i1), (ROUTE_E2, i2), (ROUTE_G1, g1), (ROUTE_G2, g2), (ROUTE_R1, r1), (ROUTE_R2, r2)):
        rec = jnp.where(lane == float(slot), val, rec)
    o_ref[...] = rec


def _router(x, w_pad, *, tm=256):
    s_dim, d_dim = x.shape
    tm = min(tm, s_dim)
    return pl.pallas_call(
        _router_kernel,
        grid=(s_dim // tm,),
        in_specs=[pl.BlockSpec((tm, d_dim), lambda i: (i, 0)), pl.BlockSpec((d_dim, LANES), lambda i: (0, 0))],
        out_specs=[pl.BlockSpec((tm, LANES), lambda i: (i, 0)), pl.BlockSpec((8, LANES), lambda i: (0, 0))],
        out_shape=[jax.ShapeDtypeStruct((s_dim, LANES), F32), jax.ShapeDtypeStruct((8, LANES), F32)],
        scratch_shapes=[pltpu.VMEM((8, LANES), F32)],
        compiler_params=_cparams("arbitrary"),
        name="moe_router",
    )(x, w_pad)


MOE_TM = 512


def _moe_plan(route, counts, s_dim):
    tm = min(MOE_TM, s_dim)
    n_rows = 2 * s_dim + N_EXPERTS * tm
    e1 = route[:, ROUTE_E1].astype(I32)
    e2 = route[:, ROUTE_E2].astype(I32)
    cnt = counts[0, :N_EXPERTS].astype(I32)
    padded = (cnt + tm - 1) // tm * tm
    ends = jnp.cumsum(padded)
    starts = ends - padded
    dest1 = starts[e1] + route[:, ROUTE_R1].astype(I32)
    dest2 = starts[e2] + route[:, ROUTE_R2].astype(I32)
    tok = jnp.arange(s_dim, dtype=I32)
    src_tok = jnp.zeros((n_rows,), I32).at[dest1].set(tok).at[dest2].set(tok)
    tile_start = jnp.arange(n_rows // tm, dtype=I32) * tm
    tile_expert = jnp.minimum(jnp.searchsorted(ends, tile_start, side="right"), N_EXPERTS - 1).astype(I32)
    n_used = (ends[-1:] // tm).astype(I32)
    return dict(tm=tm, n_rows=n_rows, dest=jnp.concatenate([dest1, dest2]), src_tok=src_tok,
                tile_expert=tile_expert, n_used=n_used)


def _row_copy(src_ref, src_row, dst_ref, dst_row, sem):
    return pltpu.make_async_copy(src_ref.at[pl.ds(src_row, 1), :], dst_ref.at[pl.ds(dst_row, 1), :], sem)


def _dispatch_kernel(src_tok_ref, x_ref, o_ref, buf, sem):
    i = pl.program_id(0)
    n = pl.num_programs(0)
    tm = buf.shape[1]

    def fetch(tile, slot):
        def issue(r, _):
            _row_copy(x_ref, src_tok_ref[tile * tm + r], buf.at[slot], r, sem.at[slot]).start()
            return 0
        lax.fori_loop(0, tm, issue, 0)

    @pl.when(i == 0)
    def _():
        fetch(0, 0)

    @pl.when(i + 1 < n)
    def _():
        fetch(i + 1, (i + 1) % 2)

    slot = i % 2
    pltpu.make_async_copy(x_ref.at[pl.ds(0, tm), :], buf.at[slot], sem.at[slot]).wait()
    o_ref[...] = buf[slot].astype(o_ref.dtype)


def _moe_dispatch(x, plan, *, tm=512):
    s_dim, d_dim = x.shape
    tm = min(tm, s_dim)
    n_rows = plan["n_rows"]
    return pl.pallas_call(
        _dispatch_kernel,
        grid_spec=pltpu.PrefetchScalarGridSpec(
            num_scalar_prefetch=1,
            grid=(n_rows // tm,),
            in_specs=[pl.BlockSpec(memory_space=pl.ANY)],
            out_specs=pl.BlockSpec((tm, d_dim), lambda i, st: (i, 0)),
            scratch_shapes=[pltpu.VMEM((2, tm, d_dim), F32), pltpu.SemaphoreType.DMA((2,))],
        ),
        out_shape=jax.ShapeDtypeStruct((n_rows, d_dim), BF16),
        compiler_params=_cparams("arbitrary"),
        name="moe_dispatch",
    )(plan["src_tok"], x)


def _moe_up_kernel(te_ref, nu_ref, a_ref, wg_ref, wu_ref, o_ref):
    m = pl.program_id(1)

    @pl.when(m < nu_ref[0])
    def _():
        a = a_ref[...]
        g = _dot(a, wg_ref[...].astype(BF16))
        u = _dot(a, wu_ref[...].astype(BF16))
        o_ref[...] = (g * _sigmoid(g) * u).astype(o_ref.dtype)

    @pl.when(m >= nu_ref[0])
    def _():
        o_ref[...] = jnp.zeros_like(o_ref)


def _moe_up(xs, wg, wu, layer, plan, *, tn=256):
    n_rows, d_dim = xs.shape
    tm = plan["tm"]
    w_spec = pl.BlockSpec((None, None, d_dim, tn), lambda n, m, te, nu: (layer, te[m], 0, n))
    return pl.pallas_call(
        _moe_up_kernel,
        grid_spec=pltpu.PrefetchScalarGridSpec(
            num_scalar_prefetch=2,
            grid=(D_FF_EXPERT // tn, n_rows // tm),
            in_specs=[pl.BlockSpec((tm, d_dim), lambda n, m, te, nu: (m, 0)), w_spec, w_spec],
            out_specs=pl.BlockSpec((tm, tn), lambda n, m, te, nu: (m, n)),
        ),
        out_shape=jax.ShapeDtypeStruct((n_rows, D_FF_EXPERT), BF16),
        compiler_params=_cparams("parallel", "arbitrary"),
        name="moe_up",
    )(plan["tile_expert"], plan["n_used"], xs, wg, wu)


def _moe_down_kernel(te_ref, nu_ref, a_ref, w_ref, o_ref):
    m = pl.program_id(1)

    @pl.when(m < nu_ref[0])
    def _():
        o_ref[...] = _dot(a_ref[...], w_ref[...].astype(BF16))

    @pl.when(m >= nu_ref[0])
    def _():
        o_ref[...] = jnp.zeros_like(o_ref)


def _moe_down(hid, wd, layer, plan, *, tn=2048):
    n_rows, f_dim = hid.shape
    tm = plan["tm"]
    d_dim = wd.shape[-1]
    return pl.pallas_call(
        _moe_down_kernel,
        grid_spec=pltpu.PrefetchScalarGridSpec(
            num_scalar_prefetch=2,
            grid=(d_dim // tn, n_rows // tm),
            in_specs=[pl.BlockSpec((tm, f_dim), lambda n, m, te, nu: (m, 0)),
                      pl.BlockSpec((None, None, f_dim, tn), lambda n, m, te, nu: (layer, te[m], 0, n))],
            out_specs=pl.BlockSpec((tm, tn), lambda n, m, te, nu: (m, n)),
        ),
        out_shape=jax.ShapeDtypeStruct((n_rows, d_dim), F32),
        compiler_params=_cparams("parallel", "arbitrary"),
        name="moe_down",
    )(plan["tile_expert"], plan["n_used"], hid, wd)


def _combine_ln_kernel(dest_ref, x_ref, r_ref, ys_ref, g_ref, b_ref, o_ref, ob_ref, buf, sem, *, s_dim):
    i = pl.program_id(0)
    n = pl.num_programs(0)
    tb = x_ref.shape[0]

    def fetch(blk, slot):
        def issue(r, _):
            t = blk * tb + r
            _row_copy(ys_ref, dest_ref[t], buf.at[slot, 0], r, sem.at[slot]).start()
            _row_copy(ys_ref, dest_ref[s_dim + t], buf.at[slot, 1], r, sem.at[slot]).start()
            return 0
        lax.fori_loop(0, tb, issue, 0)

    @pl.when(i == 0)
    def _():
        fetch(0, 0)

    @pl.when(i + 1 < n)
    def _():
        fetch(i + 1, (i + 1) % 2)

    slot = i % 2
    for k in range(2):
        pltpu.make_async_copy(ys_ref.at[pl.ds(0, tb), :], buf.at[slot, k], sem.at[slot]).wait()
    rec = r_ref[...]
    f = rec[:, ROUTE_G1:ROUTE_G1 + 1] * buf[slot, 0] + rec[:, ROUTE_G2:ROUTE_G2 + 1] * buf[slot, 1]
    y = DEEPNORM_ALPHA * x_ref[...] + f
    mu = jnp.mean(y, axis=-1, keepdims=True)
    d = y - mu
    var = jnp.mean(d * d, axis=-1, keepdims=True)
    out = d * lax.rsqrt(var + LN_EPS) * g_ref[...] + b_ref[...]
    o_ref[...] = out
    ob_ref[...] = out.astype(BF16)


def _moe_combine_ln(x, route, ys, plan, g3, b3, idx, *, tb=256):
    s_dim, d_dim = x.shape
    tb = min(tb, s_dim)
    row = pl.BlockSpec((tb, d_dim), lambda i, de: (i, 0))
    par = pl.BlockSpec((None, 1, d_dim), lambda i, de: (idx, 0, 0))
    return pl.pallas_call(
        functools.partial(_combine_ln_kernel, s_dim=s_dim),
        grid_spec=pltpu.PrefetchScalarGridSpec(
            num_scalar_prefetch=1,
            grid=(s_dim // tb,),
            in_specs=[row, pl.BlockSpec((tb, LANES), lambda i, de: (i, 0)), pl.BlockSpec(memory_space=pl.ANY),
                      par, par],
            out_specs=[row, row],
            scratch_shapes=[pltpu.VMEM((2, 2, tb, d_dim), F32), pltpu.SemaphoreType.DMA((2,))],
        ),
        out_shape=[jax.ShapeDtypeStruct((s_dim, d_dim), F32), jax.ShapeDtypeStruct((s_dim, d_dim), BF16)],
        compiler_params=_cparams("arbitrary"),
        name="moe_combine_ln",
    )(plan["dest"], x, route, ys, g3, b3)


def _even_layer(x, xb, j, i, even_w_in, even_w_out, ffn_w_gate, ffn_w_up, ffn_w_down, ln_g3, ln_b3, tabs):
    d = D_MODEL
    p = _matmul_rows([xb], even_w_in, lambda n: (j, 0, n), EVEN_IN, name="even_in_proj")
    a = _stick_breaking(p)
    r = _retention(p, tabs["log_gamma"], tabs["ret_cos"], tabs["ret_sin"])
    h = _matmul_rows([a, r], even_w_out, lambda n: (j, 0, n), d, name="even_out_proj")
    x, xb = _deepnorm_ln(x, h, ln_g3, ln_b3, 2 * i)
    hid = _swiglu_rows(xb, ffn_w_gate, ffn_w_up, lambda n: (j, 0, n), D_FF, name="ffn_up")
    f = _matmul_rows([hid], ffn_w_down, lambda n: (j, 0, n), d, name="ffn_down")
    return _deepnorm_ln(x, f, ln_g3, ln_b3, 2 * i + 1)


def _odd_layer(x, xb, j, i, odd_w_in, odd_w_out, moe_w_router, moe_w_gate, moe_w_up, moe_w_down,
               ln_g3, ln_b3, tabs):
    d = D_MODEL
    s_dim = x.shape[0]
    w_nk = jnp.swapaxes(odd_w_in, 1, 2)
    p = _matmul_rows([xb], w_nk, lambda n: (j, n, 0), ODD_MAIN, name="odd_in_proj", w_is_nk=True)
    w_tail = jnp.pad(w_nk[j, ODD_MAIN:, :], ((0, LANES - ODD_TAIL), (0, 0)))
    tail = _matmul_rows([xb], w_tail, lambda n: (n, 0), LANES, name="odd_in_tail", w_is_nk=True)

    cos_h, sin_h = tabs["head_cos"], tabs["head_sin"]
    q = _rope_cast(p, 0, DSA_Q_W, cos_h, sin_h, half=HEAD_DIM // ROT_FRACTION // 2,
                   scale=HEAD_DIM ** -0.5 * LOG2_E)
    k = _rope_cast(p, DSA_Q_W, DSA_KV_W, cos_h, sin_h, half=HEAD_DIM // ROT_FRACTION // 2, scale=1.0)
    v = _cast(p, DSA_Q_W + DSA_KV_W, DSA_KV_W)
    qi = _idx_q_prep(p, DSA_Q_W + 2 * DSA_KV_W, tabs["idx_cos"], tabs["idx_sin"])
    ki, wi = _idx_tail_prep(tail, tabs["tail_cos"], tabs["tail_sin"])

    bias = _dsa_select(qi, ki, wi, min(IDX_TOPK_MAX, s_dim // 4))
    y = _dsa_attention(q, k, v, bias)
    h = _matmul_rows([y], odd_w_out, lambda n: (j, 0, n), d, name="odd_out_proj")
    x, xb = _deepnorm_ln(x, h, ln_g3, ln_b3, 2 * i)

    w_router = jnp.pad(moe_w_router[j], ((0, 0), (0, LANES - N_EXPERTS)))
    route, counts = _router(x, w_router)
    plan = _moe_plan(route, counts, s_dim)
    xs = _moe_dispatch(x, plan)
    hid = _moe_up(xs, moe_w_gate, moe_w_up, j, plan)
    ys = _moe_down(hid, moe_w_down, j, plan)
    return _moe_combine_ln(x, route, ys, plan, ln_g3, ln_b3, 2 * i + 1)


def kernel(x, even_w_in, even_w_out, odd_w_in, odd_w_out, ffn_w_gate, ffn_w_up, ffn_w_down, moe_w_router,
           moe_w_gate, moe_w_up, moe_w_down, ln_g, ln_b):
    batch, s_dim, d = x.shape
    ln_g3 = ln_g.reshape(2 * DEPTH, 1, d)
    ln_b3 = ln_b.reshape(2 * DEPTH, 1, d)
    ret_cos, ret_sin = _rope_tables(s_dim, LANES, RET_QK_DIM, RET_THETA)
    head_cos, head_sin = _rope_tables(s_dim, LANES, HEAD_DIM // ROT_FRACTION, ROPE_THETA)
    idx_cos, idx_sin = _rope_tables(s_dim, IDX_DIM, IDX_DIM // ROT_FRACTION, ROPE_THETA)
    tail_cos, tail_sin = _rope_tables(s_dim, IDX_DIM, IDX_DIM // ROT_FRACTION, ROPE_THETA, active=IDX_DIM)
    tabs = dict(
        log_gamma=jnp.log1p(-jnp.exp2(-5.0 - jnp.arange(RET_HEADS, dtype=F32))),
        ret_cos=ret_cos, ret_sin=ret_sin, head_cos=head_cos, head_sin=head_sin,
        idx_cos=idx_cos, idx_sin=idx_sin, tail_cos=tail_cos, tail_sin=tail_sin,
    )
    outs = []
    for b in range(batch):
        xs = x[b] if batch > 1 else x.reshape(s_dim, d)
        xb = xs.astype(BF16)
        for i in range(DEPTH):
            j = i // 2
            if i % 2 == 0:
                xs, xb = _even_layer(xs, xb, j, i, even_w_in, even_w_out, ffn_w_gate, ffn_w_up, ffn_w_down,
                                     ln_g3, ln_b3, tabs)
            else:
                xs, xb = _odd_layer(xs, xb, j, i, odd_w_in, odd_w_out, moe_w_router, moe_w_gate, moe_w_up,
                                    moe_w_down, ln_g3, ln_b3, tabs)
        outs.append(xs)
    return jnp.stack(outs, axis=0) if batch > 1 else outs[0].reshape(1, s_dim, d)
```

```python
import functools
import math

import jax
import jax.numpy as jnp
from jax import lax
from jax.experimental import pallas as pl
from jax.experimental.pallas import tpu as pltpu

F32 = jnp.float32
BF16 = jnp.bfloat16
I32 = jnp.int32

D_MODEL = 4096
DEPTH = 4
HEAD_DIM = 128
SB_HEADS = 16
SB_W = SB_HEADS * HEAD_DIM
RET_HEADS = 8
RET_QK_DIM = 128
RET_V_DIM = 256
RET_QK_W = RET_HEADS * RET_QK_DIM
RET_V_W = RET_HEADS * RET_V_DIM
RET_THETA = 10000.0
DSA_Q_HEADS = 32
DSA_KV_HEADS = 8
DSA_GROUP = DSA_Q_HEADS // DSA_KV_HEADS
DSA_Q_W = DSA_Q_HEADS * HEAD_DIM
DSA_KV_W = DSA_KV_HEADS * HEAD_DIM
IDX_HEADS = 16
IDX_DIM = 64
IDX_Q_W = IDX_HEADS * IDX_DIM
IDX_TOPK_MAX = 256
ROPE_THETA = 500000.0
ROT_FRACTION = 4
D_FF = 11008
N_EXPERTS = 8
D_FF_EXPERT = 1792
LN_EPS = 1e-5
DEEPNORM_ALPHA = (2.0 * DEPTH) ** 0.25

EVEN_IN = 3 * SB_W + 2 * RET_QK_W + 2 * RET_V_W
ODD_MAIN = DSA_Q_W + 2 * DSA_KV_W + IDX_Q_W
ODD_TAIL = IDX_DIM + IDX_HEADS

LANES = 128
V7X_VMEM_BYTES = 64 * 1024 * 1024
VMEM_LIMIT_BYTES = V7X_VMEM_BYTES * 7 // 8

INT_MIN = -2147483648
MASK_NEG = -(2.0 ** 100)
LOG2_E = math.log2(math.e)


def _cparams(*sem):
    return pltpu.CompilerParams(dimension_semantics=sem, vmem_limit_bytes=VMEM_LIMIT_BYTES)


def _sigmoid(x):
    return 1.0 / (1.0 + jnp.exp(-x))


def _dot(a, b):
    return jnp.dot(a, b, preferred_element_type=F32)


def _dot_nt(a, b):
    return lax.dot_general(a, b, (((1,), (1,)), ((), ())), preferred_element_type=F32)


def _mm_rows_kernel(*refs, w_is_nk):
    *a_refs, w_ref, o_ref = refs
    k0, acc = 0, None
    for a_ref in a_refs:
        kw = a_ref.shape[1]
        if w_is_nk:
            part = _dot_nt(a_ref[...], w_ref[:, k0:k0 + kw].astype(BF16))
        else:
            part = _dot(a_ref[...], w_ref[k0:k0 + kw, :].astype(BF16))
        acc = part if acc is None else acc + part
        k0 += kw
    o_ref[...] = acc


ROW_PANEL_MAX_ROWS = 2048
ROW_PANEL_VMEM_SHARE = 0.4
ROW_STEP_VMEM_SHARE = 0.85


def _row_panel_tiles(m_dim, k_dim, n_cols, n_weights, out_bytes):
    tm = min(ROW_PANEL_MAX_ROWS, m_dim)
    while tm * k_dim * 2 > ROW_PANEL_VMEM_SHARE * VMEM_LIMIT_BYTES and tm % 2 == 0 and m_dim % (tm // 2) == 0:
        tm //= 2
    for tn in (4 * LANES, 2 * LANES, LANES):
        step = tm * k_dim * 2 + 2 * (n_weights * k_dim * tn * 4 + tm * tn * out_bytes)
        if n_cols % tn == 0 and step <= ROW_STEP_VMEM_SHARE * VMEM_LIMIT_BYTES:
            return tm, tn
    raise ValueError(f"no row-panel tiling for K={k_dim}, N={n_cols}")


def _matmul_rows(panels, w, w_index, n_cols, *, name, w_is_nk=False):
    m_dim = panels[0].shape[0]
    k_dim = sum(a.shape[1] for a in panels)
    tm, tn = _row_panel_tiles(m_dim, k_dim, n_cols, 1, 4)
    n_lead = w.ndim - 2
    w_block = (tn, k_dim) if w_is_nk else (k_dim, tn)
    return pl.pallas_call(
        functools.partial(_mm_rows_kernel, w_is_nk=w_is_nk),
        grid=(m_dim // tm, n_cols // tn),
        in_specs=[pl.BlockSpec((tm, a.shape[1]), lambda m, n: (m, 0), pipeline_mode=pl.Buffered(1))
                  for a in panels]
        + [pl.BlockSpec((None,) * n_lead + w_block, lambda m, n: w_index(n))],
        out_specs=pl.BlockSpec((tm, tn), lambda m, n: (m, n)),
        out_shape=jax.ShapeDtypeStruct((m_dim, n_cols), F32),
        compiler_params=_cparams("parallel", "arbitrary"),
        name=name,
    )(*panels, w)


def _swiglu_rows_kernel(a_ref, wg_ref, wu_ref, o_ref):
    a = a_ref[...]
    g = _dot(a, wg_ref[...].astype(BF16))
    u = _dot(a, wu_ref[...].astype(BF16))
    o_ref[...] = (g * _sigmoid(g) * u).astype(o_ref.dtype)


def _swiglu_rows(a, wg, wu, w_index, n_cols, *, name):
    m_dim, k_dim = a.shape
    tm, tn = _row_panel_tiles(m_dim, k_dim, n_cols, 2, 2)
    n_lead = wg.ndim - 2
    w_spec = pl.BlockSpec((None,) * n_lead + (k_dim, tn), lambda m, n: w_index(n))
    return pl.pallas_call(
        _swiglu_rows_kernel,
        grid=(m_dim // tm, n_cols // tn),
        in_specs=[pl.BlockSpec((tm, k_dim), lambda m, n: (m, 0), pipeline_mode=pl.Buffered(1)), w_spec, w_spec],
        out_specs=pl.BlockSpec((tm, tn), lambda m, n: (m, n)),
        out_shape=jax.ShapeDtypeStruct((m_dim, n_cols), BF16),
        compiler_params=_cparams("parallel", "arbitrary"),
        name=name,
    )(a, wg, wu)


def _ln_kernel(x_ref, h_ref, g_ref, b_ref, o_ref, ob_ref):
    y = DEEPNORM_ALPHA * x_ref[...] + h_ref[...]
    mu = jnp.mean(y, axis=-1, keepdims=True)
    d = y - mu
    var = jnp.mean(d * d, axis=-1, keepdims=True)
    out = d * lax.rsqrt(var + LN_EPS) * g_ref[...] + b_ref[...]
    o_ref[...] = out
    ob_ref[...] = out.astype(BF16)


def _deepnorm_ln(x, h, g3, b3, idx, *, tm=256):
    s_dim, d_dim = x.shape
    tm = min(tm, s_dim)
    row = pl.BlockSpec((tm, d_dim), lambda i: (i, 0))
    par = pl.BlockSpec((None, 1, d_dim), lambda i: (idx, 0, 0))
    return pl.pallas_call(
        _ln_kernel,
        grid=(s_dim // tm,),
        in_specs=[row, row, par, par],
        out_specs=[row, row],
        out_shape=[jax.ShapeDtypeStruct((s_dim, d_dim), F32), jax.ShapeDtypeStruct((s_dim, d_dim), BF16)],
        compiler_params=_cparams("parallel"),
        name="deepnorm_ln",
    )(x, h, g3, b3)


def _rope_tables(s_dim, period, rot_dim, theta, active=LANES):
    half = rot_dim // 2
    pos = jnp.arange(s_dim, dtype=F32)
    freqs = jnp.exp(-math.log(theta) * jnp.arange(half, dtype=F32) * (2.0 / rot_dim))
    ang = pos[:, None] * freqs[None, :]
    cos, sin = jnp.cos(ang), jnp.sin(ang)
    ones = jnp.ones((s_dim, period - rot_dim), F32)
    zeros = jnp.zeros((s_dim, period - rot_dim), F32)
    cos_p = jnp.concatenate([cos, cos, ones], axis=1)
    sin_p = jnp.concatenate([-sin, sin, zeros], axis=1)
    reps = LANES // period
    cos_t, sin_t = jnp.tile(cos_p, (1, reps)), jnp.tile(sin_p, (1, reps))
    if active < LANES:
        lane = jnp.arange(LANES)[None, :]
        cos_t = jnp.where(lane < active, cos_t, 1.0)
        sin_t = jnp.where(lane < active, sin_t, 0.0)
    return cos_t, sin_t


def _rope_tile(x, cos, sin, half, period=LANES):
    if 2 * half == LANES:
        return x * cos + pltpu.roll(x, half, 1) * sin
    lane = lax.broadcasted_iota(I32, x.shape, 1)
    first = (lane & (period - 1)) < half
    partner = jnp.where(first, pltpu.roll(x, LANES - half, 1), pltpu.roll(x, half, 1))
    return x * cos + partner * sin


SB_BLOCK = 256
SB_GROUP = 2
SB_DEAD = 160.0


def _sb_kernel(q_ref, k_ref, v_ref, o_ref, kb_ref, vb_ref):
    i = pl.program_id(1)
    blk = SB_BLOCK
    span = SB_GROUP * blk

    @pl.when(i == 0)
    def _():
        kb_ref[...] = k_ref[...].astype(BF16)
        vb_ref[...] = v_ref[...].astype(BF16)

    q = (q_ref[...] * (HEAD_DIM ** -0.5 * LOG2_E)).astype(BF16)
    row = lax.broadcasted_iota(I32, (blk, blk), 0)
    col = lax.broadcasted_iota(I32, (blk, blk), 1)
    later = jnp.where(row > col, 1.0, 0.0).astype(BF16)

    def group(start, acc, run, key_end):
        z = _dot_nt(q, kb_ref[pl.ds(start, span), :])
        soft = jnp.log(1.0 + jnp.exp2(-jnp.abs(z))) * LOG2_E
        sp = jnp.maximum(z, 0.0) + soft
        log_beta = z - sp
        if key_end is not None:
            key_pos = start + lax.broadcasted_iota(I32, (blk, span), 1)
            q_pos = i * blk + lax.broadcasted_iota(I32, (blk, span), 0)
            causal = (key_pos < q_pos) & (key_pos < key_end)
            sp = jnp.where(causal, sp, 0.0)
        parts = [None] * SB_GROUP
        for c in reversed(range(SB_GROUP)):
            sp_c = sp[:, c * blk:(c + 1) * blk]
            parts[c] = _dot(sp_c.astype(BF16), later) + run
            run = run + jnp.sum(sp_c, axis=1, keepdims=True)
        w = jnp.exp2(log_beta - jnp.concatenate(parts, axis=1))
        if key_end is not None:
            w = jnp.where(causal, w, 0.0)
        acc = acc + _dot(w.astype(BF16), vb_ref[pl.ds(start, span), :])
        return acc, run

    below = jnp.maximum(i - (SB_GROUP - 1), 0)
    n_full = below // SB_GROUP
    n_left = below - n_full * SB_GROUP
    acc0 = jnp.zeros((blk, HEAD_DIM), F32)
    run0 = jnp.zeros((blk, 1), F32)
    acc, run = group(pl.multiple_of(below * blk, blk), acc0, run0, (i + 1) * blk)

    def live(run):
        return jnp.min(run) < SB_DEAD

    def cond(carry):
        return jnp.logical_and(carry[0] < n_full, live(carry[2]))

    def body(carry):
        t, acc, run = carry
        start = pl.multiple_of((below - SB_GROUP * (t + 1)) * blk, blk)
        acc, run = group(start, acc, run, None)
        return t + 1, acc, run

    t, acc, run = lax.while_loop(cond, body, (jnp.int32(0), acc, run))
    tail = jnp.logical_and(jnp.logical_and(t == n_full, n_left > 0), live(run))
    acc, run = lax.cond(tail, lambda: group(0, acc, run, n_left * blk), lambda: (acc, run))
    o_ref[...] = acc.astype(o_ref.dtype)


def _stick_breaking(p):
    s_dim = p.shape[0]
    blk = SB_BLOCK
    kv = lambda off: pl.BlockSpec((s_dim, HEAD_DIM), lambda h, i: (0, off + h))
    return pl.pallas_call(
        _sb_kernel,
        grid=(SB_HEADS, s_dim // blk),
        in_specs=[pl.BlockSpec((blk, HEAD_DIM), lambda h, i: (i, h)), kv(SB_HEADS), kv(2 * SB_HEADS)],
        out_specs=pl.BlockSpec((blk, HEAD_DIM), lambda h, i: (i, h)),
        out_shape=jax.ShapeDtypeStruct((s_dim, SB_W), BF16),
        scratch_shapes=[pltpu.VMEM((s_dim, HEAD_DIM), BF16), pltpu.VMEM((s_dim, HEAD_DIM), BF16)],
        compiler_params=_cparams("parallel", "arbitrary"),
        name="stick_breaking",
    )(p, p, p)


RET_CHUNK = 512


def _ret_kernel(lg_ref, q_ref, k_ref, v_ref, g_ref, cos_ref, sin_ref, o_ref, state):
    h = pl.program_id(0)
    c = pl.program_id(1)
    n = RET_CHUNK

    @pl.when(c == 0)
    def _():
        state[...] = jnp.zeros_like(state)

    lg = lg_ref[h]
    cos, sin = cos_ref[...], sin_ref[...]
    q = _rope_tile(q_ref[...], cos, sin, RET_QK_DIM // 2)
    k = _rope_tile(k_ref[...], cos, sin, RET_QK_DIM // 2) * (RET_QK_DIM ** -0.5)
    v = v_ref[...].astype(BF16)

    ii = lax.broadcasted_iota(I32, (n, n), 0)
    jj = lax.broadcasted_iota(I32, (n, n), 1)
    rel = (ii - jj).astype(F32)
    decay = jnp.where(rel >= 0.0, jnp.exp(lg * jnp.maximum(rel, 0.0)), 0.0)
    inner = _dot_nt(q.astype(BF16), k.astype(BF16)) * decay
    out = _dot(inner.astype(BF16), v)

    pos = lax.broadcasted_iota(I32, (n, 1), 0).astype(F32)
    q_decay = jnp.exp(lg * (pos + 1.0))
    k_decay = jnp.exp(lg * (n - 1.0 - pos))
    prev = state[...]
    out = out + _dot((q * q_decay).astype(BF16), prev.astype(BF16))
    kd_t = jnp.transpose(k * k_decay).astype(BF16)
    state[...] = jnp.exp(lg * jnp.full((1, 1), n, F32)) * prev + _dot(kd_t, v)

    mu = jnp.mean(out, axis=-1, keepdims=True)
    d = out - mu
    var = jnp.mean(d * d, axis=-1, keepdims=True)
    g = g_ref[...]
    o_ref[...] = (g * _sigmoid(g) * (d * lax.rsqrt(var + LN_EPS))).astype(o_ref.dtype)


def _retention(p, log_gamma, cos, sin):
    s_dim = p.shape[0]
    n = RET_CHUNK
    qk = lambda off: pl.BlockSpec((n, RET_QK_DIM), lambda h, c, lg: (c, off + h))
    vg = lambda off: pl.BlockSpec((n, RET_V_DIM), lambda h, c, lg: (c, off + h))
    tab = pl.BlockSpec((n, LANES), lambda h, c, lg: (c, 0))
    q_off = 3 * SB_W // RET_QK_DIM
    v_off = (3 * SB_W + 2 * RET_QK_W) // RET_V_DIM
    grid_spec = pltpu.PrefetchScalarGridSpec(
        num_scalar_prefetch=1,
        grid=(RET_HEADS, s_dim // n),
        in_specs=[qk(q_off), qk(q_off + RET_HEADS), vg(v_off), vg(v_off + RET_HEADS), tab, tab],
        out_specs=pl.BlockSpec((n, RET_V_DIM), lambda h, c, lg: (c, h)),
        scratch_shapes=[pltpu.VMEM((RET_QK_DIM, RET_V_DIM), F32)],
    )
    return pl.pallas_call(
        _ret_kernel,
        grid_spec=grid_spec,
        out_shape=jax.ShapeDtypeStruct((s_dim, RET_V_W), BF16),
        compiler_params=_cparams("parallel", "arbitrary"),
        name="retention",
    )(log_gamma, p, p, p, p, cos, sin)


def _rope_cast_kernel(x_ref, cos_ref, sin_ref, o_ref, *, half, scale):
    cos, sin = cos_ref[...], sin_ref[...]
    for t in range(x_ref.shape[1] // LANES):
        sl = slice(t * LANES, (t + 1) * LANES)
        o_ref[:, sl] = (_rope_tile(x_ref[:, sl], cos, sin, half) * scale).astype(o_ref.dtype)


def _rope_cast(p, col0, n_cols, cos, sin, *, half, scale, tm=512, tn=1024):
    s_dim = p.shape[0]
    tm = min(tm, s_dim)
    tn = min(tn, n_cols)
    tab = pl.BlockSpec((tm, LANES), lambda i, j: (i, 0))
    return pl.pallas_call(
        functools.partial(_rope_cast_kernel, half=half, scale=scale),
        grid=(s_dim // tm, n_cols // tn),
        in_specs=[pl.BlockSpec((tm, tn), lambda i, j: (i, col0 // tn + j)), tab, tab],
        out_specs=pl.BlockSpec((tm, tn), lambda i, j: (i, j)),
        out_shape=jax.ShapeDtypeStruct((s_dim, n_cols), BF16),
        compiler_params=_cparams("parallel", "parallel"),
        name="rope_cast",
    )(p, cos, sin)


def _cast_kernel(x_ref, o_ref):
    o_ref[...] = x_ref[...].astype(o_ref.dtype)


def _cast(p, col0, n_cols, *, tm=512, tn=1024):
    s_dim = p.shape[0]
    tm = min(tm, s_dim)
    return pl.pallas_call(
        _cast_kernel,
        grid=(s_dim // tm, n_cols // tn),
        in_specs=[pl.BlockSpec((tm, tn), lambda i, j: (i, col0 // tn + j))],
        out_specs=pl.BlockSpec((tm, tn), lambda i, j: (i, j)),
        out_shape=jax.ShapeDtypeStruct((s_dim, n_cols), BF16),
        compiler_params=_cparams("parallel", "parallel"),
        name="cast_bf16",
    )(p)


def _idx_q_kernel(x_ref, cos_ref, sin_ref, o_ref):
    cos, sin = cos_ref[...], sin_ref[...]
    lane = lax.broadcasted_iota(I32, cos.shape, 1)
    low = lane < IDX_DIM
    for t in range(x_ref.shape[1] // LANES):
        y = _rope_tile(x_ref[:, t * LANES:(t + 1) * LANES], cos, sin, IDX_DIM // ROT_FRACTION // 2, IDX_DIM)
        o_ref[:, (2 * t) * LANES:(2 * t + 1) * LANES] = jnp.where(low, y, 0.0).astype(o_ref.dtype)
        o_ref[:, (2 * t + 1) * LANES:(2 * t + 2) * LANES] = jnp.where(
            low, pltpu.roll(y, IDX_DIM, 1), 0.0).astype(o_ref.dtype)


def _idx_q_prep(p, col0, cos, sin, *, tm=512):
    s_dim = p.shape[0]
    tm = min(tm, s_dim)
    tab = pl.BlockSpec((tm, LANES), lambda i: (i, 0))
    return pl.pallas_call(
        _idx_q_kernel,
        grid=(s_dim // tm,),
        in_specs=[pl.BlockSpec((tm, IDX_Q_W), lambda i: (i, col0 // IDX_Q_W)), tab, tab],
        out_specs=pl.BlockSpec((tm, IDX_HEADS * LANES), lambda i: (i, 0)),
        out_shape=jax.ShapeDtypeStruct((s_dim, IDX_HEADS * LANES), BF16),
        compiler_params=_cparams("parallel"),
        name="idx_q_prep",
    )(p, cos, sin)


def _idx_tail_kernel(x_ref, cos_ref, sin_ref, k_ref, w_ref):
    x = x_ref[...]
    lane = lax.broadcasted_iota(I32, x.shape, 1)
    y = _rope_tile(x, cos_ref[...], sin_ref[...], IDX_DIM // ROT_FRACTION // 2, IDX_DIM) * (IDX_DIM ** -0.5)
    k_ref[...] = jnp.where(lane < IDX_DIM, y, 0.0).astype(k_ref.dtype)
    w_ref[...] = x * (IDX_HEADS ** -0.5)


def _idx_tail_prep(tail, cos, sin, *, tm=512):
    s_dim = tail.shape[0]
    tm = min(tm, s_dim)
    blk = pl.BlockSpec((tm, LANES), lambda i: (i, 0))
    return pl.pallas_call(
        _idx_tail_kernel,
        grid=(s_dim // tm,),
        in_specs=[blk, blk, blk],
        out_specs=[blk, blk],
        out_shape=[jax.ShapeDtypeStruct((s_dim, LANES), BF16), jax.ShapeDtypeStruct((s_dim, LANES), F32)],
        compiler_params=_cparams("parallel"),
        name="idx_tail_prep",
    )(tail, cos, sin)


DSA_QB = 128
DSA_KB = 512
DSA_AQB = 512


def _select_kernel(qi_ref, ki_ref, w_ref, o_ref, keys_ref, *, topk, n_kblocks, idx_bits):
    i = pl.program_id(0)
    qb, kb = DSA_QB, DSA_KB
    n_live = ((i + 1) * qb + kb - 1) // kb
    q_pos = i * qb + lax.broadcasted_iota(I32, (qb, kb), 0)
    col0 = lax.broadcasted_iota(I32, (qb, kb), 1)
    w = w_ref[...]

    def score_block(jb, _):
        start = pl.multiple_of(jb * kb, kb)
        kj = ki_ref[pl.ds(start, kb), :]
        acc = jnp.zeros((qb, kb), F32)
        for h in range(IDX_HEADS):
            z = _dot_nt(qi_ref[:, h * LANES:(h + 1) * LANES], kj)
            acc = acc + jnp.maximum(z, 0.0) * w[:, IDX_DIM + h:IDX_DIM + h + 1]
        acc = jnp.where(acc == 0.0, 0.0, acc)
        bits = lax.bitcast_convert_type(acc, I32)
        key = jnp.where(bits >= 0, bits, bits ^ 0x7FFFFFFF)
        keys_ref[jb] = jnp.where(jb * kb + col0 <= q_pos, key, INT_MIN)
        return 0

    lax.fori_loop(0, n_live, score_block, 0)

    def count(pred):
        def body(jb, part):
            kk = keys_ref[jb]
            hit = jnp.where(pred(kk, jb * kb + col0), 1.0, 0.0)
            for t in range(kb // LANES):
                part = part + hit[:, t * LANES:(t + 1) * LANES]
            return part
        part = lax.fori_loop(0, n_live, body, jnp.zeros((qb, LANES), F32))
        return jnp.sum(part, axis=1, keepdims=True)

    k_f = float(topk)
    c_nonneg = count(lambda kk, cc: kk >= 0)
    thr = jnp.where(c_nonneg >= k_f, jnp.int32(0), jnp.int32(INT_MIN))
    c_thr = jnp.where(c_nonneg >= k_f, c_nonneg, float(n_kblocks * kb))
    short = i * qb + lax.broadcasted_iota(I32, (qb, 1), 0) + 1 < topk

    def unsettled(c_thr):
        return jnp.max(jnp.where((c_thr == k_f) | short, 0.0, 1.0)) > 0.0

    def thr_cond(carry):
        return jnp.logical_and(carry[0] < 31, unsettled(carry[2]))

    def thr_bit(carry):
        b, thr, c_thr = carry
        cand = thr + jnp.left_shift(jnp.int32(1), 30 - b)
        c = count(lambda kk, cc: kk >= cand)
        ok = c >= k_f
        return b + 1, jnp.where(ok, cand, thr), jnp.where(ok, c, c_thr)

    _, thr, c_thr = lax.while_loop(thr_cond, thr_bit, (jnp.int32(0), thr, c_thr))

    def no_cut():
        return jnp.full((qb, 1), n_kblocks * kb, I32)

    def tie_cut():
        need = k_f - count(lambda kk, cc: kk > thr)
        n_eq = count(lambda kk, cc: kk == thr)
        tied = (n_eq > need) & (thr != INT_MIN)

        def cut_bit(b, cut):
            cand = cut + jnp.left_shift(jnp.int32(1), idx_bits - 1 - b)
            c = count(lambda kk, cc: (kk == thr) & (cc < cand))
            return jnp.where(c < need, cand, cut)

        def cut_search():
            return lax.fori_loop(0, idx_bits, cut_bit, jnp.zeros((qb, 1), I32))

        return lax.cond(jnp.max(jnp.where(tied, 1.0, 0.0)) > 0.0, cut_search, no_cut)

    cut = lax.cond(unsettled(c_thr), tie_cut, no_cut)

    def write_live(jb, _):
        kk = keys_ref[jb]
        cc = jb * kb + col0
        sel = ((kk > thr) | ((kk == thr) & (cc <= cut))) & (kk != INT_MIN)
        o_ref[jb] = jnp.where(sel, 0.0, MASK_NEG).astype(o_ref.dtype)
        return 0

    lax.fori_loop(0, n_live, write_live, 0)

    def write_dead(jb, _):
        o_ref[jb] = jnp.full((qb, kb), MASK_NEG, o_ref.dtype)
        return 0

    lax.fori_loop(n_live, n_kblocks, write_dead, 0)


def _dsa_select(qi, ki, w, topk):
    s_dim = qi.shape[0]
    qb, kb = DSA_QB, DSA_KB
    n_kblocks = s_dim // kb
    idx_bits = max(1, (s_dim - 1).bit_length())
    return pl.pallas_call(
        functools.partial(_select_kernel, topk=topk, n_kblocks=n_kblocks, idx_bits=idx_bits),
        grid=(s_dim // qb,),
        in_specs=[
            pl.BlockSpec((qb, IDX_HEADS * LANES), lambda i: (i, 0)),
            pl.BlockSpec((s_dim, LANES), lambda i: (0, 0)),
            pl.BlockSpec((qb, LANES), lambda i: (i, 0)),
        ],
        out_specs=pl.BlockSpec((None, n_kblocks, qb, kb), lambda i: (i, 0, 0, 0)),
        out_shape=jax.ShapeDtypeStruct((s_dim // qb, n_kblocks, qb, kb), BF16),
        scratch_shapes=[pltpu.VMEM((n_kblocks, qb, kb), I32)],
        compiler_params=_cparams("parallel"),
        name="dsa_select",
    )(qi, ki, w)


def _dsa_attn_kernel(q_ref, k_ref, v_ref, b_ref, o_ref):
    i = pl.program_id(1)
    qb, kb, grp = DSA_AQB, DSA_KB, DSA_GROUP
    n_sel = qb // DSA_QB
    n_live = ((i + 1) * qb + kb - 1) // kb
    q = jnp.concatenate([q_ref[:, r * HEAD_DIM:(r + 1) * HEAD_DIM] for r in range(grp)], axis=0)

    def body(jb, carry):
        m, l, acc = carry
        start = pl.multiple_of(jb * kb, kb)
        kj = k_ref[pl.ds(start, kb), :]
        vj = v_ref[pl.ds(start, kb), :]
        z = _dot_nt(q, kj).astype(BF16)
        bias = b_ref[:, jb].reshape(1, qb, kb)
        z = (z.reshape(grp, qb, kb) + bias).reshape(grp * qb, kb)
        m_new = jnp.maximum(m, jnp.max(z, axis=1, keepdims=True).astype(F32))
        p = jnp.exp2(z - m_new.astype(BF16))
        alpha = jnp.exp2(m - m_new)
        l = alpha * l + jnp.sum(p.astype(F32), axis=1, keepdims=True)
        acc = alpha * acc + _dot(p, vj)
        return m_new, l, acc

    rows = grp * qb
    init = (jnp.full((rows, 1), MASK_NEG, F32), jnp.zeros((rows, 1), F32), jnp.zeros((rows, HEAD_DIM), F32))
    _, l, acc = lax.fori_loop(0, n_live, body, init)
    out = acc / l
    for r in range(grp):
        o_ref[:, r * HEAD_DIM:(r + 1) * HEAD_DIM] = out[r * qb:(r + 1) * qb, :].astype(o_ref.dtype)


def _dsa_attention(q, k, v, bias):
    s_dim = q.shape[0]
    qb, kb = DSA_AQB, DSA_KB
    gw = DSA_GROUP * HEAD_DIM
    kv = pl.BlockSpec((s_dim, HEAD_DIM), lambda g, i: (0, g))
    return pl.pallas_call(
        _dsa_attn_kernel,
        grid=(DSA_KV_HEADS, s_dim // qb),
        in_specs=[
            pl.BlockSpec((qb, gw), lambda g, i: (i, g)),
            kv, kv,
            pl.BlockSpec((qb // DSA_QB, s_dim // kb, DSA_QB, kb), lambda g, i: (i, 0, 0, 0)),
        ],
        out_specs=pl.BlockSpec((qb, gw), lambda g, i: (i, g)),
        out_shape=jax.ShapeDtypeStruct((s_dim, DSA_Q_W), BF16),
        compiler_params=_cparams("parallel", "arbitrary"),
        name="dsa_attention",
    )(q, k, v, bias)


ROUTE_E1, ROUTE_E2, ROUTE_G1, ROUTE_G2, ROUTE_R1, ROUTE_R2 = range(6)


def _router_kernel(x_ref, w_ref, o_ref, cnt_ref, seen):
    @pl.when(pl.program_id(0) == 0)
    def _():
        seen[...] = jnp.zeros_like(seen)

    x = x_ref[...]
    w = w_ref[...]
    xh = x.astype(BF16)
    xl = (x - xh.astype(F32)).astype(BF16)
    wh = w.astype(BF16)
    wl = (w - wh.astype(F32)).astype(BF16)
    logits = _dot(xh, wh) + (_dot(xh, wl) + _dot(xl, wh))
    lane = lax.broadcasted_iota(I32, logits.shape, 1).astype(F32)
    logits = jnp.where(lane < N_EXPERTS, logits, -jnp.inf)
    v1 = jnp.max(logits, axis=1, keepdims=True)
    i1 = jnp.min(jnp.where(logits == v1, lane, float(LANES)), axis=1, keepdims=True)
    rest = jnp.where(lane == i1, -jnp.inf, logits)
    v2 = jnp.max(rest, axis=1, keepdims=True)
    i2 = jnp.min(jnp.where(rest == v2, lane, float(LANES)), axis=1, keepdims=True)
    e2 = jnp.exp(v2 - v1)
    g1 = 1.0 / (1.0 + e2)
    g2 = e2 / (1.0 + e2)

    tb = x.shape[0]
    member = jnp.where((lane == i1) | (lane == i2), 1.0, 0.0)
    earlier = jnp.where(lax.broadcasted_iota(I32, (tb, tb), 1) < lax.broadcasted_iota(I32, (tb, tb), 0), 1.0, 0.0)
    prefix = _dot(earlier.astype(BF16), member.astype(BF16)) + seen[0:1, :]
    r1 = jnp.sum(jnp.where(lane == i1, prefix, 0.0), axis=1, keepdims=True)
    r2 = jnp.sum(jnp.where(lane == i2, prefix, 0.0), axis=1, keepdims=True)
    seen[...] = seen[...] + jnp.sum(member, axis=0, keepdims=True)
    cnt_ref[...] = seen[...]

    rec = jnp.zeros_like(logits)
    for slot, val in ((ROUTE_E1, i1), (ROUTE_E2, i2), (ROUTE_G1, g1), (ROUTE_G2, g2), (ROUTE_R1, r1), (ROUTE_R2, r2)):
        rec = jnp.where(lane == float(slot), val, rec)
    o_ref[...] = rec


def _router(x, w_pad, *, tm=256):
    s_dim, d_dim = x.shape
    tm = min(tm, s_dim)
    return pl.pallas_call(
        _router_kernel,
        grid=(s_dim // tm,),
        in_specs=[pl.BlockSpec((tm, d_dim), lambda i: (i, 0)), pl.BlockSpec((d_dim, LANES), lambda i: (0, 0))],
        out_specs=[pl.BlockSpec((tm, LANES), lambda i: (i, 0)), pl.BlockSpec((8, LANES), lambda i: (0, 0))],
        out_shape=[jax.ShapeDtypeStruct((s_dim, LANES), F32), jax.ShapeDtypeStruct((8, LANES), F32)],
        scratch_shapes=[pltpu.VMEM((8, LANES), F32)],
        compiler_params=_cparams("arbitrary"),
        name="moe_router",
    )(x, w_pad)


MOE_TM = 512


def _moe_plan(route, counts, s_dim):
    tm = min(MOE_TM, s_dim)
    n_rows = 2 * s_dim + N_EXPERTS * tm
    e1 = route[:, ROUTE_E1].astype(I32)
    e2 = route[:, ROUTE_E2].astype(I32)
    cnt = counts[0, :N_EXPERTS].astype(I32)
    padded = (cnt + tm - 1) // tm * tm
    ends = jnp.cumsum(padded)
    starts = ends - padded
    dest1 = starts[e1] + route[:, ROUTE_R1].astype(I32)
    dest2 = starts[e2] + route[:, ROUTE_R2].astype(I32)
    tok = jnp.arange(s_dim, dtype=I32)
    src_tok = jnp.zeros((n_rows,), I32).at[dest1].set(tok).at[dest2].set(tok)
    tile_start = jnp.arange(n_rows // tm, dtype=I32) * tm
    tile_expert = jnp.minimum(jnp.searchsorted(ends, tile_start, side="right"), N_EXPERTS - 1).astype(I32)
    n_used = (ends[-1:] // tm).astype(I32)
    return dict(tm=tm, n_rows=n_rows, dest=jnp.concatenate([dest1, dest2]), src_tok=src_tok,
                tile_expert=tile_expert, n_used=n_used)


def _row_copy(src_ref, src_row, dst_ref, dst_row, sem):
    return pltpu.make_async_copy(src_ref.at[pl.ds(src_row, 1), :], dst_ref.at[pl.ds(dst_row, 1), :], sem)


def _dispatch_kernel(src_tok_ref, x_ref, o_ref, buf, sem):
    i = pl.program_id(0)
    n = pl.num_programs(0)
    tm = buf.shape[1]

    def fetch(tile, slot):
        def issue(pair, _):
            for prio in range(2):
                r = 2 * pair + prio
                _row_copy(x_ref, src_tok_ref[tile * tm + r], buf.at[slot], r, sem.at[slot]).start(priority=prio)
            return 0
        lax.fori_loop(0, tm // 2, issue, 0)

    @pl.when(i == 0)
    def _():
        fetch(0, 0)

    @pl.when(i + 1 < n)
    def _():
        fetch(i + 1, (i + 1) % 2)

    slot = i % 2
    pltpu.make_async_copy(x_ref.at[pl.ds(0, tm), :], buf.at[slot], sem.at[slot]).wait()
    o_ref[...] = buf[slot].astype(o_ref.dtype)


def _moe_dispatch(x, plan, *, tm=512):
    s_dim, d_dim = x.shape
    tm = min(tm, s_dim)
    n_rows = plan["n_rows"]
    return pl.pallas_call(
        _dispatch_kernel,
        grid_spec=pltpu.PrefetchScalarGridSpec(
            num_scalar_prefetch=1,
            grid=(n_rows // tm,),
            in_specs=[pl.BlockSpec(memory_space=pl.ANY)],
            out_specs=pl.BlockSpec((tm, d_dim), lambda i, st: (i, 0)),
            scratch_shapes=[pltpu.VMEM((2, tm, d_dim), F32), pltpu.SemaphoreType.DMA((2,))],
        ),
        out_shape=jax.ShapeDtypeStruct((n_rows, d_dim), BF16),
        compiler_params=_cparams("arbitrary"),
        name="moe_dispatch",
    )(plan["src_tok"], x)


def _moe_up_kernel(te_ref, nu_ref, a_ref, wg_ref, wu_ref, o_ref):
    m = pl.program_id(1)

    @pl.when(m < nu_ref[0])
    def _():
        a = a_ref[...]
        g = _dot(a, wg_ref[...].astype(BF16))
        u = _dot(a, wu_ref[...].astype(BF16))
        o_ref[...] = (g * _sigmoid(g) * u).astype(o_ref.dtype)

    @pl.when(m >= nu_ref[0])
    def _():
        o_ref[...] = jnp.zeros_like(o_ref)


def _moe_up(xs, wg, wu, layer, plan, *, tn=256):
    n_rows, d_dim = xs.shape
    tm = plan["tm"]
    w_spec = pl.BlockSpec((None, None, d_dim, tn), lambda n, m, te, nu: (layer, te[m], 0, n))
    return pl.pallas_call(
        _moe_up_kernel,
        grid_spec=pltpu.PrefetchScalarGridSpec(
            num_scalar_prefetch=2,
            grid=(D_FF_EXPERT // tn, n_rows // tm),
            in_specs=[pl.BlockSpec((tm, d_dim), lambda n, m, te, nu: (m, 0)), w_spec, w_spec],
            out_specs=pl.BlockSpec((tm, tn), lambda n, m, te, nu: (m, n)),
        ),
        out_shape=jax.ShapeDtypeStruct((n_rows, D_FF_EXPERT), BF16),
        compiler_params=_cparams("parallel", "arbitrary"),
        name="moe_up",
    )(plan["tile_expert"], plan["n_used"], xs, wg, wu)


def _moe_down_kernel(te_ref, nu_ref, a_ref, w_ref, o_ref):
    m = pl.program_id(1)

    @pl.when(m < nu_ref[0])
    def _():
        o_ref[...] = _dot(a_ref[...], w_ref[...].astype(BF16))

    @pl.when(m >= nu_ref[0])
    def _():
        o_ref[...] = jnp.zeros_like(o_ref)


def _moe_down(hid, wd, layer, plan, *, tn=2048):
    n_rows, f_dim = hid.shape
    tm = plan["tm"]
    d_dim = wd.shape[-1]
    return pl.pallas_call(
        _moe_down_kernel,
        grid_spec=pltpu.PrefetchScalarGridSpec(
            num_scalar_prefetch=2,
            grid=(d_dim // tn, n_rows // tm),
            in_specs=[pl.BlockSpec((tm, f_dim), lambda n, m, te, nu: (m, 0)),
                      pl.BlockSpec((None, None, f_dim, tn), lambda n, m, te, nu: (layer, te[m], 0, n))],
            out_specs=pl.BlockSpec((tm, tn), lambda n, m, te, nu: (m, n)),
        ),
        out_shape=jax.ShapeDtypeStruct((n_rows, d_dim), F32),
        compiler_params=_cparams("parallel", "arbitrary"),
        name="moe_down",
    )(plan["tile_expert"], plan["n_used"], hid, wd)


def _combine_ln_kernel(dest_ref, x_ref, r_ref, ys_ref, g_ref, b_ref, o_ref, ob_ref, buf, sem, *, s_dim):
    i = pl.program_id(0)
    n = pl.num_programs(0)
    tb = x_ref.shape[0]

    def fetch(blk, slot):
        def issue(r, _):
            t = blk * tb + r
            _row_copy(ys_ref, dest_ref[t], buf.at[slot, 0], r, sem.at[slot]).start(priority=0)
            _row_copy(ys_ref, dest_ref[s_dim + t], buf.at[slot, 1], r, sem.at[slot]).start(priority=1)
            return 0
        lax.fori_loop(0, tb, issue, 0)

    @pl.when(i == 0)
    def _():
        fetch(0, 0)

    @pl.when(i + 1 < n)
    def _():
        fetch(i + 1, (i + 1) % 2)

    slot = i % 2
    for k in range(2):
        pltpu.make_async_copy(ys_ref.at[pl.ds(0, tb), :], buf.at[slot, k], sem.at[slot]).wait()
    rec = r_ref[...]
    f = rec[:, ROUTE_G1:ROUTE_G1 + 1] * buf[slot, 0] + rec[:, ROUTE_G2:ROUTE_G2 + 1] * buf[slot, 1]
    y = DEEPNORM_ALPHA * x_ref[...] + f
    mu = jnp.mean(y, axis=-1, keepdims=True)
    d = y - mu
    var = jnp.mean(d * d, axis=-1, keepdims=True)
    out = d * lax.rsqrt(var + LN_EPS) * g_ref[...] + b_ref[...]
    o_ref[...] = out
    ob_ref[...] = out.astype(BF16)


def _moe_combine_ln(x, route, ys, plan, g3, b3, idx, *, tb=256):
    s_dim, d_dim = x.shape
    tb = min(tb, s_dim)
    row = pl.BlockSpec((tb, d_dim), lambda i, de: (i, 0))
    par = pl.BlockSpec((None, 1, d_dim), lambda i, de: (idx, 0, 0))
    return pl.pallas_call(
        functools.partial(_combine_ln_kernel, s_dim=s_dim),
        grid_spec=pltpu.PrefetchScalarGridSpec(
            num_scalar_prefetch=1,
            grid=(s_dim // tb,),
            in_specs=[row, pl.BlockSpec((tb, LANES), lambda i, de: (i, 0)), pl.BlockSpec(memory_space=pl.ANY),
                      par, par],
            out_specs=[row, row],
            scratch_shapes=[pltpu.VMEM((2, 2, tb, d_dim), F32), pltpu.SemaphoreType.DMA((2,))],
        ),
        out_shape=[jax.ShapeDtypeStruct((s_dim, d_dim), F32), jax.ShapeDtypeStruct((s_dim, d_dim), BF16)],
        compiler_params=_cparams("arbitrary"),
        name="moe_combine_ln",
    )(plan["dest"], x, route, ys, g3, b3)


def _even_layer(x, xb, j, i, even_w_in, even_w_out, ffn_w_gate, ffn_w_up, ffn_w_down, ln_g3, ln_b3, tabs):
    d = D_MODEL
    p = _matmul_rows([xb], even_w_in, lambda n: (j, 0, n), EVEN_IN, name="even_in_proj")
    a = _stick_breaking(p)
    r = _retention(p, tabs["log_gamma"], tabs["ret_cos"], tabs["ret_sin"])
    h = _matmul_rows([a, r], even_w_out, lambda n: (j, 0, n), d, name="even_out_proj")
    x, xb = _deepnorm_ln(x, h, ln_g3, ln_b3, 2 * i)
    hid = _swiglu_rows(xb, ffn_w_gate, ffn_w_up, lambda n: (j, 0, n), D_FF, name="ffn_up")
    f = _matmul_rows([hid], ffn_w_down, lambda n: (j, 0, n), d, name="ffn_down")
    return _deepnorm_ln(x, f, ln_g3, ln_b3, 2 * i + 1)


def _odd_layer(x, xb, j, i, odd_w_in, odd_w_out, moe_w_router, moe_w_gate, moe_w_up, moe_w_down,
               ln_g3, ln_b3, tabs):
    d = D_MODEL
    s_dim = x.shape[0]
    w_nk = jnp.swapaxes(odd_w_in, 1, 2)
    p = _matmul_rows([xb], w_nk, lambda n: (j, n, 0), ODD_MAIN, name="odd_in_proj", w_is_nk=True)
    w_tail = jnp.pad(w_nk[j, ODD_MAIN:, :], ((0, LANES - ODD_TAIL), (0, 0)))
    tail = _matmul_rows([xb], w_tail, lambda n: (n, 0), LANES, name="odd_in_tail", w_is_nk=True)

    cos_h, sin_h = tabs["head_cos"], tabs["head_sin"]
    q = _rope_cast(p, 0, DSA_Q_W, cos_h, sin_h, half=HEAD_DIM // ROT_FRACTION // 2,
                   scale=HEAD_DIM ** -0.5 * LOG2_E)
    k = _rope_cast(p, DSA_Q_W, DSA_KV_W, cos_h, sin_h, half=HEAD_DIM // ROT_FRACTION // 2, scale=1.0)
    v = _cast(p, DSA_Q_W + DSA_KV_W, DSA_KV_W)
    qi = _idx_q_prep(p, DSA_Q_W + 2 * DSA_KV_W, tabs["idx_cos"], tabs["idx_sin"])
    ki, wi = _idx_tail_prep(tail, tabs["tail_cos"], tabs["tail_sin"])

    bias = _dsa_select(qi, ki, wi, min(IDX_TOPK_MAX, s_dim // 4))
    y = _dsa_attention(q, k, v, bias)
    h = _matmul_rows([y], odd_w_out, lambda n: (j, 0, n), d, name="odd_out_proj")
    x, xb = _deepnorm_ln(x, h, ln_g3, ln_b3, 2 * i)

    w_router = jnp.pad(moe_w_router[j], ((0, 0), (0, LANES - N_EXPERTS)))
    route, counts = _router(x, w_router)
    plan = _moe_plan(route, counts, s_dim)
    xs = _moe_dispatch(x, plan)
    hid = _moe_up(xs, moe_w_gate, moe_w_up, j, plan)
    ys = _moe_down(hid, moe_w_down, j, plan)
    return _moe_combine_ln(x, route, ys, plan, ln_g3, ln_b3, 2 * i + 1)


def kernel(x, even_w_in, even_w_out, odd_w_in, odd_w_out, ffn_w_gate, ffn_w_up, ffn_w_down, moe_w_router,
           moe_w_gate, moe_w_up, moe_w_down, ln_g, ln_b):
    batch, s_dim, d = x.shape
    ln_g3 = ln_g.reshape(2 * DEPTH, 1, d)
    ln_b3 = ln_b.reshape(2 * DEPTH, 1, d)
    ret_cos, ret_sin = _rope_tables(s_dim, LANES, RET_QK_DIM, RET_THETA)
    head_cos, head_sin = _rope_tables(s_dim, LANES, HEAD_DIM // ROT_FRACTION, ROPE_THETA)
    idx_cos, idx_sin = _rope_tables(s_dim, IDX_DIM, IDX_DIM // ROT_FRACTION, ROPE_THETA)
    tail_cos, tail_sin = _rope_tables(s_dim, IDX_DIM, IDX_DIM // ROT_FRACTION, ROPE_THETA, active=IDX_DIM)
    tabs = dict(
        log_gamma=jnp.log1p(-jnp.exp2(-5.0 - jnp.arange(RET_HEADS, dtype=F32))),
        ret_cos=ret_cos, ret_sin=ret_sin, head_cos=head_cos, head_sin=head_sin,
        idx_cos=idx_cos, idx_sin=idx_sin, tail_cos=tail_cos, tail_sin=tail_sin,
    )
    outs = []
    for b in range(batch):
        xs = x[b] if batch > 1 else x.reshape(s_dim, d)
        xb = xs.astype(BF16)
        for i in range(DEPTH):
            j = i // 2
            if i % 2 == 0:
                xs, xb = _even_layer(xs, xb, j, i, even_w_in, even_w_out, ffn_w_gate, ffn_w_up, ffn_w_down,
                                     ln_g3, ln_b3, tabs)
            else:
                xs, xb = _odd_layer(xs, xb, j, i, odd_w_in, odd_w_out, moe_w_router, moe_w_gate, moe_w_up,
                                    moe_w_down, ln_g3, ln_b3, tabs)
        outs.append(xs)
    return jnp.stack(outs, axis=0) if batch > 1 else outs[0].reshape(1, s_dim, d)
```
